```python
import jax, jax.numpy as jnp
from jax import lax
import numpy as np

D_MODEL = 2048
BATCH = 4
SEQ = 4096
DEPTH = 4
DEC_BATCH = 32
DEC_SEQ = 64
PAST_LEN = 4096

CHUNK = 64
WINDOW = 128
WIN_CHUNKS = WINDOW // CHUNK
HEAD_DIM = 64
N_Q_HEADS = 16
N_KV_HEADS = 4
GQA_GROUP = N_Q_HEADS // N_KV_HEADS
ATTN_WIDTH = N_Q_HEADS * HEAD_DIM
KV_WIDTH = N_KV_HEADS * HEAD_DIM
GLA_HEADS = 4
GLA_DK = 512
GLA_DV = 1024
GLA_HK = GLA_DK // GLA_HEADS
GLA_HV = GLA_DV // GLA_HEADS
GLA_RANK = 16
GLA_TAU = 16.0
NORM_EPS = 1e-6
IN_SIZES = (ATTN_WIDTH, KV_WIDTH, KV_WIDTH, ATTN_WIDTH,
            GLA_DK, GLA_DK, GLA_DV, GLA_DV, GLA_RANK, D_MODEL, D_MODEL)
IN_COLS = sum(IN_SIZES)

kernel_name = 'hybrid_swa_sink_gla_stream_step'


def rms_norm(x, g):
    xf = x.astype(jnp.float32)
    y = xf * lax.rsqrt(jnp.mean(xf * xf, axis=-1, keepdims=True) + NORM_EPS)
    return (y * g.astype(jnp.float32)).astype(x.dtype)


def split_cols(p):
    idx = []
    acc = 0
    for s in IN_SIZES[:-1]:
        acc += s
        idx.append(acc)
    return jnp.split(p, idx, axis=-1)


def alibi_slopes():
    h = jnp.arange(1, N_Q_HEADS + 1, dtype=jnp.float32)
    return jnp.exp2(-8.0 * h / N_Q_HEADS).reshape(N_KV_HEADS, GQA_GROUP)


def banded_attention(q, k, v, sinks, q_pos, k_pos, k_valid):
    s = jnp.einsum('bnqkgd,bnskd->bnkgqs', q, k).astype(jnp.float32) * (HEAD_DIM ** -0.5)
    dist = jnp.abs(q_pos[:, None] - k_pos[None, :]).astype(jnp.float32)
    s = s - alibi_slopes()[:, :, None, None] * dist
    s = jnp.where(k_valid[None, :, None, None, None, :], s, -jnp.inf)
    sink = sinks.astype(jnp.float32).reshape(N_KV_HEADS, GQA_GROUP)[None, None, :, :, None, None]
    m = jnp.maximum(jnp.max(s, axis=-1, keepdims=True), sink)
    p = jnp.exp(s - m)
    p = p / (jnp.sum(p, axis=-1, keepdims=True) + jnp.exp(sink - m))
    return jnp.einsum('bnkgqs,bnskd->bnqkgd', p.astype(v.dtype), v)


def attn_prompt(q, k, v, sinks):
    B, S = q.shape[:2]
    nc = S // CHUNK
    qb = q.reshape(B, nc, CHUNK, N_KV_HEADS, GQA_GROUP, HEAD_DIM)
    pad = ((0, 0), (WIN_CHUNKS, 0), (0, 0), (0, 0), (0, 0))
    kp = jnp.pad(k.reshape(B, nc, CHUNK, N_KV_HEADS, HEAD_DIM), pad)
    vp = jnp.pad(v.reshape(B, nc, CHUNK, N_KV_HEADS, HEAD_DIM), pad)
    kb = jnp.concatenate([kp[:, j:j + nc] for j in range(WIN_CHUNKS + 1)], axis=2)
    vb = jnp.concatenate([vp[:, j:j + nc] for j in range(WIN_CHUNKS + 1)], axis=2)
    q_pos = WIN_CHUNKS * CHUNK + jnp.arange(CHUNK)
    k_pos = jnp.arange((WIN_CHUNKS + 1) * CHUNK)
    k_valid = (jnp.arange(nc)[:, None] - WIN_CHUNKS + k_pos[None, :] // CHUNK) >= 0
    o = banded_attention(qb, kb, vb, sinks, q_pos, k_pos, k_valid)
    return o.reshape(B, S, ATTN_WIDTH)


def attn_sample(q, k, v, cache_k, cache_v, sinks):
    B, T = q.shape[:2]
    L = cache_k.shape[1]
    k_all = jnp.concatenate([cache_k.astype(k.dtype), k], axis=1)
    v_all = jnp.concatenate([cache_v.astype(v.dtype), v], axis=1)
    q_pos = L + jnp.arange(T)
    k_pos = jnp.arange(L + T)
    k_valid = jnp.ones((1, L + T), dtype=bool)
    o = banded_attention(q[:, None], k_all[:, None], v_all[:, None], sinks, q_pos, k_pos, k_valid)
    return o.reshape(B, T, ATTN_WIDTH), k_all[:, T:], v_all[:, T:]


def gla_block(s0, blk):
    q, k, v, la = blk
    c = q.shape[2]
    b = jnp.cumsum(la, axis=2)
    causal = jnp.tril(jnp.ones((c, c), dtype=bool))[None, None, :, :, None]
    decay = jnp.where(causal, jnp.exp(jnp.minimum(b[:, :, :, None, :] - b[:, :, None, :, :], 0.0)), 0.0)
    scores = jnp.einsum('bhtd,bhsd,bhtsd->bhts', q, k, decay)
    o = jnp.einsum('bhts,bhsv->bhtv', scores, v) + jnp.einsum('bhtd,bhdv->bhtv', q * jnp.exp(b), s0)
    b_end = b[:, :, -1, :]
    s1 = jnp.exp(b_end)[..., None] * s0 + jnp.einsum('bhsd,bhsv->bhdv', k * jnp.exp(b_end[:, :, None, :] - b), v)
    return s1, o


def gla_mix(q, k, v, log_a, state):
    B, T, _ = q.shape
    blk = min(CHUNK, T)
    nb = T // blk

    def to_blocks(a, d):
        return a.astype(jnp.float32).reshape(B, nb, blk, GLA_HEADS, d).transpose(1, 0, 3, 2, 4)

    qb = to_blocks(q, GLA_HK) * (GLA_HK ** -0.5)
    kb = to_blocks(k, GLA_HK)
    vb = to_blocks(v, GLA_HV)
    lb = to_blocks(log_a, GLA_HK)
    if state is None:
        s0 = jnp.zeros((B, GLA_HEADS, GLA_HK, GLA_HV), jnp.float32)
    else:
        s0 = state.astype(jnp.float32)
    s1, ob = lax.scan(gla_block, s0, (qb, kb, vb, lb))
    o = ob.transpose(1, 0, 3, 2, 4).reshape(B, T, GLA_HEADS, GLA_HV)
    return o, s1


def trunk_layer(h, norm_pre, norm_post, w_in, b_gate, attn_sinks, w_forget_up, b_forget,
                gla_norm, w_branch_attn, w_branch_gla, w_out, cache_k, cache_v, state):
    B, T, _ = h.shape
    u = rms_norm(h, norm_pre)
    qa, ka, va, za, qg, kg, vg, zg, lr, ga, gg = split_cols(u @ w_in)
    qa = qa.reshape(B, T, N_KV_HEADS, GQA_GROUP, HEAD_DIM)
    ka = ka.reshape(B, T, N_KV_HEADS, HEAD_DIM)
    va = va.reshape(B, T, N_KV_HEADS, HEAD_DIM)
    if cache_k is None:
        oa = attn_prompt(qa, ka, va, attn_sinks)
        keep = min(WINDOW, T)
        new_k, new_v = ka[:, T - keep:], va[:, T - keep:]
    else:
        oa, new_k, new_v = attn_sample(qa, ka, va, cache_k, cache_v, attn_sinks)
    y_a = (oa * jax.nn.silu(za)) @ w_branch_attn
    log_a = jax.nn.log_sigmoid((lr @ w_forget_up + b_forget).astype(jnp.float32)) / GLA_TAU
    og, new_state = gla_mix(qg, kg, vg, log_a, state)
    og = rms_norm(og.astype(h.dtype), gla_norm).reshape(B, T, GLA_DV)
    y_g = (og * jax.nn.silu(zg)) @ w_branch_gla
    merged = jax.nn.sigmoid(ga + b_gate[:D_MODEL]) * y_a + jax.nn.sigmoid(gg + b_gate[D_MODEL:]) * y_g
    h = h + rms_norm(merged @ w_out, norm_post)
    return h, new_k, new_v, new_state.astype(h.dtype)


def setup_inputs(seed: int = 0) -> dict:
    key = jax.random.key(seed)
    ks = jax.random.split(key, 18)
    f32 = jnp.float32
    L = min(WINDOW, PAST_LEN)
    nrm = lambda k, shape: jax.random.normal(k, shape, f32)
    return {
        'x_prompt': nrm(ks[0], (BATCH, SEQ, D_MODEL)),
        'x_sample': nrm(ks[1], (DEC_BATCH, DEC_SEQ, D_MODEL)),
        'cache_k': nrm(ks[2], (DEPTH, DEC_BATCH, L, N_KV_HEADS, HEAD_DIM)),
        'cache_v': nrm(ks[3], (DEPTH, DEC_BATCH, L, N_KV_HEADS, HEAD_DIM)),
        'state_gla': nrm(ks[4], (DEPTH, DEC_BATCH, GLA_HEADS, GLA_HK, GLA_HV)),
        'norm_pre': 1.0 + 0.05 * nrm(ks[5], (DEPTH, D_MODEL)),
        'norm_post': 1.0 + 0.05 * nrm(ks[6], (DEPTH, D_MODEL)),
        'w_in': nrm(ks[7], (DEPTH, D_MODEL, IN_COLS)) * D_MODEL ** -0.5,
        'b_gate': 0.01 * nrm(ks[8], (DEPTH, 2 * D_MODEL)),
        'attn_sinks': 0.5 * nrm(ks[9], (DEPTH, N_Q_HEADS)),
        'w_forget_up': nrm(ks[10], (DEPTH, GLA_RANK, GLA_DK)) * GLA_RANK ** -0.5,
        'b_forget': 0.1 * nrm(ks[11], (DEPTH, GLA_DK)),
        'gla_norm': 1.0 + 0.05 * nrm(ks[12], (DEPTH, GLA_HV)),
        'w_branch_attn': nrm(ks[13], (DEPTH, ATTN_WIDTH, D_MODEL)) * ATTN_WIDTH ** -0.5,
        'w_branch_gla': nrm(ks[14], (DEPTH, GLA_DV, D_MODEL)) * GLA_DV ** -0.5,
        'w_out': nrm(ks[15], (DEPTH, D_MODEL, D_MODEL)) * D_MODEL ** -0.5,
    }


def reference(x_prompt, x_sample, cache_k, cache_v, state_gla, norm_pre, norm_post, w_in, b_gate,
              attn_sinks, w_forget_up, b_forget, gla_norm, w_branch_attn, w_branch_gla, w_out):
    h_p, h_s = x_prompt, x_sample
    kp_l, vp_l, sp_l, ks_l, vs_l, ss_l = [], [], [], [], [], []
    for l in range(DEPTH):
        w = (norm_pre[l], norm_post[l], w_in[l], b_gate[l], attn_sinks[l], w_forget_up[l],
             b_forget[l], gla_norm[l], w_branch_attn[l], w_branch_gla[l], w_out[l])
        h_p, kp, vp, sp = trunk_layer(h_p, *w, None, None, None)
        h_s, ks, vs, ss = trunk_layer(h_s, *w, cache_k[l], cache_v[l], state_gla[l])
        kp_l.append(kp); vp_l.append(vp); sp_l.append(sp)
        ks_l.append(ks); vs_l.append(vs); ss_l.append(ss)
    return (h_p, h_s, jnp.stack(kp_l), jnp.stack(vp_l), jnp.stack(sp_l),
            jnp.stack(ks_l), jnp.stack(vs_l), jnp.stack(ss_l))
```

```python
import functools
import math

import jax
import jax.numpy as jnp
from jax import lax
from jax.experimental import pallas as pl
from jax.experimental.pallas import tpu as pltpu

F32 = jnp.float32
BF16 = jnp.bfloat16

D_MODEL = 2048
DEPTH = 4
CHUNK = 64
WINDOW = 128
HEAD_DIM = 64
N_Q_HEADS = 16
N_KV_HEADS = 4
GQA_GROUP = N_Q_HEADS // N_KV_HEADS
ATTN_WIDTH = N_Q_HEADS * HEAD_DIM
KV_WIDTH = N_KV_HEADS * HEAD_DIM
GLA_HEADS = 4
GLA_DK = 512
GLA_DV = 1024
GLA_HK = GLA_DK // GLA_HEADS
GLA_HV = GLA_DV // GLA_HEADS
GLA_RANK = 16
GLA_TAU = 16.0
NORM_EPS = 1e-6
IN_SIZES = (ATTN_WIDTH, KV_WIDTH, KV_WIDTH, ATTN_WIDTH,
            GLA_DK, GLA_DK, GLA_DV, GLA_DV, GLA_RANK, D_MODEL, D_MODEL)

KEYS = WINDOW + CHUNK
SUB = 8
NSUB = CHUNK // SUB

COL_GA = 0
COL_GG = 2048
COL_QA = 4096
COL_ZA = 5120
COL_VG = 6144
COL_ZG = 7168
COL_QG = 8192
COL_KG = 8704
COL_KA = 9216
COL_VA = 9472
COL_LR = 9728
LR_PAD = 128
PACKED_COLS = 9984

TM_PROJ = 2048
TM_PROJ_SUB = 512
TN_PROJ = 768
TM_OUT = 256
TM_NORM = 512
PROMPT_CHUNKS_PER_STEP = 4

VMEM_LIMIT = 56 * 1024 * 1024


def _pack_w_in(w_in):
    idx = []
    acc = 0
    for s in IN_SIZES[:-1]:
        acc += s
        idx.append(acc)
    qa, ka, va, za, qg, kg, vg, zg, lr, ga, gg = jnp.split(w_in, idx, axis=-1)
    pad = jnp.zeros(w_in.shape[:-1] + (PACKED_COLS - COL_LR - GLA_RANK,), w_in.dtype)
    packed = jnp.concatenate([ga, gg, qa, za, vg, zg, qg, kg, ka, va, lr, pad], axis=-1)
    return packed.astype(BF16)


def _rms(x, g):
    return x * lax.rsqrt(jnp.mean(x * x, axis=-1, keepdims=True) + NORM_EPS) * g


def _prenorm_kernel(x_ref, g_ref, u_ref):
    u_ref[...] = _rms(x_ref[...], g_ref[...]).astype(BF16)


def _prenorm(x, g):
    t = x.shape[0]
    return pl.pallas_call(
        _prenorm_kernel,
        grid=(t // TM_NORM,),
        in_specs=[pl.BlockSpec((TM_NORM, D_MODEL), lambda i: (i, 0)),
                  pl.BlockSpec((1, D_MODEL), lambda i: (0, 0))],
        out_specs=pl.BlockSpec((TM_NORM, D_MODEL), lambda i: (i, 0)),
        out_shape=jax.ShapeDtypeStruct((t, D_MODEL), BF16),
        compiler_params=pltpu.CompilerParams(dimension_semantics=("arbitrary",)),
        name="prenorm",
    )(x, g)


def _proj_kernel(u_ref, w_ref, o_ref):
    w = w_ref[...]
    for m in range(TM_PROJ // TM_PROJ_SUB):
        rows = slice(m * TM_PROJ_SUB, (m + 1) * TM_PROJ_SUB)
        o_ref[rows, :] = jnp.dot(u_ref[rows, :], w, preferred_element_type=F32).astype(BF16)


def _project(u, w_packed, layer):
    t = u.shape[0]
    return pl.pallas_call(
        _proj_kernel,
        grid=(t // TM_PROJ, PACKED_COLS // TN_PROJ),
        in_specs=[pl.BlockSpec((TM_PROJ, D_MODEL), lambda i, j: (i, 0)),
                  pl.BlockSpec((None, D_MODEL, TN_PROJ), lambda i, j: (layer, 0, j))],
        out_specs=pl.BlockSpec((TM_PROJ, TN_PROJ), lambda i, j: (i, j)),
        out_shape=jax.ShapeDtypeStruct((t, PACKED_COLS), BF16),
        compiler_params=pltpu.CompilerParams(
            dimension_semantics=("arbitrary", "arbitrary"), vmem_limit_bytes=VMEM_LIMIT),
        name="proj",
    )(u, w_packed)


def _silu(x):
    return x / (1.0 + jnp.exp(-x))


def _log_sigmoid(x):
    return jnp.minimum(x, 0.0) - jnp.log1p(jnp.exp(-jnp.abs(x)))


def _alibi_slope(h):
    return math.pow(2.0, -8.0 * (h + 1) / N_Q_HEADS)


def _fill_alibi(bias_ref):
    t = lax.broadcasted_iota(jnp.int32, (CHUNK, KEYS), 0)
    s = lax.broadcasted_iota(jnp.int32, (CHUNK, KEYS), 1)
    dist = jnp.abs(WINDOW + t - s).astype(F32)
    for kh in range(N_KV_HEADS):
        for g in range(GQA_GROUP):
            bias_ref[kh, g * CHUNK:(g + 1) * CHUNK, :] = dist * (-_alibi_slope(kh * GQA_GROUP + g))


def _attention_chunk(r, qa_ref, za_ref, kwin, vwin, sinks_ref, layer, bias_ref, key_lo, xa_ref):
    rows = pl.ds(r, CHUNK)
    if key_lo is not None:
        key_idx = lax.broadcasted_iota(jnp.int32, (CHUNK, KEYS), 1)
        key_ok = key_idx >= key_lo
    for kh in range(N_KV_HEADS):
        k = kwin[:, kh * HEAD_DIM:(kh + 1) * HEAD_DIM]
        v = vwin[:, kh * HEAD_DIM:(kh + 1) * HEAD_DIM]
        q = jnp.concatenate(
            [qa_ref[rows, pl.ds((kh * GQA_GROUP + g) * HEAD_DIM, HEAD_DIM)] for g in range(GQA_GROUP)],
            axis=0)
        q = q * jnp.asarray(HEAD_DIM ** -0.5, BF16)
        s = lax.dot_general(q, k, (((1,), (1,)), ((), ())), preferred_element_type=F32)
        s = s + bias_ref[kh]
        probs = []
        dens = []
        for g in range(GQA_GROUP):
            sg = s[g * CHUNK:(g + 1) * CHUNK]
            if key_lo is not None:
                sg = jnp.where(key_ok, sg, -jnp.inf)
            sink = sinks_ref[layer, kh * GQA_GROUP + g]
            m = jnp.maximum(jnp.max(sg, axis=1, keepdims=True), sink)
            p = jnp.exp(sg - m)
            dens.append(jnp.sum(p, axis=1, keepdims=True) + jnp.exp(sink - m))
            probs.append(p.astype(BF16))
        o = jnp.dot(jnp.concatenate(probs, axis=0), v, preferred_element_type=F32)
        for g in range(GQA_GROUP):
            h = kh * GQA_GROUP + g
            cols = pl.ds(h * HEAD_DIM, HEAD_DIM)
            og = o[g * CHUNK:(g + 1) * CHUNK] / dens[g]
            z = za_ref[rows, cols].astype(F32)
            xa_ref[rows, cols] = (og * _silu(z)).astype(BF16)


def _gla_chunk(r, qg_ref, kg_ref, vg_ref, zg_ref, la, state_ref, gnorm, band, xg_ref):
    rows = pl.ds(r, CHUNK)
    tri = (lax.broadcasted_iota(jnp.int32, (CHUNK, CHUNK), 0)
           >= lax.broadcasted_iota(jnp.int32, (CHUNK, CHUNK), 1)).astype(BF16)
    la_hi = la.astype(BF16)
    rem = la - la_hi.astype(F32)
    la_mid = rem.astype(BF16)
    la_lo = (rem - la_mid.astype(F32)).astype(BF16)
    b_all = (jnp.dot(tri, la_hi, preferred_element_type=F32)
             + jnp.dot(tri, la_mid, preferred_element_type=F32)
             + jnp.dot(tri, la_lo, preferred_element_type=F32))
    col_blk = lax.broadcasted_iota(jnp.int32, (SUB, CHUNK), 1) // SUB

    for hh in range(GLA_HEADS):
        kcols = pl.ds(hh * GLA_HK, GLA_HK)
        vcols = pl.ds(hh * GLA_HV, GLA_HV)
        q = qg_ref[rows, kcols].astype(F32) * (GLA_HK ** -0.5)
        k = kg_ref[rows, kcols].astype(F32)
        v = vg_ref[rows, vcols]
        b = b_all[:, hh * GLA_HK:(hh + 1) * GLA_HK]
        s0 = state_ref[hh]

        diag = jnp.zeros((CHUNK, CHUNK), F32)
        for d in range(SUB):
            if d == 0:
                prod = q * k
            else:
                kd = pltpu.roll(k, d, 0)
                bd = pltpu.roll(b, d, 0)
                prod = q * kd * jnp.exp(jnp.minimum(b - bd, 0.0))
            rd = jnp.sum(prod, axis=1, keepdims=True)
            diag = jnp.where(band == d, rd, diag)

        b3 = b.reshape(NSUB, SUB, GLA_HK)
        b_last = b3[:, SUB - 1:SUB, :]
        k_hat = k * jnp.exp(jnp.broadcast_to(b_last, b3.shape).reshape(CHUNK, GLA_HK) - b)

        lhs = []
        for j in range(NSUB - 1):
            lo = (j + 1) * SUB
            lhs.append(q[lo:] * jnp.exp(b[lo:] - b_last[j]))
        lhs = jnp.concatenate(lhs, axis=0).astype(BF16)
        rr = lax.dot_general(lhs, k_hat.astype(BF16), (((1,), (1,)), ((), ())),
                             preferred_element_type=F32)
        off_rows = [jnp.zeros((SUB, CHUNK), F32)]
        for i in range(1, NSUB):
            acc = jnp.zeros((SUB, CHUNK), F32)
            for j in range(i):
                base = sum((NSUB - 1 - jj) * SUB for jj in range(j)) + (i - j - 1) * SUB
                acc = jnp.where(col_blk == j, rr[base:base + SUB], acc)
            off_rows.append(acc)
        scores = diag + jnp.concatenate(off_rows, axis=0)

        o = (jnp.dot(scores.astype(BF16), v, preferred_element_type=F32)
             + jnp.dot((q * jnp.exp(b)).astype(BF16), s0.astype(BF16), preferred_element_type=F32))

        b_end = b[CHUNK - 1:CHUNK, :]
        k_til = (k * jnp.exp(b_end - b)).astype(BF16)
        upd = lax.dot_general(k_til, v, (((0,), (0,)), ((), ())), preferred_element_type=F32)
        e_col = jnp.transpose(jnp.broadcast_to(jnp.exp(b_end), (GLA_HK, GLA_HK)))
        state_ref[hh] = s0 * jnp.concatenate([e_col, e_col], axis=1) + upd

        y = _rms(o, gnorm)
        z = zg_ref[rows, vcols].astype(F32)
        xg_ref[rows, vcols] = (y * _silu(z)).astype(BF16)


def _band_matrix():
    row = lax.broadcasted_iota(jnp.int32, (CHUNK, CHUNK), 0)
    col = lax.broadcasted_iota(jnp.int32, (CHUNK, CHUNK), 1)
    return jnp.where(row // SUB == col // SUB, row - col, -1)


def _log_forget(lr, wfu_ref, bf_ref):
    x = jnp.dot(lr, wfu_ref[...], preferred_element_type=F32) + bf_ref[...]
    return _log_sigmoid(x) / GLA_TAU


def _mix_prompt_kernel(layer, n_steps, chunks,
                       qa_ref, za_ref, vg_ref, zg_ref, qg_ref, kg_ref, ka_ref, va_ref, lr_ref,
                       wfu_ref, bf_ref, gn_ref, sinks_ref,
                       xa_ref, xg_ref, st_ref,
                       kfull, vfull, state, la_scr, bias_scr):
    b = pl.program_id(0)
    t = pl.program_id(1)

    @pl.when((b == 0) & (t == 0))
    def _():
        _fill_alibi(bias_scr)

    @pl.when(t == 0)
    def _():
        kfull[0:WINDOW, :] = jnp.zeros((WINDOW, KV_WIDTH), BF16)
        vfull[0:WINDOW, :] = jnp.zeros((WINDOW, KV_WIDTH), BF16)
        state[...] = jnp.zeros(state.shape, F32)

    kfull[WINDOW:, :] = ka_ref[...]
    vfull[WINDOW:, :] = va_ref[...]
    la_scr[...] = _log_forget(lr_ref[...], wfu_ref, bf_ref)
    band = _band_matrix()
    gnorm = gn_ref[...]

    def body(c, carry):
        r = pl.multiple_of(c * CHUNK, CHUNK)
        kwin = kfull[pl.ds(r, KEYS), :]
        vwin = vfull[pl.ds(r, KEYS), :]
        key_lo = (WINDOW // CHUNK - (t * chunks + c)) * CHUNK
        _attention_chunk(r, qa_ref, za_ref, kwin, vwin, sinks_ref, layer, bias_scr, key_lo, xa_ref)
        _gla_chunk(r, qg_ref, kg_ref, vg_ref, zg_ref, la_scr[pl.ds(r, CHUNK), :], state, gnorm, band, xg_ref)
        return carry

    lax.fori_loop(0, chunks, body, 0)

    kfull[0:WINDOW, :] = kfull[chunks * CHUNK:, :]
    vfull[0:WINDOW, :] = vfull[chunks * CHUNK:, :]

    @pl.when(t == n_steps - 1)
    def _():
        st_ref[...] = state[...]


def _mix_sample_kernel(layer, row_block0,
                       qa_ref, za_ref, vg_ref, zg_ref, qg_ref, kg_ref, ka_ref, va_ref, lr_ref,
                       wfu_ref, bf_ref, gn_ref, sinks_ref, ck_ref, cv_ref, s0_ref,
                       xa_ref, xg_ref, st_ref,
                       bias_scr):
    del row_block0

    @pl.when(pl.program_id(0) == 0)
    def _():
        _fill_alibi(bias_scr)

    kwin = jnp.concatenate([ck_ref[...].astype(BF16), ka_ref[...]], axis=0)
    vwin = jnp.concatenate([cv_ref[...].astype(BF16), va_ref[...]], axis=0)
    st_ref[...] = s0_ref[...]
    la = _log_forget(lr_ref[...], wfu_ref, bf_ref)
    _attention_chunk(0, qa_ref, za_ref, kwin, vwin, sinks_ref, layer, bias_scr, None, xa_ref)
    _gla_chunk(0, qg_ref, kg_ref, vg_ref, zg_ref, la, st_ref, gn_ref[...], _band_matrix(), xg_ref)


def _section_specs(rows, row_index):
    def spec(width, col):
        return pl.BlockSpec((rows, width), functools.partial(
            lambda cb, *g: (row_index(*g), cb), col // width))
    return [spec(ATTN_WIDTH, COL_QA), spec(ATTN_WIDTH, COL_ZA), spec(GLA_DV, COL_VG), spec(GLA_DV, COL_ZG),
            spec(GLA_DK, COL_QG), spec(GLA_DK, COL_KG), spec(KV_WIDTH, COL_KA), spec(KV_WIDTH, COL_VA),
            spec(LR_PAD, COL_LR)]


def _layer_weight_specs(layer, ngrid):
    zeros = (0,) * 2
    return [pl.BlockSpec((None, LR_PAD, GLA_DK), lambda *g: (layer,) + zeros),
            pl.BlockSpec((None, 1, GLA_DK), lambda *g: (layer,) + zeros),
            pl.BlockSpec((None, 1, GLA_HV), lambda *g: (layer,) + zeros),
            pl.BlockSpec(memory_space=pltpu.SMEM)]


def _mix_prompt(p, wfu, bfg, gn, sinks, layer, batch, seq):
    chunks = PROMPT_CHUNKS_PER_STEP
    rows = chunks * CHUNK
    n_steps = seq // rows
    row_index = lambda b, t: b * n_steps + t
    return pl.pallas_call(
        functools.partial(_mix_prompt_kernel, layer, n_steps, chunks),
        grid=(batch, n_steps),
        in_specs=_section_specs(rows, row_index) + _layer_weight_specs(layer, 2),
        out_specs=[pl.BlockSpec((rows, ATTN_WIDTH), lambda b, t: (b * n_steps + t, 0)),
                   pl.BlockSpec((rows, GLA_DV), lambda b, t: (b * n_steps + t, 0)),
                   pl.BlockSpec((None, GLA_HEADS, GLA_HK, GLA_HV), lambda b, t: (b, 0, 0, 0))],
        out_shape=[jax.ShapeDtypeStruct((batch * seq, ATTN_WIDTH), BF16),
                   jax.ShapeDtypeStruct((batch * seq, GLA_DV), BF16),
                   jax.ShapeDtypeStruct((batch, GLA_HEADS, GLA_HK, GLA_HV), F32)],
        scratch_shapes=[pltpu.VMEM((rows + WINDOW, KV_WIDTH), BF16),
                        pltpu.VMEM((rows + WINDOW, KV_WIDTH), BF16),
                        pltpu.VMEM((GLA_HEADS, GLA_HK, GLA_HV), F32),
                        pltpu.VMEM((rows, GLA_DK), F32),
                        pltpu.VMEM((N_KV_HEADS, GQA_GROUP * CHUNK, KEYS), F32)],
        compiler_params=pltpu.CompilerParams(
            dimension_semantics=("arbitrary", "arbitrary"), vmem_limit_bytes=VMEM_LIMIT),
        name="mix_prompt",
    )(*([p] * 9), wfu, bfg, gn, sinks)


def _mix_sample(p, wfu, bfg, gn, sinks, cache_k, cache_v, state, layer, row0, batch):
    row_block0 = row0 // CHUNK
    row_index = lambda b: row_block0 + b
    cache_spec = pl.BlockSpec((None, None, WINDOW, KV_WIDTH), lambda b: (layer, b, 0, 0))
    state_spec = pl.BlockSpec((None, None, GLA_HEADS, GLA_HK, GLA_HV), lambda b: (layer, b, 0, 0, 0))
    return pl.pallas_call(
        functools.partial(_mix_sample_kernel, layer, row_block0),
        grid=(batch,),
        in_specs=(_section_specs(CHUNK, row_index) + _layer_weight_specs(layer, 1)
                  + [cache_spec, cache_spec, state_spec]),
        out_specs=[pl.BlockSpec((CHUNK, ATTN_WIDTH), lambda b: (b, 0)),
                   pl.BlockSpec((CHUNK, GLA_DV), lambda b: (b, 0)),
                   pl.BlockSpec((None, GLA_HEADS, GLA_HK, GLA_HV), lambda b: (b, 0, 0, 0))],
        out_shape=[jax.ShapeDtypeStruct((batch * CHUNK, ATTN_WIDTH), BF16),
                   jax.ShapeDtypeStruct((batch * CHUNK, GLA_DV), BF16),
                   jax.ShapeDtypeStruct((batch, GLA_HEADS, GLA_HK, GLA_HV), F32)],
        scratch_shapes=[pltpu.VMEM((N_KV_HEADS, GQA_GROUP * CHUNK, KEYS), F32)],
        compiler_params=pltpu.CompilerParams(
            dimension_semantics=("arbitrary",), vmem_limit_bytes=VMEM_LIMIT),
        name="mix_sample",
    )(*([p] * 9), wfu, bfg, gn, sinks, cache_k, cache_v, state)


def _sigmoid(x):
    return 1.0 / (1.0 + jnp.exp(-x))


def _out_kernel(n_prompt_tiles,
                xa_p_ref, xa_s_ref, xg_p_ref, xg_s_ref, ga_ref, gg_ref, h_ref,
                wa_ref, wg_ref, wo_ref, bg_ref, npost_ref, npre_ref,
                h_out_ref, u_out_ref):
    is_prompt = pl.program_id(0) < n_prompt_tiles
    xa = jnp.where(is_prompt, xa_p_ref[...], xa_s_ref[...])
    xg = jnp.where(is_prompt, xg_p_ref[...], xg_s_ref[...])
    ya = jnp.dot(xa, wa_ref[...], preferred_element_type=F32)
    yg = jnp.dot(xg, wg_ref[...], preferred_element_type=F32)
    bg = bg_ref[...]
    gate_a = _sigmoid(ga_ref[...].astype(F32) + bg[:, :D_MODEL])
    gate_g = _sigmoid(gg_ref[...].astype(F32) + bg[:, D_MODEL:])
    merged = (gate_a * ya + gate_g * yg).astype(BF16)
    z = jnp.dot(merged, wo_ref[...], preferred_element_type=F32)
    h = h_ref[...] + _rms(z, npost_ref[...])
    h_out_ref[...] = h
    u_out_ref[...] = _rms(h, npre_ref[...]).astype(BF16)


def _output(xa_p, xa_s, xg_p, xg_s, p, h, wa, wg, wo, bg, npost, npre_next, layer, next_layer):
    t = h.shape[0]
    n_tiles = t // TM_OUT
    n_prompt_tiles = xa_p.shape[0] // TM_OUT
    last_p = n_prompt_tiles - 1
    p_idx = lambda i: (jnp.minimum(i, last_p), 0)
    s_idx = lambda i: (jnp.maximum(i - n_prompt_tiles, 0), 0)
    wspec = lambda k: pl.BlockSpec((None, k, D_MODEL), lambda i: (layer, 0, 0), pipeline_mode=pl.Buffered(1))
    vspec = lambda n, l: pl.BlockSpec((None, 1, n), lambda i: (l, 0, 0))
    return pl.pallas_call(
        functools.partial(_out_kernel, n_prompt_tiles),
        grid=(n_tiles,),
        in_specs=[pl.BlockSpec((TM_OUT, ATTN_WIDTH), p_idx), pl.BlockSpec((TM_OUT, ATTN_WIDTH), s_idx),
                  pl.BlockSpec((TM_OUT, GLA_DV), p_idx), pl.BlockSpec((TM_OUT, GLA_DV), s_idx),
                  pl.BlockSpec((TM_OUT, D_MODEL), lambda i: (i, COL_GA // D_MODEL)),
                  pl.BlockSpec((TM_OUT, D_MODEL), lambda i: (i, COL_GG // D_MODEL)),
                  pl.BlockSpec((TM_OUT, D_MODEL), lambda i: (i, 0)),
                  wspec(ATTN_WIDTH), wspec(GLA_DV), wspec(D_MODEL),
                  vspec(2 * D_MODEL, layer), vspec(D_MODEL, layer), vspec(D_MODEL, next_layer)],
        out_specs=[pl.BlockSpec((TM_OUT, D_MODEL), lambda i: (i, 0)),
                   pl.BlockSpec((TM_OUT, D_MODEL), lambda i: (i, 0))],
        out_shape=[jax.ShapeDtypeStruct((t, D_MODEL), F32),
                   jax.ShapeDtypeStruct((t, D_MODEL), BF16)],
        compiler_params=pltpu.CompilerParams(
            dimension_semantics=("arbitrary",), vmem_limit_bytes=VMEM_LIMIT),
        name="out",
    )(xa_p, xa_s, xg_p, xg_s, p, p, h, wa, wg, wo, bg, npost, npre_next)


def kernel(x_prompt, x_sample, cache_k, cache_v, state_gla, norm_pre, norm_post, w_in, b_gate,
           attn_sinks, w_forget_up, b_forget, gla_norm, w_branch_attn, w_branch_gla, w_out):
    batch, seq, _ = x_prompt.shape
    dec_batch, dec_seq, _ = x_sample.shape
    assert dec_seq == CHUNK and seq % (PROMPT_CHUNKS_PER_STEP * CHUNK) == 0
    assert cache_k.shape[2] == WINDOW
    n_prompt = batch * seq
    n_sample = dec_batch * dec_seq

    w_packed = _pack_w_in(w_in)
    wa = w_branch_attn.astype(BF16)
    wg = w_branch_gla.astype(BF16)
    wo = w_out.astype(BF16)
    wfu = jnp.pad(w_forget_up, ((0, 0), (0, LR_PAD - GLA_RANK), (0, 0))).astype(BF16)
    bfg = b_forget.reshape(DEPTH, 1, GLA_DK)
    gn = gla_norm.reshape(DEPTH, 1, GLA_HV)
    bg = b_gate.reshape(DEPTH, 1, 2 * D_MODEL)
    npre = norm_pre.reshape(DEPTH, 1, D_MODEL)
    npost = norm_post.reshape(DEPTH, 1, D_MODEL)
    ck = cache_k.reshape(DEPTH, dec_batch, WINDOW, KV_WIDTH)
    cv = cache_v.reshape(DEPTH, dec_batch, WINDOW, KV_WIDTH)

    h = jnp.concatenate([x_prompt.reshape(n_prompt, D_MODEL), x_sample.reshape(n_sample, D_MODEL)], axis=0)
    u = _prenorm(h, npre[0])

    kp_l, vp_l, sp_l, ks_l, vs_l, ss_l = [], [], [], [], [], []
    for l in range(DEPTH):
        p = _project(u, w_packed, l)
        xa_p, xg_p, st_p = _mix_prompt(p, wfu, bfg, gn, attn_sinks, l, batch, seq)
        xa_s, xg_s, st_s = _mix_sample(p, wfu, bfg, gn, attn_sinks, ck, cv, state_gla, l, n_prompt, dec_batch)
        h, u = _output(xa_p, xa_s, xg_p, xg_s, p, h, wa, wg, wo, bg, npost, npre, l, min(l + 1, DEPTH - 1))

        keep = min(WINDOW, seq)
        kv_p = p[:n_prompt, COL_KA:COL_KA + 2 * KV_WIDTH].reshape(batch, seq, 2 * KV_WIDTH)[:, seq - keep:]
        kv_p = kv_p.astype(F32)
        kp_l.append(kv_p[..., :KV_WIDTH].reshape(batch, keep, N_KV_HEADS, HEAD_DIM))
        vp_l.append(kv_p[..., KV_WIDTH:].reshape(batch, keep, N_KV_HEADS, HEAD_DIM))
        kv_s = p[n_prompt:, COL_KA:COL_KA + 2 * KV_WIDTH].reshape(dec_batch, dec_seq, 2 * KV_WIDTH).astype(F32)
        ks_l.append(jnp.concatenate([ck[l][:, dec_seq:], kv_s[..., :KV_WIDTH]], axis=1)
                    .reshape(dec_batch, WINDOW, N_KV_HEADS, HEAD_DIM))
        vs_l.append(jnp.concatenate([cv[l][:, dec_seq:], kv_s[..., KV_WIDTH:]], axis=1)
                    .reshape(dec_batch, WINDOW, N_KV_HEADS, HEAD_DIM))
        sp_l.append(st_p)
        ss_l.append(st_s)

    y_prompt = h[:n_prompt].reshape(batch, seq, D_MODEL)
    y_sample = h[n_prompt:].reshape(dec_batch, dec_seq, D_MODEL)
    return (y_prompt, y_sample, jnp.stack(kp_l), jnp.stack(vp_l), jnp.stack(sp_l),
            jnp.stack(ks_l), jnp.stack(vs_l), jnp.stack(ss_l))
```

```python
import functools
import math

import jax
import jax.numpy as jnp
from jax import lax
from jax.experimental import pallas as pl
from jax.experimental.pallas import tpu as pltpu

F32 = jnp.float32
BF16 = jnp.bfloat16

D_MODEL = 2048
DEPTH = 4
CHUNK = 64
WINDOW = 128
HEAD_DIM = 64
N_Q_HEADS = 16
N_KV_HEADS = 4
GQA_GROUP = N_Q_HEADS // N_KV_HEADS
ATTN_WIDTH = N_Q_HEADS * HEAD_DIM
KV_WIDTH = N_KV_HEADS * HEAD_DIM
GLA_HEADS = 4
GLA_DK = 512
GLA_DV = 1024
GLA_HK = GLA_DK // GLA_HEADS
GLA_HV = GLA_DV // GLA_HEADS
GLA_RANK = 16
GLA_TAU = 16.0
NORM_EPS = 1e-6
IN_SIZES = (ATTN_WIDTH, KV_WIDTH, KV_WIDTH, ATTN_WIDTH,
            GLA_DK, GLA_DK, GLA_DV, GLA_DV, GLA_RANK, D_MODEL, D_MODEL)

KEYS = WINDOW + CHUNK
SUB = 8
NSUB = CHUNK // SUB

TM_PROJ = 3072
TM_PROJ_SUB = 512
TN_PROJ = 512
TM_OUT = 256
TM_NORM = 512
PROMPT_CHUNKS_PER_STEP = 4

VMEM_LIMIT = 56 * 1024 * 1024

SRC_COLS = {}
_acc = 0
for _name, _size in zip(("qa", "ka", "va", "za", "qg", "kg", "vg", "zg", "lr", "ga", "gg"), IN_SIZES):
    SRC_COLS[_name] = (_acc, _size)
    _acc += _size
IN_COLS = _acc

PACKED_ORDER = ("ga", "gg", "qa", "za", "vg", "zg", "qg", "kg", "ka", "va")
PACKED_COL = {}
PACKED_SRC_ROWS = []
_acc = 0
for _name in PACKED_ORDER:
    _start, _size = SRC_COLS[_name]
    PACKED_COL[_name] = _acc
    if _acc % TN_PROJ == 0:
        PACKED_SRC_ROWS += [_start + k for k in range(0, max(_size, TN_PROJ), TN_PROJ)]
    _acc += _size
PACKED_COLS = _acc
COL_GA, COL_GG, COL_QA, COL_ZA, COL_VG, COL_ZG, COL_QG, COL_KG, COL_KA, COL_VA = (
    PACKED_COL[_n] for _n in PACKED_ORDER)
LR_PAD = 128
assert SRC_COLS["va"][0] == SRC_COLS["ka"][0] + KV_WIDTH and 2 * KV_WIDTH == TN_PROJ
assert len(PACKED_SRC_ROWS) * TN_PROJ == PACKED_COLS
assert all(_r % GLA_RANK == 0 for _r in PACKED_SRC_ROWS)


def _rms(x, g):
    return x * lax.rsqrt(jnp.mean(x * x, axis=-1, keepdims=True) + NORM_EPS) * g


def _two_group_index_maps(n_prompt_tiles):
    p_idx = lambda i: (jnp.minimum(i, n_prompt_tiles - 1), 0)
    s_idx = lambda i: (jnp.maximum(i - n_prompt_tiles, 0), 0)
    return p_idx, s_idx


def _prenorm_kernel(n_prompt_tiles, xp_ref, xs_ref, g_ref, u_ref):
    x = jnp.where(pl.program_id(0) < n_prompt_tiles, xp_ref[...], xs_ref[...])
    u_ref[...] = _rms(x, g_ref[...]).astype(BF16)


def _prenorm(x_prompt, x_sample, g):
    n_prompt_tiles = x_prompt.shape[0] // TM_NORM
    n_tiles = n_prompt_tiles + x_sample.shape[0] // TM_NORM
    p_idx, s_idx = _two_group_index_maps(n_prompt_tiles)
    return pl.pallas_call(
        functools.partial(_prenorm_kernel, n_prompt_tiles),
        grid=(n_tiles,),
        in_specs=[pl.BlockSpec((TM_NORM, D_MODEL), p_idx),
                  pl.BlockSpec((TM_NORM, D_MODEL), s_idx),
                  pl.BlockSpec((1, D_MODEL), lambda i: (0, 0))],
        out_specs=pl.BlockSpec((TM_NORM, D_MODEL), lambda i: (i, 0)),
        out_shape=jax.ShapeDtypeStruct((n_tiles * TM_NORM, D_MODEL), BF16),
        compiler_params=pltpu.CompilerParams(dimension_semantics=("arbitrary",)),
        name="prenorm",
    )(x_prompt, x_sample, g)


def _nt_dot(a, b):
    return lax.dot_general(a, b, (((1,), (1,)), ((), ())), preferred_element_type=F32)


def _proj_kernel(src_rows_ref, u_ref, w_ref, wlr_ref, o_ref, olr_ref):
    del src_rows_ref
    w = w_ref[...].astype(BF16)
    for m in range(TM_PROJ // TM_PROJ_SUB):
        rows = slice(m * TM_PROJ_SUB, (m + 1) * TM_PROJ_SUB)
        o_ref[rows, :] = _nt_dot(u_ref[rows, :], w).astype(BF16)

    @pl.when(pl.program_id(1) == 0)
    def _():
        wlr = wlr_ref[...].astype(BF16)
        for m in range(TM_PROJ // TM_PROJ_SUB):
            rows = slice(m * TM_PROJ_SUB, (m + 1) * TM_PROJ_SUB)
            olr_ref[rows, :] = _nt_dot(u_ref[rows, :], wlr).astype(BF16)


def _project(u, w_in_t, layer):
    t = u.shape[0]
    lr_row = SRC_COLS["lr"][0]
    wspec = lambda rows, index_map: pl.BlockSpec((None, pl.Element(rows), pl.Element(D_MODEL)), index_map)
    grid_spec = pltpu.PrefetchScalarGridSpec(
        num_scalar_prefetch=1,
        grid=(t // TM_PROJ, PACKED_COLS // TN_PROJ),
        in_specs=[pl.BlockSpec((TM_PROJ, D_MODEL), lambda i, j, src: (i, 0)),
                  wspec(TN_PROJ, lambda i, j, src: (layer, src[j] * GLA_RANK, 0)),
                  wspec(LR_PAD, lambda i, j, src: (layer, lr_row, 0))],
        out_specs=[pl.BlockSpec((TM_PROJ, TN_PROJ), lambda i, j, src: (i, j)),
                   pl.BlockSpec((TM_PROJ, LR_PAD), lambda i, j, src: (i, 0))])
    return pl.pallas_call(
        _proj_kernel,
        grid_spec=grid_spec,
        out_shape=[jax.ShapeDtypeStruct((t, PACKED_COLS), BF16),
                   jax.ShapeDtypeStruct((t, LR_PAD), BF16)],
        compiler_params=pltpu.CompilerParams(
            dimension_semantics=("arbitrary", "arbitrary"), vmem_limit_bytes=VMEM_LIMIT),
        name="proj",
    )(jnp.asarray([r // GLA_RANK for r in PACKED_SRC_ROWS], jnp.int32), u, w_in_t, w_in_t)


def _silu(x):
    return x / (1.0 + jnp.exp(-x))


def _log_sigmoid(x):
    return jnp.minimum(x, 0.0) - jnp.log1p(jnp.exp(-jnp.abs(x)))


def _alibi_slope(h):
    return math.pow(2.0, -8.0 * (h + 1) / N_Q_HEADS)


def _fill_alibi(bias_ref):
    t = lax.broadcasted_iota(jnp.int32, (CHUNK, KEYS), 0)
    s = lax.broadcasted_iota(jnp.int32, (CHUNK, KEYS), 1)
    dist = jnp.abs(WINDOW + t - s).astype(F32)
    for kh in range(N_KV_HEADS):
        for g in range(GQA_GROUP):
            bias_ref[kh, g * CHUNK:(g + 1) * CHUNK, :] = dist * (-_alibi_slope(kh * GQA_GROUP + g))


def _attention_chunk(r, qa_ref, za_ref, kwin, vwin, sinks_ref, layer, bias_ref, key_lo, xa_ref):
    rows = pl.ds(r, CHUNK)
    if key_lo is not None:
        key_idx = lax.broadcasted_iota(jnp.int32, (CHUNK, KEYS), 1)
        key_ok = key_idx >= key_lo
    for kh in range(N_KV_HEADS):
        k = kwin[:, kh * HEAD_DIM:(kh + 1) * HEAD_DIM]
        v = vwin[:, kh * HEAD_DIM:(kh + 1) * HEAD_DIM]
        q = jnp.concatenate(
            [qa_ref[rows, pl.ds((kh * GQA_GROUP + g) * HEAD_DIM, HEAD_DIM)] for g in range(GQA_GROUP)],
            axis=0)
        q = q * jnp.asarray(HEAD_DIM ** -0.5, BF16)
        s = lax.dot_general(q, k, (((1,), (1,)), ((), ())), preferred_element_type=F32)
        s = s + bias_ref[kh]
        probs = []
        dens = []
        for g in range(GQA_GROUP):
            sg = s[g * CHUNK:(g + 1) * CHUNK]
            if key_lo is not None:
                sg = jnp.where(key_ok, sg, -jnp.inf)
            sink = sinks_ref[layer, kh * GQA_GROUP + g]
            m = jnp.maximum(jnp.max(sg, axis=1, keepdims=True), sink)
            p = jnp.exp(sg - m)
            dens.append(jnp.sum(p, axis=1, keepdims=True) + jnp.exp(sink - m))
            probs.append(p.astype(BF16))
        o = jnp.dot(jnp.concatenate(probs, axis=0), v, preferred_element_type=F32)
        for g in range(GQA_GROUP):
            h = kh * GQA_GROUP + g
            cols = pl.ds(h * HEAD_DIM, HEAD_DIM)
            og = o[g * CHUNK:(g + 1) * CHUNK] / dens[g]
            z = za_ref[rows, cols].astype(F32)
            xa_ref[rows, cols] = (og * _silu(z)).astype(BF16)


def _gla_chunk(r, qg_ref, kg_ref, vg_ref, zg_ref, la, state_ref, gnorm, band, xg_ref):
    rows = pl.ds(r, CHUNK)
    tri = (lax.broadcasted_iota(jnp.int32, (CHUNK, CHUNK), 0)
           >= lax.broadcasted_iota(jnp.int32, (CHUNK, CHUNK), 1)).astype(BF16)
    la_hi = la.astype(BF16)
    rem = la - la_hi.astype(F32)
    la_mid = rem.astype(BF16)
    la_lo = (rem - la_mid.astype(F32)).astype(BF16)
    b_all = (jnp.dot(tri, la_hi, preferred_element_type=F32)
             + jnp.dot(tri, la_mid, preferred_element_type=F32)
             + jnp.dot(tri, la_lo, preferred_element_type=F32))
    col_blk = lax.broadcasted_iota(jnp.int32, (SUB, CHUNK), 1) // SUB

    for hh in range(GLA_HEADS):
        kcols = pl.ds(hh * GLA_HK, GLA_HK)
        vcols = pl.ds(hh * GLA_HV, GLA_HV)
        q = qg_ref[rows, kcols].astype(F32) * (GLA_HK ** -0.5)
        k = kg_ref[rows, kcols].astype(F32)
        v = vg_ref[rows, vcols]
        b = b_all[:, hh * GLA_HK:(hh + 1) * GLA_HK]
        s0 = state_ref[hh]

        diag = jnp.zeros((CHUNK, CHUNK), F32)
        for d in range(SUB):
            if d == 0:
                prod = q * k
            else:
                kd = pltpu.roll(k, d, 0)
                bd = pltpu.roll(b, d, 0)
                prod = q * kd * jnp.exp(jnp.minimum(b - bd, 0.0))
            rd = jnp.sum(prod, axis=1, keepdims=True)
            diag = jnp.where(band == d, rd, diag)

        b3 = b.reshape(NSUB, SUB, GLA_HK)
        b_last = b3[:, SUB - 1:SUB, :]
        k_hat = k * jnp.exp(jnp.broadcast_to(b_last, b3.shape).reshape(CHUNK, GLA_HK) - b)

        lhs = []
        for j in range(NSUB - 1):
            lo = (j + 1) * SUB
            lhs.append(q[lo:] * jnp.exp(b[lo:] - b_last[j]))
        lhs = jnp.concatenate(lhs, axis=0).astype(BF16)
        rr = lax.dot_general(lhs, k_hat.astype(BF16), (((1,), (1,)), ((), ())),
                             preferred_element_type=F32)
        off_rows = [jnp.zeros((SUB, CHUNK), F32)]
        for i in range(1, NSUB):
            acc = jnp.zeros((SUB, CHUNK), F32)
            for j in range(i):
                base = sum((NSUB - 1 - jj) * SUB for jj in range(j)) + (i - j - 1) * SUB
                acc = jnp.where(col_blk == j, rr[base:base + SUB], acc)
            off_rows.append(acc)
        scores = diag + jnp.concatenate(off_rows, axis=0)

        o = (jnp.dot(scores.astype(BF16), v, preferred_element_type=F32)
             + jnp.dot((q * jnp.exp(b)).astype(BF16), s0.astype(BF16), preferred_element_type=F32))

        b_end = b[CHUNK - 1:CHUNK, :]
        k_til = (k * jnp.exp(b_end - b)).astype(BF16)
        upd = lax.dot_general(k_til, v, (((0,), (0,)), ((), ())), preferred_element_type=F32)
        e_col = jnp.transpose(jnp.broadcast_to(jnp.exp(b_end), (GLA_HK, GLA_HK)))
        state_ref[hh] = s0 * jnp.concatenate([e_col, e_col], axis=1) + upd

        y = _rms(o, gnorm)
        z = zg_ref[rows, vcols].astype(F32)
        xg_ref[rows, vcols] = (y * _silu(z)).astype(BF16)


def _band_matrix():
    row = lax.broadcasted_iota(jnp.int32, (CHUNK, CHUNK), 0)
    col = lax.broadcasted_iota(jnp.int32, (CHUNK, CHUNK), 1)
    return jnp.where(row // SUB == col // SUB, row - col, -1)


def _log_forget(lr, wfu_ref, bf_ref):
    x = jnp.dot(lr, wfu_ref[...], preferred_element_type=F32) + bf_ref[...]
    return _log_sigmoid(x) / GLA_TAU


def _mix_prompt_kernel(layer, n_steps, chunks, *refs):
    n_carried = 3 if layer > 0 else 0
    (qa_ref, za_ref, vg_ref, zg_ref, qg_ref, kg_ref, ka_ref, va_ref, lr_ref,
     wfu_ref, bf_ref, gn_ref, sinks_ref) = refs[:13]
    (xa_ref, xg_ref, st_ref, knew_ref, vnew_ref,
     kfull, vfull, state, la_scr, bias_scr) = refs[13 + n_carried:]
    b = pl.program_id(0)
    t = pl.program_id(1)

    @pl.when((b == 0) & (t == 0))
    def _():
        _fill_alibi(bias_scr)

    @pl.when(t == 0)
    def _():
        kfull[0:WINDOW, :] = jnp.zeros((WINDOW, KV_WIDTH), BF16)
        vfull[0:WINDOW, :] = jnp.zeros((WINDOW, KV_WIDTH), BF16)
        state[...] = jnp.zeros(state.shape, F32)

    kfull[WINDOW:, :] = ka_ref[...]
    vfull[WINDOW:, :] = va_ref[...]
    la_scr[...] = _log_forget(lr_ref[...], wfu_ref, bf_ref)
    band = _band_matrix()
    gnorm = gn_ref[...]

    def body(c, carry):
        r = pl.multiple_of(c * CHUNK, CHUNK)
        kwin = kfull[pl.ds(r, KEYS), :]
        vwin = vfull[pl.ds(r, KEYS), :]
        key_lo = (WINDOW // CHUNK - (t * chunks + c)) * CHUNK
        _attention_chunk(r, qa_ref, za_ref, kwin, vwin, sinks_ref, layer, bias_scr, key_lo, xa_ref)
        _gla_chunk(r, qg_ref, kg_ref, vg_ref, zg_ref, la_scr[pl.ds(r, CHUNK), :], state, gnorm, band, xg_ref)
        return carry

    lax.fori_loop(0, chunks, body, 0)

    kfull[0:WINDOW, :] = kfull[chunks * CHUNK:, :]
    vfull[0:WINDOW, :] = vfull[chunks * CHUNK:, :]

    @pl.when(t == n_steps - 1)
    def _():
        st_ref[...] = state[...]
        knew_ref[...] = ka_ref[chunks * CHUNK - WINDOW:, :].astype(F32)
        vnew_ref[...] = va_ref[chunks * CHUNK - WINDOW:, :].astype(F32)


def _mix_sample_kernel(layer, *refs):
    n_carried = 3 if layer > 0 else 0
    (qa_ref, za_ref, vg_ref, zg_ref, qg_ref, kg_ref, ka_ref, va_ref, lr_ref,
     wfu_ref, bf_ref, gn_ref, sinks_ref, ck_ref, cv_ref, s0_ref) = refs[:16]
    xa_ref, xg_ref, st_ref, knew_ref, vnew_ref, bias_scr = refs[16 + n_carried:]

    @pl.when(pl.program_id(0) == 0)
    def _():
        _fill_alibi(bias_scr)

    kwin = jnp.concatenate([ck_ref[...].astype(BF16), ka_ref[...]], axis=0)
    vwin = jnp.concatenate([cv_ref[...].astype(BF16), va_ref[...]], axis=0)
    knew_ref[0:WINDOW - CHUNK, :] = ck_ref[CHUNK:, :]
    knew_ref[WINDOW - CHUNK:, :] = ka_ref[...].astype(F32)
    vnew_ref[0:WINDOW - CHUNK, :] = cv_ref[CHUNK:, :]
    vnew_ref[WINDOW - CHUNK:, :] = va_ref[...].astype(F32)
    st_ref[...] = s0_ref[...]
    la = _log_forget(lr_ref[...], wfu_ref, bf_ref)
    _attention_chunk(0, qa_ref, za_ref, kwin, vwin, sinks_ref, layer, bias_scr, None, xa_ref)
    _gla_chunk(0, qg_ref, kg_ref, vg_ref, zg_ref, la, st_ref, gn_ref[...], _band_matrix(), xg_ref)


def _section_specs(rows, row_index):
    def spec(width, col):
        return pl.BlockSpec((rows, width), functools.partial(
            lambda cb, *g: (row_index(*g), cb), col // width))
    return [spec(ATTN_WIDTH, COL_QA), spec(ATTN_WIDTH, COL_ZA), spec(GLA_DV, COL_VG), spec(GLA_DV, COL_ZG),
            spec(GLA_DK, COL_QG), spec(GLA_DK, COL_KG), spec(KV_WIDTH, COL_KA), spec(KV_WIDTH, COL_VA),
            spec(LR_PAD, 0)]


def _layer_weight_specs(layer, ngrid):
    zeros = (0,) * 2
    return [pl.BlockSpec((None, LR_PAD, GLA_DK), lambda *g: (layer,) + zeros),
            pl.BlockSpec((None, 1, GLA_DK), lambda *g: (layer,) + zeros),
            pl.BlockSpec((None, 1, GLA_HV), lambda *g: (layer,) + zeros),
            pl.BlockSpec(memory_space=pltpu.SMEM)]


def _stacked_outputs(layer, batch, n_inputs, carried):
    lead = lambda *g: (layer, g[0])
    specs = [pl.BlockSpec((None, None, GLA_HEADS, GLA_HK, GLA_HV), lambda *g: lead(*g) + (0, 0, 0)),
             pl.BlockSpec((None, None, WINDOW, KV_WIDTH), lambda *g: lead(*g) + (0, 0)),
             pl.BlockSpec((None, None, WINDOW, KV_WIDTH), lambda *g: lead(*g) + (0, 0))]
    shapes = [jax.ShapeDtypeStruct((DEPTH, batch, GLA_HEADS, GLA_HK, GLA_HV), F32),
              jax.ShapeDtypeStruct((DEPTH, batch, WINDOW, KV_WIDTH), F32),
              jax.ShapeDtypeStruct((DEPTH, batch, WINDOW, KV_WIDTH), F32)]
    if carried is None:
        return specs, shapes, [], {}
    alias_specs = [pl.BlockSpec(memory_space=pl.ANY)] * 3
    aliases = {n_inputs + k: 2 + k for k in range(3)}
    return specs, shapes, alias_specs, aliases


def _mix_prompt(p, p_lr, wfu, bfg, gn, sinks, layer, batch, seq, carried):
    chunks = PROMPT_CHUNKS_PER_STEP
    rows = chunks * CHUNK
    n_steps = seq // rows
    row_index = lambda b, t: b * n_steps + t
    in_specs = _section_specs(rows, row_index) + _layer_weight_specs(layer, 2)
    st_specs, st_shapes, alias_specs, aliases = _stacked_outputs(layer, batch, len(in_specs), carried)
    outs = pl.pallas_call(
        functools.partial(_mix_prompt_kernel, layer, n_steps, chunks),
        grid=(batch, n_steps),
        in_specs=in_specs + alias_specs,
        out_specs=[pl.BlockSpec((rows, ATTN_WIDTH), lambda b, t: (b * n_steps + t, 0)),
                   pl.BlockSpec((rows, GLA_DV), lambda b, t: (b * n_steps + t, 0))] + st_specs,
        out_shape=[jax.ShapeDtypeStruct((batch * seq, ATTN_WIDTH), BF16),
                   jax.ShapeDtypeStruct((batch * seq, GLA_DV), BF16)] + st_shapes,
        input_output_aliases=aliases,
        scratch_shapes=[pltpu.VMEM((rows + WINDOW, KV_WIDTH), BF16),
                        pltpu.VMEM((rows + WINDOW, KV_WIDTH), BF16),
                        pltpu.VMEM((GLA_HEADS, GLA_HK, GLA_HV), F32),
                        pltpu.VMEM((rows, GLA_DK), F32),
                        pltpu.VMEM((N_KV_HEADS, GQA_GROUP * CHUNK, KEYS), F32)],
        compiler_params=pltpu.CompilerParams(
            dimension_semantics=("arbitrary", "arbitrary"), vmem_limit_bytes=VMEM_LIMIT),
        name="mix_prompt",
    )(*([p] * 8), p_lr, wfu, bfg, gn, sinks, *(carried or ()))
    return outs[0], outs[1], tuple(outs[2:])


def _mix_sample(p, p_lr, wfu, bfg, gn, sinks, cache_k, cache_v, state, layer, row0, batch, carried):
    row_block0 = row0 // CHUNK
    row_index = lambda b: row_block0 + b
    cache_spec = pl.BlockSpec((None, None, WINDOW, KV_WIDTH), lambda b: (layer, b, 0, 0))
    state_spec = pl.BlockSpec((None, None, GLA_HEADS, GLA_HK, GLA_HV), lambda b: (layer, b, 0, 0, 0))
    in_specs = (_section_specs(CHUNK, row_index) + _layer_weight_specs(layer, 1)
                + [cache_spec, cache_spec, state_spec])
    st_specs, st_shapes, alias_specs, aliases = _stacked_outputs(layer, batch, len(in_specs), carried)
    outs = pl.pallas_call(
        functools.partial(_mix_sample_kernel, layer),
        grid=(batch,),
        in_specs=in_specs + alias_specs,
        out_specs=[pl.BlockSpec((CHUNK, ATTN_WIDTH), lambda b: (b, 0)),
                   pl.BlockSpec((CHUNK, GLA_DV), lambda b: (b, 0))] + st_specs,
        out_shape=[jax.ShapeDtypeStruct((batch * CHUNK, ATTN_WIDTH), BF16),
                   jax.ShapeDtypeStruct((batch * CHUNK, GLA_DV), BF16)] + st_shapes,
        input_output_aliases=aliases,
        scratch_shapes=[pltpu.VMEM((N_KV_HEADS, GQA_GROUP * CHUNK, KEYS), F32)],
        compiler_params=pltpu.CompilerParams(
            dimension_semantics=("arbitrary",), vmem_limit_bytes=VMEM_LIMIT),
        name="mix_sample",
    )(*([p] * 8), p_lr, wfu, bfg, gn, sinks, cache_k, cache_v, state, *(carried or ()))
    return outs[0], outs[1], tuple(outs[2:])


def _sigmoid(x):
    return 1.0 / (1.0 + jnp.exp(-x))


def _out_kernel(n_prompt_tiles, first, last, *refs):
    xa_p_ref, xa_s_ref, xg_p_ref, xg_s_ref, ga_ref, gg_ref = refs[:6]
    n_h = 2 if first else 1
    h_refs = refs[6:6 + n_h]
    wa_ref, wg_ref, wo_ref, bg_ref, npost_ref, npre_ref = refs[6 + n_h:12 + n_h]
    out_refs = refs[12 + n_h:]

    is_prompt = pl.program_id(0) < n_prompt_tiles
    xa = jnp.where(is_prompt, xa_p_ref[...], xa_s_ref[...])
    xg = jnp.where(is_prompt, xg_p_ref[...], xg_s_ref[...])
    ya = jnp.dot(xa, wa_ref[...], preferred_element_type=F32)
    yg = jnp.dot(xg, wg_ref[...], preferred_element_type=F32)
    bg = bg_ref[...]
    gate_a = _sigmoid(ga_ref[...].astype(F32) + bg[:, :D_MODEL])
    gate_g = _sigmoid(gg_ref[...].astype(F32) + bg[:, D_MODEL:])
    merged = (gate_a * ya + gate_g * yg).astype(BF16)
    z = jnp.dot(merged, wo_ref[...], preferred_element_type=F32)
    if first:
        h_in = jnp.where(is_prompt, h_refs[0][...], h_refs[1][...])
    else:
        h_in = h_refs[0][...]
    h = h_in + _rms(z, npost_ref[...])
    if last:
        y_p_ref, y_s_ref = out_refs

        @pl.when(is_prompt)
        def _():
            y_p_ref[...] = h

        @pl.when(jnp.logical_not(is_prompt))
        def _():
            y_s_ref[...] = h
    else:
        h_out_ref, u_out_ref = out_refs
        h_out_ref[...] = h
        u_out_ref[...] = _rms(h, npre_ref[...]).astype(BF16)


def _output(xa_p, xa_s, xg_p, xg_s, p, h, wa, wg, wo, bg, npost, npre, layer):
    first = layer == 0
    last = layer == DEPTH - 1
    n_prompt_tiles = xa_p.shape[0] // TM_OUT
    n_sample_tiles = xa_s.shape[0] // TM_OUT
    n_tiles = n_prompt_tiles + n_sample_tiles
    t = n_tiles * TM_OUT
    p_idx, s_idx = _two_group_index_maps(n_prompt_tiles)
    row = lambda i: (i, 0)
    wspec = lambda k: pl.BlockSpec((None, k, D_MODEL), lambda i: (layer, 0, 0), pipeline_mode=pl.Buffered(1))
    vspec = lambda n, l: pl.BlockSpec((None, 1, n), lambda i: (l, 0, 0))
    tile = lambda idx: pl.BlockSpec((TM_OUT, D_MODEL), idx)
    h_args = tuple(h) if first else (h,)
    h_specs = [tile(p_idx), tile(s_idx)] if first else [tile(row)]
    if last:
        out_specs = [tile(p_idx), tile(s_idx)]
        out_shape = [jax.ShapeDtypeStruct((n_prompt_tiles * TM_OUT, D_MODEL), F32),
                     jax.ShapeDtypeStruct((n_sample_tiles * TM_OUT, D_MODEL), F32)]
    else:
        out_specs = [tile(row), tile(row)]
        out_shape = [jax.ShapeDtypeStruct((t, D_MODEL), F32), jax.ShapeDtypeStruct((t, D_MODEL), BF16)]
    return pl.pallas_call(
        functools.partial(_out_kernel, n_prompt_tiles, first, last),
        grid=(n_tiles,),
        in_specs=[pl.BlockSpec((TM_OUT, ATTN_WIDTH), p_idx), pl.BlockSpec((TM_OUT, ATTN_WIDTH), s_idx),
                  pl.BlockSpec((TM_OUT, GLA_DV), p_idx), pl.BlockSpec((TM_OUT, GLA_DV), s_idx),
                  pl.BlockSpec((TM_OUT, D_MODEL), lambda i: (i, COL_GA // D_MODEL)),
                  pl.BlockSpec((TM_OUT, D_MODEL), lambda i: (i, COL_GG // D_MODEL))]
                 + h_specs
                 + [wspec(ATTN_WIDTH), wspec(GLA_DV), wspec(D_MODEL),
                    vspec(2 * D_MODEL, layer), vspec(D_MODEL, layer), vspec(D_MODEL, min(layer + 1, DEPTH - 1))],
        out_specs=out_specs,
        out_shape=out_shape,
        compiler_params=pltpu.CompilerParams(
            dimension_semantics=("arbitrary",), vmem_limit_bytes=VMEM_LIMIT),
        name="out",
    )(xa_p, xa_s, xg_p, xg_s, p, p, *h_args, wa, wg, wo, bg, npost, npre)


def kernel(x_prompt, x_sample, cache_k, cache_v, state_gla, norm_pre, norm_post, w_in, b_gate,
           attn_sinks, w_forget_up, b_forget, gla_norm, w_branch_attn, w_branch_gla, w_out):
    batch, seq, _ = x_prompt.shape
    dec_batch, dec_seq, _ = x_sample.shape
    assert dec_seq == CHUNK and seq % (PROMPT_CHUNKS_PER_STEP * CHUNK) == 0
    assert cache_k.shape[2] == WINDOW
    n_prompt = batch * seq
    n_sample = dec_batch * dec_seq

    w_in_t = jnp.swapaxes(w_in, 1, 2)
    wa =w_branch_attn.astype(BF16)
    wg = w_branch_gla.astype(BF16)
    wo = w_out.astype(BF16)
    wfu = jnp.pad(w_forget_up, ((0, 0), (0, LR_PAD - GLA_RANK), (0, 0))).astype(BF16)
    bfg = b_forget.reshape(DEPTH, 1, GLA_DK)
    gn = gla_norm.reshape(DEPTH, 1, GLA_HV)
    bg = b_gate.reshape(DEPTH, 1, 2 * D_MODEL)
    npre = norm_pre.reshape(DEPTH, 1, D_MODEL)
    npost = norm_post.reshape(DEPTH, 1, D_MODEL)
    ck = cache_k.reshape(DEPTH, dec_batch, WINDOW, KV_WIDTH)
    cv = cache_v.reshape(DEPTH, dec_batch, WINDOW, KV_WIDTH)

    xp = x_prompt.reshape(n_prompt, D_MODEL)
    xs = x_sample.reshape(n_sample, D_MODEL)
    u = _prenorm(xp, xs, npre[0])
    h = (xp, xs)

    stacked_p = stacked_s = None
    for l in range(DEPTH):
        p, p_lr = _project(u, w_in_t, l)
        xa_p, xg_p, stacked_p = _mix_prompt(p, p_lr, wfu, bfg, gn, attn_sinks, l, batch, seq, stacked_p)
        xa_s, xg_s, stacked_s = _mix_sample(p, p_lr, wfu, bfg, gn, attn_sinks, ck, cv, state_gla, l, n_prompt,
                                            dec_batch, stacked_s)
        outs = _output(xa_p, xa_s, xg_p, xg_s, p, h, wa, wg, wo, bg, npost, npre, l)
        if l < DEPTH - 1:
            h, u = outs

    y_prompt = outs[0].reshape(batch, seq, D_MODEL)
    y_sample = outs[1].reshape(dec_batch, dec_seq, D_MODEL)
    heads = lambda a: a.reshape(a.shape[:3] + (N_KV_HEADS, HEAD_DIM))
    return (y_prompt, y_sample, heads(stacked_p[1]), heads(stacked_p[2]), stacked_p[0],
            heads(stacked_s[1]), heads(stacked_s[2]), stacked_s[0])
```

```python
import functools
import math

import jax
import jax.numpy as jnp
from jax import lax
from jax.experimental import pallas as pl
from jax.experimental.pallas import tpu as pltpu

F32 = jnp.float32
BF16 = jnp.bfloat16

D_MODEL = 2048
DEPTH = 4
CHUNK = 64
WINDOW = 128
HEAD_DIM = 64
N_Q_HEADS = 16
N_KV_HEADS = 4
GQA_GROUP = N_Q_HEADS // N_KV_HEADS
ATTN_WIDTH = N_Q_HEADS * HEAD_DIM
KV_WIDTH = N_KV_HEADS * HEAD_DIM
GLA_HEADS = 4
GLA_DK = 512
GLA_DV = 1024
GLA_HK = GLA_DK // GLA_HEADS
GLA_HV = GLA_DV // GLA_HEADS
GLA_RANK = 16
GLA_TAU = 16.0
NORM_EPS = 1e-6
LOG2E = math.log2(math.e)
IN_SIZES = (ATTN_WIDTH, KV_WIDTH, KV_WIDTH, ATTN_WIDTH,
            GLA_DK, GLA_DK, GLA_DV, GLA_DV, GLA_RANK, D_MODEL, D_MODEL)

KEYS = WINDOW + CHUNK
SUB = 8
NSUB = CHUNK // SUB

TM_PROJ = 3072
TM_PROJ_SUB = 512
TN_PROJ = 512
TM_OUT = 256
TM_NORM = 512
PROMPT_CHUNKS_PER_STEP = 4
SAMPLE_SEQS_PER_STEP = 4

VMEM_LIMIT = 56 * 1024 * 1024

SRC_COLS = {}
_acc = 0
for _name, _size in zip(("qa", "ka", "va", "za", "qg", "kg", "vg", "zg", "lr", "ga", "gg"), IN_SIZES):
    SRC_COLS[_name] = (_acc, _size)
    _acc += _size
IN_COLS = _acc

PACKED_ORDER = ("ga", "gg", "qa", "za", "vg", "zg", "qg", "kg", "ka", "va")
PACKED_COL = {}
PACKED_SRC_ROWS = []
_acc = 0
for _name in PACKED_ORDER:
    _start, _size = SRC_COLS[_name]
    PACKED_COL[_name] = _acc
    if _acc % TN_PROJ == 0:
        PACKED_SRC_ROWS += [_start + k for k in range(0, max(_size, TN_PROJ), TN_PROJ)]
    _acc += _size
PACKED_COLS = _acc
COL_GA, COL_GG, COL_QA, COL_ZA, COL_VG, COL_ZG, COL_QG, COL_KG, COL_KA, COL_VA = (
    PACKED_COL[_n] for _n in PACKED_ORDER)
LR_PAD = 128
assert SRC_COLS["va"][0] == SRC_COLS["ka"][0] + KV_WIDTH and 2 * KV_WIDTH == TN_PROJ
assert len(PACKED_SRC_ROWS) * TN_PROJ == PACKED_COLS
assert all(_r % GLA_RANK == 0 for _r in PACKED_SRC_ROWS)


def _rms(x, g):
    return x * lax.rsqrt(jnp.mean(x * x, axis=-1, keepdims=True) + NORM_EPS) * g


def _two_group_index_maps(n_prompt_tiles):
    p_idx = lambda i: (jnp.minimum(i, n_prompt_tiles - 1), 0)
    s_idx = lambda i: (jnp.maximum(i - n_prompt_tiles, 0), 0)
    return p_idx, s_idx


def _prenorm_kernel(n_prompt_tiles, xp_ref, xs_ref, g_ref, u_ref):
    x = jnp.where(pl.program_id(0) < n_prompt_tiles, xp_ref[...], xs_ref[...])
    u_ref[...] = _rms(x, g_ref[...]).astype(BF16)


def _prenorm(x_prompt, x_sample, g):
    n_prompt_tiles = x_prompt.shape[0] // TM_NORM
    n_tiles = n_prompt_tiles + x_sample.shape[0] // TM_NORM
    p_idx, s_idx = _two_group_index_maps(n_prompt_tiles)
    return pl.pallas_call(
        functools.partial(_prenorm_kernel, n_prompt_tiles),
        grid=(n_tiles,),
        in_specs=[pl.BlockSpec((TM_NORM, D_MODEL), p_idx),
                  pl.BlockSpec((TM_NORM, D_MODEL), s_idx),
                  pl.BlockSpec((1, D_MODEL), lambda i: (0, 0))],
        out_specs=pl.BlockSpec((TM_NORM, D_MODEL), lambda i: (i, 0)),
        out_shape=jax.ShapeDtypeStruct((n_tiles * TM_NORM, D_MODEL), BF16),
        compiler_params=pltpu.CompilerParams(dimension_semantics=("arbitrary",)),
        name="prenorm",
    )(x_prompt, x_sample, g)


def _nt_dot(a, b):
    return lax.dot_general(a, b, (((1,), (1,)), ((), ())), preferred_element_type=F32)


def _proj_kernel(src_rows_ref, u_ref, w_ref, wlr_ref, o_ref, olr_ref):
    del src_rows_ref
    w = w_ref[...].astype(BF16)
    for m in range(TM_PROJ // TM_PROJ_SUB):
        rows = slice(m * TM_PROJ_SUB, (m + 1) * TM_PROJ_SUB)
        o_ref[rows, :] = _nt_dot(u_ref[rows, :], w).astype(BF16)

    @pl.when(pl.program_id(1) == 0)
    def _():
        wlr = wlr_ref[...].astype(BF16)
        for m in range(TM_PROJ // TM_PROJ_SUB):
            rows = slice(m * TM_PROJ_SUB, (m + 1) * TM_PROJ_SUB)
            olr_ref[rows, :] = _nt_dot(u_ref[rows, :], wlr).astype(BF16)


def _project(u, w_in_t, layer):
    t = u.shape[0]
    lr_row = SRC_COLS["lr"][0]
    wspec = lambda rows, index_map: pl.BlockSpec((None, pl.Element(rows), pl.Element(D_MODEL)), index_map)
    grid_spec = pltpu.PrefetchScalarGridSpec(
        num_scalar_prefetch=1,
        grid=(t // TM_PROJ, PACKED_COLS // TN_PROJ),
        in_specs=[pl.BlockSpec((TM_PROJ, D_MODEL), lambda i, j, src: (i, 0)),
                  wspec(TN_PROJ, lambda i, j, src: (layer, src[j] * GLA_RANK, 0)),
                  wspec(LR_PAD, lambda i, j, src: (layer, lr_row, 0))],
        out_specs=[pl.BlockSpec((TM_PROJ, TN_PROJ), lambda i, j, src: (i, j)),
                   pl.BlockSpec((TM_PROJ, LR_PAD), lambda i, j, src: (i, 0))])
    return pl.pallas_call(
        _proj_kernel,
        grid_spec=grid_spec,
        out_shape=[jax.ShapeDtypeStruct((t, PACKED_COLS), BF16),
                   jax.ShapeDtypeStruct((t, LR_PAD), BF16)],
        compiler_params=pltpu.CompilerParams(
            dimension_semantics=("arbitrary", "arbitrary"), vmem_limit_bytes=VMEM_LIMIT),
        name="proj",
    )(jnp.asarray([r // GLA_RANK for r in PACKED_SRC_ROWS], jnp.int32), u, w_in_t, w_in_t)


def _silu(x):
    half = 0.5 * x
    return half + half * jnp.tanh(half)


def _log_sigmoid(x):
    return jnp.minimum(x, 0.0) - jnp.log(1.0 + jnp.exp(-jnp.abs(x)))


def _alibi_slope(h):
    return math.pow(2.0, -8.0 * (h + 1) / N_Q_HEADS)


GROUP_ROWS = GQA_GROUP * CHUNK


def _fill_alibi(bias_ref):
    s = lax.broadcasted_iota(jnp.int32, (KEYS, CHUNK), 0)
    t = lax.broadcasted_iota(jnp.int32, (KEYS, CHUNK), 1)
    dist = jnp.abs(WINDOW + t - s).astype(F32)
    for kh in range(N_KV_HEADS):
        for g in range(GQA_GROUP):
            bias_ref[kh, :, g * CHUNK:(g + 1) * CHUNK] = dist * (-_alibi_slope(kh * GQA_GROUP + g))


def _attention_chunk(r, qa_ref, za_ref, kwin, vwin, sinks_ref, layer, bias_ref, key_lo, xa_ref):
    rows = pl.ds(r, CHUNK)
    lane_head = lax.broadcasted_iota(jnp.int32, (1, GROUP_ROWS), 1) // CHUNK
    if key_lo is not None:
        key_ok = lax.broadcasted_iota(jnp.int32, (KEYS, GROUP_ROWS), 0) >= key_lo
    for kh in range(N_KV_HEADS):
        k = kwin[:, kh * HEAD_DIM:(kh + 1) * HEAD_DIM]
        v = vwin[:, kh * HEAD_DIM:(kh + 1) * HEAD_DIM]
        q = jnp.concatenate(
            [qa_ref[rows, pl.ds((kh * GQA_GROUP + g) * HEAD_DIM, HEAD_DIM)] for g in range(GQA_GROUP)],
            axis=0)
        q = q * jnp.asarray(HEAD_DIM ** -0.5, BF16)
        s = _nt_dot(k, q) + bias_ref[kh]
        if key_lo is not None:
            s = jnp.where(key_ok, s, -jnp.inf)
        sink = jnp.zeros((1, GROUP_ROWS), F32)
        for g in range(GQA_GROUP):
            sink = jnp.where(lane_head == g, sinks_ref[layer, kh * GQA_GROUP + g], sink)
        m = jnp.maximum(jnp.max(s, axis=0, keepdims=True), sink)
        p = jnp.exp(s - m)
        den = jnp.sum(p, axis=0, keepdims=True) + jnp.exp(sink - m)
        o_t = lax.dot_general(v, p.astype(BF16), (((0,), (0,)), ((), ())), preferred_element_type=F32)
        o = jnp.transpose(o_t / den)
        for g in range(GQA_GROUP):
            h = kh * GQA_GROUP + g
            cols = pl.ds(h * HEAD_DIM, HEAD_DIM)
            z = za_ref[rows, cols].astype(F32)
            xa_ref[rows, cols] = (o[g * CHUNK:(g + 1) * CHUNK] * _silu(z)).astype(BF16)


def _gla_prepare(lr_ref, kg_ref, wfu_ref, bf_ref, k_scr, b_scr):
    n = lr_ref.shape[0]
    la2 = _log2_forget(lr_ref[...], wfu_ref, bf_ref)
    row = lax.broadcasted_iota(jnp.int32, (n, n), 0)
    col = lax.broadcasted_iota(jnp.int32, (n, n), 1)
    tri = jnp.where(row // CHUNK == col // CHUNK, row - col, -1) >= 0
    tri = jnp.where(tri, 1.0, 0.0).astype(BF16)
    la_hi = la2.astype(BF16)
    rem = la2 - la_hi.astype(F32)
    la_mid = rem.astype(BF16)
    la_lo = (rem - la_mid.astype(F32)).astype(BF16)
    b_scr[...] = (jnp.dot(tri, la_hi, preferred_element_type=F32)
                  + jnp.dot(tri, la_mid, preferred_element_type=F32)
                  + jnp.dot(tri, la_lo, preferred_element_type=F32))
    k_scr[...] = kg_ref[...].astype(F32)


def _gla_states(chunk_states, vg_ref, k_scr, b_scr, s0_scr):
    for hh in range(GLA_HEADS):
        kcols = pl.ds(hh * GLA_HK, GLA_HK)
        vcols = pl.ds(hh * GLA_HV, GLA_HV)
        steps = []
        for c in range(len(chunk_states)):
            rows = pl.ds(c * CHUNK, CHUNK)
            b_end = b_scr[c * CHUNK + CHUNK - 1:(c + 1) * CHUNK, kcols]
            k_til = (k_scr[rows, kcols] * jnp.exp2(b_end - b_scr[rows, kcols])).astype(BF16)
            upd = lax.dot_general(k_til, vg_ref[rows, vcols], (((0,), (0,)), ((), ())),
                                  preferred_element_type=F32)
            e_col = jnp.transpose(jnp.broadcast_to(jnp.exp2(b_end), (GLA_HK, GLA_HK)))
            steps.append((jnp.concatenate([e_col, e_col], axis=1), upd))
        s = None
        for c, ((in_ref, out_ref), (decay, upd)) in enumerate(zip(chunk_states, steps)):
            if c == 0 or in_ref is not chunk_states[c - 1][1]:
                s = in_ref[hh]
            s0_scr[c, hh] = s.astype(BF16)
            s = s * decay + upd
            if c + 1 == len(chunk_states) or chunk_states[c + 1][0] is not out_ref:
                out_ref[hh] = s


def _gla_chunk(c, qg_ref, vg_ref, zg_ref, s0_scr, gnorm, k_scr, b_scr, xg_ref):
    r = c * CHUNK
    rows = pl.ds(r, CHUNK)
    lane = lax.broadcasted_iota(jnp.int32, (SUB, CHUNK), 1)
    row0 = lax.broadcasted_iota(jnp.int32, (SUB, CHUNK), 0)

    for hh in range(GLA_HEADS):
        kcols = pl.ds(hh * GLA_HK, GLA_HK)
        vcols = pl.ds(hh * GLA_HV, GLA_HV)
        q = qg_ref[rows, kcols].astype(F32) * (GLA_HK ** -0.5)
        k = k_scr[rows, kcols]
        v = vg_ref[rows, vcols]
        b = b_scr[rows, kcols]
        blk = lambda x, j: x[j * SUB:(j + 1) * SUB]
        b_row = lambda t: b_scr[r + t:r + t + 1, kcols]
        b_last = [b_row(j * SUB + SUB - 1) for j in range(NSUB)]

        k_hat = jnp.concatenate([blk(k, j) * jnp.exp2(b_last[j] - blk(b, j)) for j in range(NSUB)], axis=0)
        lhs = jnp.concatenate([q[(j + 1) * SUB:] * jnp.exp2(b[(j + 1) * SUB:] - b_last[j])
                               for j in range(NSUB - 1)], axis=0)
        rr = _nt_dot(lhs.astype(BF16), k_hat.astype(BF16))

        score_rows = []
        for i in range(NSUB):
            acc = jnp.zeros((SUB, CHUNK), F32)
            for j in range(i):
                base = sum((NSUB - 1 - jj) * SUB for jj in range(j)) + (i - j - 1) * SUB
                acc = jnp.where(lane // SUB == j, rr[base:base + SUB], acc)
            qi, bi = blk(q, i), blk(b, i)
            for s in range(i * SUB, (i + 1) * SUB):
                prod = (qi * k_scr[r + s:r + s + 1, kcols]) * jnp.exp2(bi - b_row(s))
                acc = jnp.where(lane == s, jnp.sum(prod, axis=1, keepdims=True), acc)
            score_rows.append(jnp.where(lane <= row0 + i * SUB, acc, 0.0))
        scores = jnp.concatenate(score_rows, axis=0)

        o = (jnp.dot(scores.astype(BF16), v, preferred_element_type=F32)
             + jnp.dot((q * jnp.exp2(b)).astype(BF16), s0_scr[c, hh], preferred_element_type=F32))

        y = _rms(o, gnorm)
        z = zg_ref[rows, vcols].astype(F32)
        xg_ref[rows, vcols] = (y * _silu(z)).astype(BF16)


def _log2_forget(lr, wfu_ref, bf_ref):
    x = jnp.dot(lr, wfu_ref[...], preferred_element_type=F32) + bf_ref[...]
    return _log_sigmoid(x) * (LOG2E / GLA_TAU)


def _mix_prompt_kernel(layer, n_steps, chunks, *refs):
    n_carried = 3 if layer > 0 else 0
    (qa_ref, za_ref, vg_ref, zg_ref, qg_ref, kg_ref, ka_ref, va_ref, lr_ref,
     wfu_ref, bf_ref, gn_ref, sinks_ref) = refs[:13]
    (xa_ref, xg_ref, st_ref, knew_ref, vnew_ref,
     kfull, vfull, state, bias_scr, k_scr, b_scr, s0_scr) = refs[13 + n_carried:]
    b = pl.program_id(0)
    t = pl.program_id(1)

    @pl.when((b == 0) & (t == 0))
    def _():
        _fill_alibi(bias_scr)

    @pl.when(t == 0)
    def _():
        kfull[0:WINDOW, :] = jnp.zeros((WINDOW, KV_WIDTH), BF16)
        vfull[0:WINDOW, :] = jnp.zeros((WINDOW, KV_WIDTH), BF16)
        state[...] = jnp.zeros(state.shape, F32)

    kfull[WINDOW:, :] = ka_ref[...]
    vfull[WINDOW:, :] = va_ref[...]
    _gla_prepare(lr_ref, kg_ref, wfu_ref, bf_ref, k_scr, b_scr)
    _gla_states([(state, state)] * chunks, vg_ref, k_scr, b_scr, s0_scr)
    gnorm = gn_ref[...]

    for c in range(chunks):
        r = c * CHUNK
        kwin = kfull[r:r + KEYS, :]
        vwin = vfull[r:r + KEYS, :]
        key_lo = (WINDOW // CHUNK - (t * chunks + c)) * CHUNK if c < WINDOW // CHUNK else None
        _attention_chunk(r, qa_ref, za_ref, kwin, vwin, sinks_ref, layer, bias_scr, key_lo, xa_ref)
        _gla_chunk(c, qg_ref, vg_ref, zg_ref, s0_scr, gnorm, k_scr, b_scr, xg_ref)

    kfull[0:WINDOW, :] = kfull[chunks * CHUNK:, :]
    vfull[0:WINDOW, :] = vfull[chunks * CHUNK:, :]

    @pl.when(t == n_steps - 1)
    def _():
        st_ref[0] = state[...]
        knew_ref[0] = ka_ref[chunks * CHUNK - WINDOW:, :].astype(F32)
        vnew_ref[0] = va_ref[chunks * CHUNK - WINDOW:, :].astype(F32)


def _mix_sample_kernel(layer, n_seqs, *refs):
    n_carried = 3 if layer > 0 else 0
    (qa_ref, za_ref, vg_ref, zg_ref, qg_ref, kg_ref, ka_ref, va_ref, lr_ref,
     wfu_ref, bf_ref, gn_ref, sinks_ref, ck_ref, cv_ref, s0_ref) = refs[:16]
    xa_ref, xg_ref, st_ref, knew_ref, vnew_ref, bias_scr, k_scr, b_scr, s0_scr = refs[16 + n_carried:]

    @pl.when(pl.program_id(0) == 0)
    def _():
        _fill_alibi(bias_scr)

    _gla_prepare(lr_ref, kg_ref, wfu_ref, bf_ref, k_scr, b_scr)
    _gla_states([(s0_ref.at[g], st_ref.at[g]) for g in range(n_seqs)], vg_ref, k_scr, b_scr, s0_scr)
    gnorm = gn_ref[...]
    for g in range(n_seqs):
        r = g * CHUNK
        k_new = ka_ref[r:r + CHUNK, :]
        v_new = va_ref[r:r + CHUNK, :]
        kwin = jnp.concatenate([ck_ref[g].astype(BF16), k_new], axis=0)
        vwin = jnp.concatenate([cv_ref[g].astype(BF16), v_new], axis=0)
        knew_ref[g, 0:WINDOW - CHUNK, :] = ck_ref[g, CHUNK:, :]
        knew_ref[g, WINDOW - CHUNK:, :] = k_new.astype(F32)
        vnew_ref[g, 0:WINDOW - CHUNK, :] = cv_ref[g, CHUNK:, :]
        vnew_ref[g, WINDOW - CHUNK:, :] = v_new.astype(F32)
        _attention_chunk(r, qa_ref, za_ref, kwin, vwin, sinks_ref, layer, bias_scr, None, xa_ref)
        _gla_chunk(g, qg_ref, vg_ref, zg_ref, s0_scr, gnorm, k_scr, b_scr, xg_ref)


def _section_specs(rows, row_index):
    def spec(width, col):
        return pl.BlockSpec((rows, width), functools.partial(
            lambda cb, *g: (row_index(*g), cb), col // width))
    return [spec(ATTN_WIDTH, COL_QA), spec(ATTN_WIDTH, COL_ZA), spec(GLA_DV, COL_VG), spec(GLA_DV, COL_ZG),
            spec(GLA_DK, COL_QG), spec(GLA_DK, COL_KG), spec(KV_WIDTH, COL_KA), spec(KV_WIDTH, COL_VA),
            spec(LR_PAD, 0)]


def _layer_weight_specs(layer, ngrid):
    zeros = (0,) * 2
    return [pl.BlockSpec((None, LR_PAD, GLA_DK), lambda *g: (layer,) + zeros),
            pl.BlockSpec((None, 1, GLA_DK), lambda *g: (layer,) + zeros),
            pl.BlockSpec((None, 1, GLA_HV), lambda *g: (layer,) + zeros),
            pl.BlockSpec(memory_space=pltpu.SMEM)]


def _stacked_outputs(layer, batch, seqs_per_step, n_inputs, carried):
    lead = lambda *g: (layer, g[0])
    nb = seqs_per_step
    specs = [pl.BlockSpec((None, nb, GLA_HEADS, GLA_HK, GLA_HV), lambda *g: lead(*g) + (0, 0, 0)),
             pl.BlockSpec((None, nb, WINDOW, KV_WIDTH), lambda *g: lead(*g) + (0, 0)),
             pl.BlockSpec((None, nb, WINDOW, KV_WIDTH), lambda *g: lead(*g) + (0, 0))]
    shapes = [jax.ShapeDtypeStruct((DEPTH, batch, GLA_HEADS, GLA_HK, GLA_HV), F32),
              jax.ShapeDtypeStruct((DEPTH, batch, WINDOW, KV_WIDTH), F32),
              jax.ShapeDtypeStruct((DEPTH, batch, WINDOW, KV_WIDTH), F32)]
    if carried is None:
        return specs, shapes, [], {}
    alias_specs = [pl.BlockSpec(memory_space=pl.ANY)] * 3
    aliases = {n_inputs + k: 2 + k for k in range(3)}
    return specs, shapes, alias_specs, aliases


def _mix_prompt(p, p_lr, wfu, bfg, gn, sinks, layer, batch, seq, carried):
    chunks = PROMPT_CHUNKS_PER_STEP
    rows = chunks * CHUNK
    n_steps = seq // rows
    row_index = lambda b, t: b * n_steps + t
    in_specs = _section_specs(rows, row_index) + _layer_weight_specs(layer, 2)
    st_specs, st_shapes, alias_specs, aliases = _stacked_outputs(layer, batch, 1, len(in_specs), carried)
    outs = pl.pallas_call(
        functools.partial(_mix_prompt_kernel, layer, n_steps, chunks),
        grid=(batch, n_steps),
        in_specs=in_specs + alias_specs,
        out_specs=[pl.BlockSpec((rows, ATTN_WIDTH), lambda b, t: (b * n_steps + t, 0)),
                   pl.BlockSpec((rows, GLA_DV), lambda b, t: (b * n_steps + t, 0))] + st_specs,
        out_shape=[jax.ShapeDtypeStruct((batch * seq, ATTN_WIDTH), BF16),
                   jax.ShapeDtypeStruct((batch * seq, GLA_DV), BF16)] + st_shapes,
        input_output_aliases=aliases,
        scratch_shapes=[pltpu.VMEM((rows + WINDOW, KV_WIDTH), BF16),
                        pltpu.VMEM((rows + WINDOW, KV_WIDTH), BF16),
                        pltpu.VMEM((GLA_HEADS, GLA_HK, GLA_HV), F32),
                        pltpu.VMEM((N_KV_HEADS, KEYS, GROUP_ROWS), F32),
                        pltpu.VMEM((rows, GLA_DK), F32),
                        pltpu.VMEM((rows, GLA_DK), F32),
                        pltpu.VMEM((rows // CHUNK, GLA_HEADS, GLA_HK, GLA_HV), BF16)],
        compiler_params=pltpu.CompilerParams(
            dimension_semantics=("arbitrary", "arbitrary"), vmem_limit_bytes=VMEM_LIMIT),
        name="mix_prompt",
    )(*([p] * 8), p_lr, wfu, bfg, gn, sinks, *(carried or ()))
    return outs[0], outs[1], tuple(outs[2:])


def _mix_sample(p, p_lr, wfu, bfg, gn, sinks, cache_k, cache_v, state, layer, row0, batch, carried):
    nb = SAMPLE_SEQS_PER_STEP
    rows = nb * CHUNK
    assert row0 % rows == 0 and batch % nb == 0
    row_block0 = row0 // rows
    row_index = lambda b: row_block0 + b
    cache_spec = pl.BlockSpec((None, nb, WINDOW, KV_WIDTH), lambda b: (layer, b, 0, 0))
    state_spec = pl.BlockSpec((None, nb, GLA_HEADS, GLA_HK, GLA_HV), lambda b: (layer, b, 0, 0, 0))
    in_specs = (_section_specs(rows, row_index) + _layer_weight_specs(layer, 1)
                + [cache_spec, cache_spec, state_spec])
    st_specs, st_shapes, alias_specs, aliases = _stacked_outputs(layer, batch, nb, len(in_specs), carried)
    outs = pl.pallas_call(
        functools.partial(_mix_sample_kernel, layer, nb),
        grid=(batch // nb,),
        in_specs=in_specs + alias_specs,
        out_specs=[pl.BlockSpec((rows, ATTN_WIDTH), lambda b: (b, 0)),
                   pl.BlockSpec((rows, GLA_DV), lambda b: (b, 0))] + st_specs,
        out_shape=[jax.ShapeDtypeStruct((batch * CHUNK, ATTN_WIDTH), BF16),
                   jax.ShapeDtypeStruct((batch * CHUNK, GLA_DV), BF16)] + st_shapes,
        input_output_aliases=aliases,
        scratch_shapes=[pltpu.VMEM((N_KV_HEADS, KEYS, GROUP_ROWS), F32),
                        pltpu.VMEM((rows, GLA_DK), F32),
                        pltpu.VMEM((rows, GLA_DK), F32),
                        pltpu.VMEM((rows // CHUNK, GLA_HEADS, GLA_HK, GLA_HV), BF16)],
        compiler_params=pltpu.CompilerParams(
            dimension_semantics=("arbitrary",), vmem_limit_bytes=VMEM_LIMIT),
        name="mix_sample",
    )(*([p] * 8), p_lr, wfu, bfg, gn, sinks, cache_k, cache_v, state, *(carried or ()))
    return outs[0], outs[1], tuple(outs[2:])


def _sigmoid(x):
    return 1.0 / (1.0 + jnp.exp(-x))


def _out_kernel(n_prompt_tiles, first, last, *refs):
    xa_p_ref, xa_s_ref, xg_p_ref, xg_s_ref, ga_ref, gg_ref = refs[:6]
    n_h = 2 if first else 1
    h_refs = refs[6:6 + n_h]
    wa_ref, wg_ref, wo_ref, bg_ref, npost_ref, npre_ref = refs[6 + n_h:12 + n_h]
    out_refs = refs[12 + n_h:]

    is_prompt = pl.program_id(0) < n_prompt_tiles
    xa = jnp.where(is_prompt, xa_p_ref[...], xa_s_ref[...])
    xg = jnp.where(is_prompt, xg_p_ref[...], xg_s_ref[...])
    ya = jnp.dot(xa, wa_ref[...], preferred_element_type=F32)
    yg = jnp.dot(xg, wg_ref[...], preferred_element_type=F32)
    bg = bg_ref[...]
    gate_a = _sigmoid(ga_ref[...].astype(F32) + bg[:, :D_MODEL])
    gate_g = _sigmoid(gg_ref[...].astype(F32) + bg[:, D_MODEL:])
    merged = (gate_a * ya + gate_g * yg).astype(BF16)
    z = jnp.dot(merged, wo_ref[...], preferred_element_type=F32)
    if first:
        h_in = jnp.where(is_prompt, h_refs[0][...], h_refs[1][...])
    else:
        h_in = h_refs[0][...]
    h = h_in + _rms(z, npost_ref[...])
    if last:
        y_p_ref, y_s_ref = out_refs

        @pl.when(is_prompt)
        def _():
            y_p_ref[...] = h

        @pl.when(jnp.logical_not(is_prompt))
        def _():
            y_s_ref[...] = h
    else:
        h_out_ref, u_out_ref = out_refs
        h_out_ref[...] = h
        u_out_ref[...] = _rms(h, npre_ref[...]).astype(BF16)


def _output(xa_p, xa_s, xg_p, xg_s, p, h, wa, wg, wo, bg, npost, npre, layer):
    first = layer == 0
    last = layer == DEPTH - 1
    n_prompt_tiles = xa_p.shape[0] // TM_OUT
    n_sample_tiles = xa_s.shape[0] // TM_OUT
    n_tiles = n_prompt_tiles + n_sample_tiles
    t = n_tiles * TM_OUT
    p_idx, s_idx = _two_group_index_maps(n_prompt_tiles)
    row = lambda i: (i, 0)
    wspec = lambda k: pl.BlockSpec((None, k, D_MODEL), lambda i: (layer, 0, 0), pipeline_mode=pl.Buffered(1))
    vspec = lambda n, l: pl.BlockSpec((None, 1, n), lambda i: (l, 0, 0))
    tile = lambda idx: pl.BlockSpec((TM_OUT, D_MODEL), idx)
    h_args = tuple(h) if first else (h,)
    h_specs = [tile(p_idx), tile(s_idx)] if first else [tile(row)]
    if last:
        out_specs = [tile(p_idx), tile(s_idx)]
        out_shape = [jax.ShapeDtypeStruct((n_prompt_tiles * TM_OUT, D_MODEL), F32),
                     jax.ShapeDtypeStruct((n_sample_tiles * TM_OUT, D_MODEL), F32)]
    else:
        out_specs = [tile(row), tile(row)]
        out_shape = [jax.ShapeDtypeStruct((t, D_MODEL), F32), jax.ShapeDtypeStruct((t, D_MODEL), BF16)]
    return pl.pallas_call(
        functools.partial(_out_kernel, n_prompt_tiles, first, last),
        grid=(n_tiles,),
        in_specs=[pl.BlockSpec((TM_OUT, ATTN_WIDTH), p_idx), pl.BlockSpec((TM_OUT, ATTN_WIDTH), s_idx),
                  pl.BlockSpec((TM_OUT, GLA_DV), p_idx), pl.BlockSpec((TM_OUT, GLA_DV), s_idx),
                  pl.BlockSpec((TM_OUT, D_MODEL), lambda i: (i, COL_GA // D_MODEL)),
                  pl.BlockSpec((TM_OUT, D_MODEL), lambda i: (i, COL_GG // D_MODEL))]
                 + h_specs
                 + [wspec(ATTN_WIDTH), wspec(GLA_DV), wspec(D_MODEL),
                    vspec(2 * D_MODEL, layer), vspec(D_MODEL, layer), vspec(D_MODEL, min(layer + 1, DEPTH - 1))],
        out_specs=out_specs,
        out_shape=out_shape,
        compiler_params=pltpu.CompilerParams(
            dimension_semantics=("arbitrary",), vmem_limit_bytes=VMEM_LIMIT),
        name="out",
    )(xa_p, xa_s, xg_p, xg_s, p, p, *h_args, wa, wg, wo, bg, npost, npre)


def kernel(x_prompt, x_sample, cache_k, cache_v, state_gla, norm_pre, norm_post, w_in, b_gate,
           attn_sinks, w_forget_up, b_forget, gla_norm, w_branch_attn, w_branch_gla, w_out):
    batch, seq, _ = x_prompt.shape
    dec_batch, dec_seq, _ = x_sample.shape
    assert dec_seq == CHUNK and seq % (PROMPT_CHUNKS_PER_STEP * CHUNK) == 0
    assert cache_k.shape[2] == WINDOW
    n_prompt = batch * seq
    n_sample = dec_batch * dec_seq

    w_in_t = jnp.swapaxes(w_in, 1, 2)
    wa =w_branch_attn.astype(BF16)
    wg = w_branch_gla.astype(BF16)
    wo = w_out.astype(BF16)
    wfu = jnp.pad(w_forget_up, ((0, 0), (0, LR_PAD - GLA_RANK), (0, 0))).astype(BF16)
    bfg = b_forget.reshape(DEPTH, 1, GLA_DK)
    gn = gla_norm.reshape(DEPTH, 1, GLA_HV)
    bg = b_gate.reshape(DEPTH, 1, 2 * D_MODEL)
    npre = norm_pre.reshape(DEPTH, 1, D_MODEL)
    npost = norm_post.reshape(DEPTH, 1, D_MODEL)
    ck = cache_k.reshape(DEPTH, dec_batch, WINDOW, KV_WIDTH)
    cv = cache_v.reshape(DEPTH, dec_batch, WINDOW, KV_WIDTH)

    xp = x_prompt.reshape(n_prompt, D_MODEL)
    xs = x_sample.reshape(n_sample, D_MODEL)
    u = _prenorm(xp, xs, npre[0])
    h = (xp, xs)

    stacked_p = stacked_s = None
    for l in range(DEPTH):
        p, p_lr = _project(u, w_in_t, l)
        xa_p, xg_p, stacked_p = _mix_prompt(p, p_lr, wfu, bfg, gn, attn_sinks, l, batch, seq, stacked_p)
        xa_s, xg_s, stacked_s = _mix_sample(p, p_lr, wfu, bfg, gn, attn_sinks, ck, cv, state_gla, l, n_prompt,
                                            dec_batch, stacked_s)
        outs = _output(xa_p, xa_s, xg_p, xg_s, p, h, wa, wg, wo, bg, npost, npre, l)
        if l < DEPTH - 1:
            h, u = outs

    y_prompt = outs[0].reshape(batch, seq, D_MODEL)
    y_sample = outs[1].reshape(dec_batch, dec_seq, D_MODEL)
    heads = lambda a: a.reshape(a.shape[:3] + (N_KV_HEADS, HEAD_DIM))
    return (y_prompt, y_sample, heads(stacked_p[1]), heads(stacked_p[2]), stacked_p[0],
            heads(stacked_s[1]), heads(stacked_s[2]), stacked_s[0])
```

```python
import functools
import math

import jax
import jax.numpy as jnp
from jax import lax
from jax.experimental import pallas as pl
from jax.experimental.pallas import tpu as pltpu

F32 = jnp.float32
BF16 = jnp.bfloat16

D_MODEL = 2048
DEPTH = 4
CHUNK = 64
WINDOW = 128
HEAD_DIM = 64
N_Q_HEADS = 16
N_KV_HEADS = 4
GQA_GROUP = N_Q_HEADS // N_KV_HEADS
ATTN_WIDTH = N_Q_HEADS * HEAD_DIM
KV_WIDTH = N_KV_HEADS * HEAD_DIM
GLA_HEADS = 4
GLA_DK = 512
GLA_DV = 1024
GLA_HK = GLA_DK // GLA_HEADS
GLA_HV = GLA_DV // GLA_HEADS
GLA_RANK = 16
GLA_TAU = 16.0
NORM_EPS = 1e-6
LOG2E = math.log2(math.e)
IN_SIZES = (ATTN_WIDTH, KV_WIDTH, KV_WIDTH, ATTN_WIDTH,
            GLA_DK, GLA_DK, GLA_DV, GLA_DV, GLA_RANK, D_MODEL, D_MODEL)

KEYS = WINDOW + CHUNK
SUB = 8
NSUB = CHUNK // SUB

TM_PROJ = 3072
TM_PROJ_SUB = 512
TN_PROJ = 512
TM_OUT = 256
TM_NORM = 512
PROMPT_CHUNKS_PER_STEP = 4
SAMPLE_SEQS_PER_STEP = 4

VMEM_LIMIT = 56 * 1024 * 1024

SRC_COLS = {}
_acc = 0
for _name, _size in zip(("qa", "ka", "va", "za", "qg", "kg", "vg", "zg", "lr", "ga", "gg"), IN_SIZES):
    SRC_COLS[_name] = (_acc, _size)
    _acc += _size
IN_COLS = _acc

PACKED_ORDER = ("ga", "gg", "qa", "za", "vg", "zg", "qg", "kg", "ka", "va")
PACKED_COL = {}
PACKED_SRC_ROWS = []
_acc = 0
for _name in PACKED_ORDER:
    _start, _size = SRC_COLS[_name]
    PACKED_COL[_name] = _acc
    if _acc % TN_PROJ == 0:
        PACKED_SRC_ROWS += [_start + k for k in range(0, max(_size, TN_PROJ), TN_PROJ)]
    _acc += _size
PACKED_COLS = _acc
COL_GA, COL_GG, COL_QA, COL_ZA, COL_VG, COL_ZG, COL_QG, COL_KG, COL_KA, COL_VA = (
    PACKED_COL[_n] for _n in PACKED_ORDER)
LR_PAD = 128
assert SRC_COLS["va"][0] == SRC_COLS["ka"][0] + KV_WIDTH and 2 * KV_WIDTH == TN_PROJ
assert len(PACKED_SRC_ROWS) * TN_PROJ == PACKED_COLS
assert all(_r % GLA_RANK == 0 for _r in PACKED_SRC_ROWS)


def _rms(x, g):
    return x * lax.rsqrt(jnp.mean(x * x, axis=-1, keepdims=True) + NORM_EPS) * g


def _two_group_index_maps(n_prompt_tiles):
    p_idx = lambda i: (jnp.minimum(i, n_prompt_tiles - 1), 0)
    s_idx = lambda i: (jnp.maximum(i - n_prompt_tiles, 0), 0)
    return p_idx, s_idx


def _prenorm_kernel(n_prompt_tiles, xp_ref, xs_ref, g_ref, u_ref):
    x = jnp.where(pl.program_id(0) < n_prompt_tiles, xp_ref[...], xs_ref[...])
    u_ref[...] = _rms(x, g_ref[...]).astype(BF16)


def _prenorm(x_prompt, x_sample, g):
    n_prompt_tiles = x_prompt.shape[0] // TM_NORM
    n_tiles = n_prompt_tiles + x_sample.shape[0] // TM_NORM
    p_idx, s_idx = _two_group_index_maps(n_prompt_tiles)
    return pl.pallas_call(
        functools.partial(_prenorm_kernel, n_prompt_tiles),
        grid=(n_tiles,),
        in_specs=[pl.BlockSpec((TM_NORM, D_MODEL), p_idx),
                  pl.BlockSpec((TM_NORM, D_MODEL), s_idx),
                  pl.BlockSpec((1, D_MODEL), lambda i: (0, 0))],
        out_specs=pl.BlockSpec((TM_NORM, D_MODEL), lambda i: (i, 0)),
        out_shape=jax.ShapeDtypeStruct((n_tiles * TM_NORM, D_MODEL), BF16),
        compiler_params=pltpu.CompilerParams(dimension_semantics=("arbitrary",)),
        name="prenorm",
    )(x_prompt, x_sample, g)


def _nt_dot(a, b):
    return lax.dot_general(a, b, (((1,), (1,)), ((), ())), preferred_element_type=F32)


def _proj_kernel(src_rows_ref, u_ref, w_ref, wlr_ref, o_ref, olr_ref):
    del src_rows_ref
    w = w_ref[...].astype(BF16)
    for m in range(TM_PROJ // TM_PROJ_SUB):
        rows = slice(m * TM_PROJ_SUB, (m + 1) * TM_PROJ_SUB)
        o_ref[rows, :] = _nt_dot(u_ref[rows, :], w).astype(BF16)

    @pl.when(pl.program_id(1) == 0)
    def _():
        wlr = wlr_ref[...].astype(BF16)
        for m in range(TM_PROJ // TM_PROJ_SUB):
            rows = slice(m * TM_PROJ_SUB, (m + 1) * TM_PROJ_SUB)
            olr_ref[rows, :] = _nt_dot(u_ref[rows, :], wlr).astype(BF16)


def _project(u, w_in_t, layer):
    t = u.shape[0]
    lr_row = SRC_COLS["lr"][0]
    wspec = lambda rows, index_map: pl.BlockSpec((None, pl.Element(rows), pl.Element(D_MODEL)), index_map)
    grid_spec = pltpu.PrefetchScalarGridSpec(
        num_scalar_prefetch=1,
        grid=(t // TM_PROJ, PACKED_COLS // TN_PROJ),
        in_specs=[pl.BlockSpec((TM_PROJ, D_MODEL), lambda i, j, src: (i, 0)),
                  wspec(TN_PROJ, lambda i, j, src: (layer, src[j] * GLA_RANK, 0)),
                  wspec(LR_PAD, lambda i, j, src: (layer, lr_row, 0))],
        out_specs=[pl.BlockSpec((TM_PROJ, TN_PROJ), lambda i, j, src: (i, j)),
                   pl.BlockSpec((TM_PROJ, LR_PAD), lambda i, j, src: (i, 0))])
    return pl.pallas_call(
        _proj_kernel,
        grid_spec=grid_spec,
        out_shape=[jax.ShapeDtypeStruct((t, PACKED_COLS), BF16),
                   jax.ShapeDtypeStruct((t, LR_PAD), BF16)],
        compiler_params=pltpu.CompilerParams(
            dimension_semantics=("arbitrary", "arbitrary"), vmem_limit_bytes=VMEM_LIMIT),
        name="proj",
    )(jnp.asarray([r // GLA_RANK for r in PACKED_SRC_ROWS], jnp.int32), u, w_in_t, w_in_t)


def _silu(x):
    return x / (1.0 + jnp.exp(-x))


def _log_sigmoid(x):
    return jnp.minimum(x, 0.0) - jnp.log(1.0 + jnp.exp(-jnp.abs(x)))


def _alibi_slope(h):
    return math.pow(2.0, -8.0 * (h + 1) / N_Q_HEADS)


GROUP_ROWS = GQA_GROUP * CHUNK


def _fill_alibi(bias_ref):
    s = lax.broadcasted_iota(jnp.int32, (KEYS, CHUNK), 0)
    t = lax.broadcasted_iota(jnp.int32, (KEYS, CHUNK), 1)
    dist = jnp.abs(WINDOW + t - s).astype(F32)
    for kh in range(N_KV_HEADS):
        for g in range(GQA_GROUP):
            bias_ref[kh, :, g * CHUNK:(g + 1) * CHUNK] = dist * (-_alibi_slope(kh * GQA_GROUP + g))


def _attention_chunk(r, qa_ref, za_ref, kwin, vwin, sinks_ref, layer, bias_ref, key_lo, xa_ref):
    rows = pl.ds(r, CHUNK)
    lane_head = lax.broadcasted_iota(jnp.int32, (1, GROUP_ROWS), 1) // CHUNK
    if key_lo is not None:
        key_ok = lax.broadcasted_iota(jnp.int32, (KEYS, GROUP_ROWS), 0) >= key_lo
    for kh in range(N_KV_HEADS):
        k = kwin[:, kh * HEAD_DIM:(kh + 1) * HEAD_DIM]
        v = vwin[:, kh * HEAD_DIM:(kh + 1) * HEAD_DIM]
        q = jnp.concatenate(
            [qa_ref[rows, pl.ds((kh * GQA_GROUP + g) * HEAD_DIM, HEAD_DIM)] for g in range(GQA_GROUP)],
            axis=0)
        q = q * jnp.asarray(HEAD_DIM ** -0.5, BF16)
        s = _nt_dot(k, q) + bias_ref[kh]
        if key_lo is not None:
            s = jnp.where(key_ok, s, -jnp.inf)
        sink = jnp.zeros((1, GROUP_ROWS), F32)
        for g in range(GQA_GROUP):
            sink = jnp.where(lane_head == g, sinks_ref[layer, kh * GQA_GROUP + g], sink)
        m = jnp.maximum(jnp.max(s, axis=0, keepdims=True), sink)
        p = jnp.exp(s - m)
        den = jnp.sum(p, axis=0, keepdims=True) + jnp.exp(sink - m)
        o_t = lax.dot_general(v, p.astype(BF16), (((0,), (0,)), ((), ())), preferred_element_type=F32)
        o = jnp.transpose(o_t / den)
        for g in range(GQA_GROUP):
            h = kh * GQA_GROUP + g
            cols = pl.ds(h * HEAD_DIM, HEAD_DIM)
            z = za_ref[rows, cols].astype(F32)
            xa_ref[rows, cols] = (o[g * CHUNK:(g + 1) * CHUNK] * _silu(z)).astype(BF16)


def _gla_prepare(lr_ref, kg_ref, wfu_ref, bf_ref, k_scr, b_scr):
    n = lr_ref.shape[0]
    la2 = _log2_forget(lr_ref[...], wfu_ref, bf_ref)
    row = lax.broadcasted_iota(jnp.int32, (n, n), 0)
    col = lax.broadcasted_iota(jnp.int32, (n, n), 1)
    tri = jnp.where(row // CHUNK == col // CHUNK, row - col, -1) >= 0
    tri = jnp.where(tri, 1.0, 0.0).astype(BF16)
    la_hi = la2.astype(BF16)
    rem = la2 - la_hi.astype(F32)
    la_mid = rem.astype(BF16)
    la_lo = (rem - la_mid.astype(F32)).astype(BF16)
    b_scr[...] = (jnp.dot(tri, la_hi, preferred_element_type=F32)
                  + jnp.dot(tri, la_mid, preferred_element_type=F32)
                  + jnp.dot(tri, la_lo, preferred_element_type=F32))
    k_scr[...] = kg_ref[...].astype(F32)


def _gla_states(chunk_states, vg_ref, k_scr, b_scr, s0_scr):
    for hh in range(GLA_HEADS):
        kcols = pl.ds(hh * GLA_HK, GLA_HK)
        vcols = pl.ds(hh * GLA_HV, GLA_HV)
        steps = []
        for c in range(len(chunk_states)):
            rows = pl.ds(c * CHUNK, CHUNK)
            b_end = b_scr[c * CHUNK + CHUNK - 1:(c + 1) * CHUNK, kcols]
            k_til = (k_scr[rows, kcols] * jnp.exp2(b_end - b_scr[rows, kcols])).astype(BF16)
            upd = lax.dot_general(k_til, vg_ref[rows, vcols], (((0,), (0,)), ((), ())),
                                  preferred_element_type=F32)
            e_col = jnp.transpose(jnp.broadcast_to(jnp.exp2(b_end), (GLA_HK, GLA_HK)))
            steps.append((jnp.concatenate([e_col, e_col], axis=1), upd))
        s = None
        for c, ((in_ref, out_ref), (decay, upd)) in enumerate(zip(chunk_states, steps)):
            if c == 0 or in_ref is not chunk_states[c - 1][1]:
                s = in_ref[hh]
            s0_scr[c, hh] = s.astype(BF16)
            s = s * decay + upd
            if c + 1 == len(chunk_states) or chunk_states[c + 1][0] is not out_ref:
                out_ref[hh] = s


def _gla_chunk(c, qg_ref, vg_ref, zg_ref, s0_scr, gnorm, k_scr, b_scr, xg_ref):
    r = c * CHUNK
    rows = pl.ds(r, CHUNK)
    lane = lax.broadcasted_iota(jnp.int32, (SUB, CHUNK), 1)
    row0 = lax.broadcasted_iota(jnp.int32, (SUB, CHUNK), 0)

    for hh in range(GLA_HEADS):
        kcols = pl.ds(hh * GLA_HK, GLA_HK)
        vcols = pl.ds(hh * GLA_HV, GLA_HV)
        q = qg_ref[rows, kcols].astype(F32) * (GLA_HK ** -0.5)
        k = k_scr[rows, kcols]
        v = vg_ref[rows, vcols]
        b = b_scr[rows, kcols]
        blk = lambda x, j: x[j * SUB:(j + 1) * SUB]
        b_row = lambda t: b_scr[r + t:r + t + 1, kcols]
        b_last = [b_row(j * SUB + SUB - 1) for j in range(NSUB)]

        k_hat = jnp.concatenate([blk(k, j) * jnp.exp2(b_last[j] - blk(b, j)) for j in range(NSUB)], axis=0)
        lhs = jnp.concatenate([q[(j + 1) * SUB:] * jnp.exp2(b[(j + 1) * SUB:] - b_last[j])
                               for j in range(NSUB - 1)], axis=0)
        rr = _nt_dot(lhs.astype(BF16), k_hat.astype(BF16))

        score_rows = []
        for i in range(NSUB):
            acc = jnp.zeros((SUB, CHUNK), F32)
            for j in range(i):
                base = sum((NSUB - 1 - jj) * SUB for jj in range(j)) + (i - j - 1) * SUB
                acc = jnp.where(lane // SUB == j, rr[base:base + SUB], acc)
            qi, bi = blk(q, i), blk(b, i)
            for s in range(i * SUB, (i + 1) * SUB):
                prod = (qi * k_scr[r + s:r + s + 1, kcols]) * jnp.exp2(bi - b_row(s))
                acc = jnp.where(lane == s, jnp.sum(prod, axis=1, keepdims=True), acc)
            score_rows.append(jnp.where(lane <= row0 + i * SUB, acc, 0.0))
        scores = jnp.concatenate(score_rows, axis=0)

        o = (jnp.dot(scores.astype(BF16), v, preferred_element_type=F32)
             + jnp.dot((q * jnp.exp2(b)).astype(BF16), s0_scr[c, hh], preferred_element_type=F32))

        y = _rms(o, gnorm)
        z = zg_ref[rows, vcols].astype(F32)
        xg_ref[rows, vcols] = (y * _silu(z)).astype(BF16)


def _log2_forget(lr, wfu_ref, bf_ref):
    x = jnp.dot(lr, wfu_ref[...], preferred_element_type=F32) + bf_ref[...]
    return _log_sigmoid(x) * (LOG2E / GLA_TAU)


def _mix_prompt_kernel(layer, n_steps, chunks, *refs):
    n_carried = 3 if layer > 0 else 0
    (qa_ref, za_ref, vg_ref, zg_ref, qg_ref, kg_ref, ka_ref, va_ref, lr_ref,
     wfu_ref, bf_ref, gn_ref, sinks_ref) = refs[:13]
    (xa_ref, xg_ref, st_ref, knew_ref, vnew_ref,
     halo_k, halo_v, state, bias_scr, k_scr, b_scr, s0_scr) = refs[13 + n_carried:]
    b = pl.program_id(0)
    t = pl.program_id(1)
    assert chunks * CHUNK >= WINDOW
    cur = t % 2
    nxt = 1 - cur

    @pl.when((b == 0) & (t == 0))
    def _():
        _fill_alibi(bias_scr)

    @pl.when(t == 0)
    def _():
        halo_k[0] = jnp.zeros((WINDOW, KV_WIDTH), BF16)
        halo_v[0] = jnp.zeros((WINDOW, KV_WIDTH), BF16)
        state[...] = jnp.zeros(state.shape, F32)

    def window(halo, tile_ref, c):
        lo = c * CHUNK - WINDOW
        if lo >= 0:
            return tile_ref[lo:lo + KEYS, :]
        return jnp.concatenate([halo[cur, WINDOW + lo:, :], tile_ref[0:(c + 1) * CHUNK, :]], axis=0)

    _gla_prepare(lr_ref, kg_ref, wfu_ref, bf_ref, k_scr, b_scr)
    _gla_states([(state, state)] * chunks, vg_ref, k_scr, b_scr, s0_scr)
    gnorm = gn_ref[...]

    for c in range(chunks):
        r = c * CHUNK
        kwin = window(halo_k, ka_ref, c)
        vwin = window(halo_v, va_ref, c)
        key_lo = (WINDOW // CHUNK - (t * chunks + c)) * CHUNK if c < WINDOW // CHUNK else None
        _attention_chunk(r, qa_ref, za_ref, kwin, vwin, sinks_ref, layer, bias_scr, key_lo, xa_ref)
        _gla_chunk(c, qg_ref, vg_ref, zg_ref, s0_scr, gnorm, k_scr, b_scr, xg_ref)

    halo_k[nxt] = ka_ref[chunks * CHUNK - WINDOW:, :]
    halo_v[nxt] = va_ref[chunks * CHUNK - WINDOW:, :]

    @pl.when(t == n_steps - 1)
    def _():
        st_ref[0] = state[...]
        knew_ref[0] = ka_ref[chunks * CHUNK - WINDOW:, :].astype(F32)
        vnew_ref[0] = va_ref[chunks * CHUNK - WINDOW:, :].astype(F32)


def _mix_sample_kernel(layer, n_seqs, *refs):
    n_carried = 3 if layer > 0 else 0
    (qa_ref, za_ref, vg_ref, zg_ref, qg_ref, kg_ref, ka_ref, va_ref, lr_ref,
     wfu_ref, bf_ref, gn_ref, sinks_ref, ck_ref, cv_ref, s0_ref) = refs[:16]
    xa_ref, xg_ref, st_ref, knew_ref, vnew_ref, bias_scr, k_scr, b_scr, s0_scr = refs[16 + n_carried:]

    @pl.when(pl.program_id(0) == 0)
    def _():
        _fill_alibi(bias_scr)

    _gla_prepare(lr_ref, kg_ref, wfu_ref, bf_ref, k_scr, b_scr)
    _gla_states([(s0_ref.at[g], st_ref.at[g]) for g in range(n_seqs)], vg_ref, k_scr, b_scr, s0_scr)
    gnorm = gn_ref[...]
    for g in range(n_seqs):
        r = g * CHUNK
        k_new = ka_ref[r:r + CHUNK, :]
        v_new = va_ref[r:r + CHUNK, :]
        kwin = jnp.concatenate([ck_ref[g].astype(BF16), k_new], axis=0)
        vwin = jnp.concatenate([cv_ref[g].astype(BF16), v_new], axis=0)
        knew_ref[g, 0:WINDOW - CHUNK, :] = ck_ref[g, CHUNK:, :]
        knew_ref[g, WINDOW - CHUNK:, :] = k_new.astype(F32)
        vnew_ref[g, 0:WINDOW - CHUNK, :] = cv_ref[g, CHUNK:, :]
        vnew_ref[g, WINDOW - CHUNK:, :] = v_new.astype(F32)
        _attention_chunk(r, qa_ref, za_ref, kwin, vwin, sinks_ref, layer, bias_scr, None, xa_ref)
        _gla_chunk(g, qg_ref, vg_ref, zg_ref, s0_scr, gnorm, k_scr, b_scr, xg_ref)


def _section_specs(rows, row_index):
    def spec(width, col):
        return pl.BlockSpec((rows, width), functools.partial(
            lambda cb, *g: (row_index(*g), cb), col // width))
    return [spec(ATTN_WIDTH, COL_QA), spec(ATTN_WIDTH, COL_ZA), spec(GLA_DV, COL_VG), spec(GLA_DV, COL_ZG),
            spec(GLA_DK, COL_QG), spec(GLA_DK, COL_KG), spec(KV_WIDTH, COL_KA), spec(KV_WIDTH, COL_VA),
            spec(LR_PAD, 0)]


def _layer_weight_specs(layer, ngrid):
    zeros = (0,) * 2
    return [pl.BlockSpec((None, LR_PAD, GLA_DK), lambda *g: (layer,) + zeros),
            pl.BlockSpec((None, 1, GLA_DK), lambda *g: (layer,) + zeros),
            pl.BlockSpec((None, 1, GLA_HV), lambda *g: (layer,) + zeros),
            pl.BlockSpec(memory_space=pltpu.SMEM)]


def _stacked_outputs(layer, batch, seqs_per_step, n_inputs, carried):
    lead = lambda *g: (layer, g[0])
    nb = seqs_per_step
    specs = [pl.BlockSpec((None, nb, GLA_HEADS, GLA_HK, GLA_HV), lambda *g: lead(*g) + (0, 0, 0)),
             pl.BlockSpec((None, nb, WINDOW, KV_WIDTH), lambda *g: lead(*g) + (0, 0)),
             pl.BlockSpec((None, nb, WINDOW, KV_WIDTH), lambda *g: lead(*g) + (0, 0))]
    shapes = [jax.ShapeDtypeStruct((DEPTH, batch, GLA_HEADS, GLA_HK, GLA_HV), F32),
              jax.ShapeDtypeStruct((DEPTH, batch, WINDOW, KV_WIDTH), F32),
              jax.ShapeDtypeStruct((DEPTH, batch, WINDOW, KV_WIDTH), F32)]
    if carried is None:
        return specs, shapes, [], {}
    alias_specs = [pl.BlockSpec(memory_space=pl.ANY)] * 3
    aliases = {n_inputs + k: 2 + k for k in range(3)}
    return specs, shapes, alias_specs, aliases


def _mix_prompt(p, p_lr, wfu, bfg, gn, sinks, layer, batch, seq, carried):
    chunks = PROMPT_CHUNKS_PER_STEP
    rows = chunks * CHUNK
    n_steps = seq // rows
    row_index = lambda b, t: b * n_steps + t
    in_specs = _section_specs(rows, row_index) + _layer_weight_specs(layer, 2)
    st_specs, st_shapes, alias_specs, aliases = _stacked_outputs(layer, batch, 1, len(in_specs), carried)
    outs = pl.pallas_call(
        functools.partial(_mix_prompt_kernel, layer, n_steps, chunks),
        grid=(batch, n_steps),
        in_specs=in_specs + alias_specs,
        out_specs=[pl.BlockSpec((rows, ATTN_WIDTH), lambda b, t: (b * n_steps + t, 0)),
                   pl.BlockSpec((rows, GLA_DV), lambda b, t: (b * n_steps + t, 0))] + st_specs,
        out_shape=[jax.ShapeDtypeStruct((batch * seq, ATTN_WIDTH), BF16),
                   jax.ShapeDtypeStruct((batch * seq, GLA_DV), BF16)] + st_shapes,
        input_output_aliases=aliases,
        scratch_shapes=[pltpu.VMEM((2, WINDOW, KV_WIDTH), BF16),
                        pltpu.VMEM((2, WINDOW, KV_WIDTH), BF16),
                        pltpu.VMEM((GLA_HEADS, GLA_HK, GLA_HV), F32),
                        pltpu.VMEM((N_KV_HEADS, KEYS, GROUP_ROWS), F32),
                        pltpu.VMEM((rows, GLA_DK), F32),
                        pltpu.VMEM((rows, GLA_DK), F32),
                        pltpu.VMEM((rows // CHUNK, GLA_HEADS, GLA_HK, GLA_HV), BF16)],
        compiler_params=pltpu.CompilerParams(
            dimension_semantics=("arbitrary", "arbitrary"), vmem_limit_bytes=VMEM_LIMIT),
        name="mix_prompt",
    )(*([p] * 8), p_lr, wfu, bfg, gn, sinks, *(carried or ()))
    return outs[0], outs[1], tuple(outs[2:])


def _mix_sample(p, p_lr, wfu, bfg, gn, sinks, cache_k, cache_v, state, layer, row0, batch, carried):
    nb = SAMPLE_SEQS_PER_STEP
    rows = nb * CHUNK
    assert row0 % rows == 0 and batch % nb == 0
    row_block0 = row0 // rows
    row_index = lambda b: row_block0 + b
    cache_spec = pl.BlockSpec((None, nb, WINDOW, KV_WIDTH), lambda b: (layer, b, 0, 0))
    state_spec = pl.BlockSpec((None, nb, GLA_HEADS, GLA_HK, GLA_HV), lambda b: (layer, b, 0, 0, 0))
    in_specs = (_section_specs(rows, row_index) + _layer_weight_specs(layer, 1)
                + [cache_spec, cache_spec, state_spec])
    st_specs, st_shapes, alias_specs, aliases = _stacked_outputs(layer, batch, nb, len(in_specs), carried)
    outs = pl.pallas_call(
        functools.partial(_mix_sample_kernel, layer, nb),
        grid=(batch // nb,),
        in_specs=in_specs + alias_specs,
        out_specs=[pl.BlockSpec((rows, ATTN_WIDTH), lambda b: (b, 0)),
                   pl.BlockSpec((rows, GLA_DV), lambda b: (b, 0))] + st_specs,
        out_shape=[jax.ShapeDtypeStruct((batch * CHUNK, ATTN_WIDTH), BF16),
                   jax.ShapeDtypeStruct((batch * CHUNK, GLA_DV), BF16)] + st_shapes,
        input_output_aliases=aliases,
        scratch_shapes=[pltpu.VMEM((N_KV_HEADS, KEYS, GROUP_ROWS), F32),
                        pltpu.VMEM((rows, GLA_DK), F32),
                        pltpu.VMEM((rows, GLA_DK), F32),
                        pltpu.VMEM((rows // CHUNK, GLA_HEADS, GLA_HK, GLA_HV), BF16)],
        compiler_params=pltpu.CompilerParams(
            dimension_semantics=("arbitrary",), vmem_limit_bytes=VMEM_LIMIT),
        name="mix_sample",
    )(*([p] * 8), p_lr, wfu, bfg, gn, sinks, cache_k, cache_v, state, *(carried or ()))
    return outs[0], outs[1], tuple(outs[2:])


def _sigmoid(x):
    return 1.0 / (1.0 + jnp.exp(-x))


def _out_kernel(n_prompt_tiles, first, last, *refs):
    xa_p_ref, xa_s_ref, xg_p_ref, xg_s_ref, ga_ref, gg_ref = refs[:6]
    n_h = 2 if first else 1
    h_refs = refs[6:6 + n_h]
    wa_ref, wg_ref, wo_ref, bg_ref, npost_ref, npre_ref = refs[6 + n_h:12 + n_h]
    out_refs = refs[12 + n_h:]

    is_prompt = pl.program_id(0) < n_prompt_tiles
    xa = jnp.where(is_prompt, xa_p_ref[...], xa_s_ref[...])
    xg = jnp.where(is_prompt, xg_p_ref[...], xg_s_ref[...])
    ya = jnp.dot(xa, wa_ref[...], preferred_element_type=F32)
    yg = jnp.dot(xg, wg_ref[...], preferred_element_type=F32)
    bg = bg_ref[...]
    gate_a = _sigmoid(ga_ref[...].astype(F32) + bg[:, :D_MODEL])
    gate_g = _sigmoid(gg_ref[...].astype(F32) + bg[:, D_MODEL:])
    merged = (gate_a * ya + gate_g * yg).astype(BF16)
    z = jnp.dot(merged, wo_ref[...], preferred_element_type=F32)
    if first:
        h_in = jnp.where(is_prompt, h_refs[0][...], h_refs[1][...])
    else:
        h_in = h_refs[0][...]
    h = h_in + _rms(z, npost_ref[...])
    if last:
        y_p_ref, y_s_ref = out_refs

        @pl.when(is_prompt)
        def _():
            y_p_ref[...] = h

        @pl.when(jnp.logical_not(is_prompt))
        def _():
            y_s_ref[...] = h
    else:
        h_out_ref, u_out_ref = out_refs
        h_out_ref[...] = h
        u_out_ref[...] = _rms(h, npre_ref[...]).astype(BF16)


def _output(xa_p, xa_s, xg_p, xg_s, p, h, wa, wg, wo, bg, npost, npre, layer):
    first = layer == 0
    last = layer == DEPTH - 1
    n_prompt_tiles = xa_p.shape[0] // TM_OUT
    n_sample_tiles = xa_s.shape[0] // TM_OUT
    n_tiles = n_prompt_tiles + n_sample_tiles
    t = n_tiles * TM_OUT
    p_idx, s_idx = _two_group_index_maps(n_prompt_tiles)
    row = lambda i: (i, 0)
    wspec = lambda k: pl.BlockSpec((None, k, D_MODEL), lambda i: (layer, 0, 0), pipeline_mode=pl.Buffered(1))
    vspec = lambda n, l: pl.BlockSpec((None, 1, n), lambda i: (l, 0, 0))
    tile = lambda idx: pl.BlockSpec((TM_OUT, D_MODEL), idx)
    h_args = tuple(h) if first else (h,)
    h_specs = [tile(p_idx), tile(s_idx)] if first else [tile(row)]
    if last:
        out_specs = [tile(p_idx), tile(s_idx)]
        out_shape = [jax.ShapeDtypeStruct((n_prompt_tiles * TM_OUT, D_MODEL), F32),
                     jax.ShapeDtypeStruct((n_sample_tiles * TM_OUT, D_MODEL), F32)]
    else:
        out_specs = [tile(row), tile(row)]
        out_shape = [jax.ShapeDtypeStruct((t, D_MODEL), F32), jax.ShapeDtypeStruct((t, D_MODEL), BF16)]
    return pl.pallas_call(
        functools.partial(_out_kernel, n_prompt_tiles, first, last),
        grid=(n_tiles,),
        in_specs=[pl.BlockSpec((TM_OUT, ATTN_WIDTH), p_idx), pl.BlockSpec((TM_OUT, ATTN_WIDTH), s_idx),
                  pl.BlockSpec((TM_OUT, GLA_DV), p_idx), pl.BlockSpec((TM_OUT, GLA_DV), s_idx),
                  pl.BlockSpec((TM_OUT, D_MODEL), lambda i: (i, COL_GA // D_MODEL)),
                  pl.BlockSpec((TM_OUT, D_MODEL), lambda i: (i, COL_GG // D_MODEL))]
                 + h_specs
                 + [wspec(ATTN_WIDTH), wspec(GLA_DV), wspec(D_MODEL),
                    vspec(2 * D_MODEL, layer), vspec(D_MODEL, layer), vspec(D_MODEL, min(layer + 1, DEPTH - 1))],
        out_specs=out_specs,
        out_shape=out_shape,
        compiler_params=pltpu.CompilerParams(
            dimension_semantics=("arbitrary",), vmem_limit_bytes=VMEM_LIMIT),
        name="out",
    )(xa_p, xa_s, xg_p, xg_s, p, p, *h_args, wa, wg, wo, bg, npost, npre)


def kernel(x_prompt, x_sample, cache_k, cache_v, state_gla, norm_pre, norm_post, w_in, b_gate,
           attn_sinks, w_forget_up, b_forget, gla_norm, w_branch_attn, w_branch_gla, w_out):
    batch, seq, _ = x_prompt.shape
    dec_batch, dec_seq, _ = x_sample.shape
    assert dec_seq == CHUNK and seq % (PROMPT_CHUNKS_PER_STEP * CHUNK) == 0
    assert cache_k.shape[2] == WINDOW
    n_prompt = batch * seq
    n_sample = dec_batch * dec_seq

    w_in_t = jnp.swapaxes(w_in, 1, 2)
    wa =w_branch_attn.astype(BF16)
    wg = w_branch_gla.astype(BF16)
    wo = w_out.astype(BF16)
    wfu = jnp.pad(w_forget_up, ((0, 0), (0, LR_PAD - GLA_RANK), (0, 0))).astype(BF16)
    bfg = b_forget.reshape(DEPTH, 1, GLA_DK)
    gn = gla_norm.reshape(DEPTH, 1, GLA_HV)
    bg = b_gate.reshape(DEPTH, 1, 2 * D_MODEL)
    npre = norm_pre.reshape(DEPTH, 1, D_MODEL)
    npost = norm_post.reshape(DEPTH, 1, D_MODEL)
    ck = cache_k.reshape(DEPTH, dec_batch, WINDOW, KV_WIDTH)
    cv = cache_v.reshape(DEPTH, dec_batch, WINDOW, KV_WIDTH)

    xp = x_prompt.reshape(n_prompt, D_MODEL)
    xs = x_sample.reshape(n_sample, D_MODEL)
    u = _prenorm(xp, xs, npre[0])
    h = (xp, xs)

    stacked_p = stacked_s = None
    for l in range(DEPTH):
        p, p_lr = _project(u, w_in_t, l)
        xa_p, xg_p, stacked_p = _mix_prompt(p, p_lr, wfu, bfg, gn, attn_sinks, l, batch, seq, stacked_p)
        xa_s, xg_s, stacked_s = _mix_sample(p, p_lr, wfu, bfg, gn, attn_sinks, ck, cv, state_gla, l, n_prompt,
                                            dec_batch, stacked_s)
        outs = _output(xa_p, xa_s, xg_p, xg_s, p, h, wa, wg, wo, bg, npost, npre, l)
        if l < DEPTH - 1:
            h, u = outs

    y_prompt = outs[0].reshape(batch, seq, D_MODEL)
    y_sample = outs[1].reshape(dec_batch, dec_seq, D_MODEL)
    heads = lambda a: a.reshape(a.shape[:3] + (N_KV_HEADS, HEAD_DIM))
    return (y_prompt, y_sample, heads(stacked_p[1]), heads(stacked_p[2]), stacked_p[0],
            heads(stacked_s[1]), heads(stacked_s[2]), stacked_s[0])
```

```python
import functools
import math

import jax
import jax.numpy as jnp
from jax import lax
from jax.experimental import pallas as pl
from jax.experimental.pallas import tpu as pltpu

F32 = jnp.float32
BF16 = jnp.bfloat16

D_MODEL = 2048
DEPTH = 4
CHUNK = 64
WINDOW = 128
HEAD_DIM = 64
N_Q_HEADS = 16
N_KV_HEADS = 4
GQA_GROUP = N_Q_HEADS // N_KV_HEADS
ATTN_WIDTH = N_Q_HEADS * HEAD_DIM
KV_WIDTH = N_KV_HEADS * HEAD_DIM
GLA_HEADS = 4
GLA_DK = 512
GLA_DV = 1024
GLA_HK = GLA_DK // GLA_HEADS
GLA_HV = GLA_DV // GLA_HEADS
GLA_RANK = 16
GLA_TAU = 16.0
NORM_EPS = 1e-6
LOG2E = math.log2(math.e)
IN_SIZES = (ATTN_WIDTH, KV_WIDTH, KV_WIDTH, ATTN_WIDTH,
            GLA_DK, GLA_DK, GLA_DV, GLA_DV, GLA_RANK, D_MODEL, D_MODEL)

KEYS = WINDOW + CHUNK
SUB = 8
NSUB = CHUNK // SUB

TM_PROJ = 3072
TM_PROJ_SUB = 512
TN_PROJ = 512
TM_OUT = 256
OUT_COLS = 256
TM_NORM = 512
PROMPT_CHUNKS_PER_STEP = 4
SAMPLE_SEQS_PER_STEP = 4

VMEM_LIMIT = 56 * 1024 * 1024

SRC_COLS = {}
_acc = 0
for _name, _size in zip(("qa", "ka", "va", "za", "qg", "kg", "vg", "zg", "lr", "ga", "gg"), IN_SIZES):
    SRC_COLS[_name] = (_acc, _size)
    _acc += _size
IN_COLS = _acc

PACKED_ORDER = ("ga", "gg", "qa", "za", "vg", "zg", "qg", "kg", "ka", "va")
PACKED_COL = {}
PACKED_SRC_ROWS = []
_acc = 0
for _name in PACKED_ORDER:
    _start, _size = SRC_COLS[_name]
    PACKED_COL[_name] = _acc
    if _acc % TN_PROJ == 0:
        PACKED_SRC_ROWS += [_start + k for k in range(0, max(_size, TN_PROJ), TN_PROJ)]
    _acc += _size
PACKED_COLS = _acc
COL_GA, COL_GG, COL_QA, COL_ZA, COL_VG, COL_ZG, COL_QG, COL_KG, COL_KA, COL_VA = (
    PACKED_COL[_n] for _n in PACKED_ORDER)
LR_PAD = 128
assert SRC_COLS["va"][0] == SRC_COLS["ka"][0] + KV_WIDTH and 2 * KV_WIDTH == TN_PROJ
assert len(PACKED_SRC_ROWS) * TN_PROJ == PACKED_COLS
assert all(_r % GLA_RANK == 0 for _r in PACKED_SRC_ROWS)


def _rms(x, g):
    return x * lax.rsqrt(jnp.mean(x * x, axis=-1, keepdims=True) + NORM_EPS) * g


def _two_group_index_maps(n_prompt_tiles):
    p_idx = lambda i: (jnp.minimum(i, n_prompt_tiles - 1), 0)
    s_idx = lambda i: (jnp.maximum(i - n_prompt_tiles, 0), 0)
    return p_idx, s_idx


def _prenorm_kernel(n_prompt_tiles, xp_ref, xs_ref, g_ref, u_ref):
    x = jnp.where(pl.program_id(0) < n_prompt_tiles, xp_ref[...], xs_ref[...])
    u_ref[...] = _rms(x, g_ref[...]).astype(BF16)


def _prenorm(x_prompt, x_sample, g):
    n_prompt_tiles = x_prompt.shape[0] // TM_NORM
    n_tiles = n_prompt_tiles + x_sample.shape[0] // TM_NORM
    p_idx, s_idx = _two_group_index_maps(n_prompt_tiles)
    return pl.pallas_call(
        functools.partial(_prenorm_kernel, n_prompt_tiles),
        grid=(n_tiles,),
        in_specs=[pl.BlockSpec((TM_NORM, D_MODEL), p_idx),
                  pl.BlockSpec((TM_NORM, D_MODEL), s_idx),
                  pl.BlockSpec((1, D_MODEL), lambda i: (0, 0))],
        out_specs=pl.BlockSpec((TM_NORM, D_MODEL), lambda i: (i, 0)),
        out_shape=jax.ShapeDtypeStruct((n_tiles * TM_NORM, D_MODEL), BF16),
        compiler_params=pltpu.CompilerParams(dimension_semantics=("arbitrary",)),
        name="prenorm",
    )(x_prompt, x_sample, g)


def _nt_dot(a, b):
    return lax.dot_general(a, b, (((1,), (1,)), ((), ())), preferred_element_type=F32)


def _proj_kernel(src_rows_ref, u_ref, w_ref, wlr_ref, o_ref, olr_ref):
    del src_rows_ref
    w = w_ref[...].astype(BF16)
    for m in range(TM_PROJ // TM_PROJ_SUB):
        rows = slice(m * TM_PROJ_SUB, (m + 1) * TM_PROJ_SUB)
        o_ref[rows, :] = _nt_dot(u_ref[rows, :], w).astype(BF16)

    @pl.when(pl.program_id(1) == 0)
    def _():
        wlr = wlr_ref[...].astype(BF16)
        for m in range(TM_PROJ // TM_PROJ_SUB):
            rows = slice(m * TM_PROJ_SUB, (m + 1) * TM_PROJ_SUB)
            olr_ref[rows, :] = _nt_dot(u_ref[rows, :], wlr).astype(BF16)


def _project(u, w_in_t, layer):
    t = u.shape[0]
    lr_row = SRC_COLS["lr"][0]
    wspec = lambda rows, index_map: pl.BlockSpec((None, pl.Element(rows), pl.Element(D_MODEL)), index_map)
    grid_spec = pltpu.PrefetchScalarGridSpec(
        num_scalar_prefetch=1,
        grid=(t // TM_PROJ, PACKED_COLS // TN_PROJ),
        in_specs=[pl.BlockSpec((TM_PROJ, D_MODEL), lambda i, j, src: (i, 0)),
                  wspec(TN_PROJ, lambda i, j, src: (layer, src[j] * GLA_RANK, 0)),
                  wspec(LR_PAD, lambda i, j, src: (layer, lr_row, 0))],
        out_specs=[pl.BlockSpec((TM_PROJ, TN_PROJ), lambda i, j, src: (i, j)),
                   pl.BlockSpec((TM_PROJ, LR_PAD), lambda i, j, src: (i, 0))])
    return pl.pallas_call(
        _proj_kernel,
        grid_spec=grid_spec,
        out_shape=[jax.ShapeDtypeStruct((t, PACKED_COLS), BF16),
                   jax.ShapeDtypeStruct((t, LR_PAD), BF16)],
        compiler_params=pltpu.CompilerParams(
            dimension_semantics=("arbitrary", "arbitrary"), vmem_limit_bytes=VMEM_LIMIT),
        name="proj",
    )(jnp.asarray([r // GLA_RANK for r in PACKED_SRC_ROWS], jnp.int32), u, w_in_t, w_in_t)


def _run(stages):
    for _ in stages:
        pass


def _interleave(major, n_major, minor, n_minor):
    done = 0
    for i in range(n_major):
        while done < n_minor and done * n_major < (i + 1) * n_minor:
            next(minor)
            done += 1
        next(major)
    _run(minor)
    _run(major)


def _silu(x):
    return x / (1.0 + jnp.exp(-x))


def _log_sigmoid(x):
    return jnp.minimum(x, 0.0) - jnp.log(1.0 + jnp.exp(-jnp.abs(x)))


def _alibi_slope(h):
    return math.pow(2.0, -8.0 * (h + 1) / N_Q_HEADS)


GROUP_ROWS = GQA_GROUP * CHUNK


def _fill_alibi(bias_ref):
    s = lax.broadcasted_iota(jnp.int32, (KEYS, CHUNK), 0)
    t = lax.broadcasted_iota(jnp.int32, (KEYS, CHUNK), 1)
    dist = jnp.abs(WINDOW + t - s).astype(F32)
    for kh in range(N_KV_HEADS):
        for g in range(GQA_GROUP):
            bias_ref[kh, :, g * CHUNK:(g + 1) * CHUNK] = dist * (-_alibi_slope(kh * GQA_GROUP + g))


def _attention_chunk(r, qa_ref, za_ref, kwin, vwin, sinks_ref, layer, bias_ref, key_lo, xa_ref):
    rows = pl.ds(r, CHUNK)
    lane_head = lax.broadcasted_iota(jnp.int32, (1, GROUP_ROWS), 1) // CHUNK
    if key_lo is not None:
        key_ok = lax.broadcasted_iota(jnp.int32, (KEYS, GROUP_ROWS), 0) >= key_lo
    for kh in range(N_KV_HEADS):
        k = kwin[:, kh * HEAD_DIM:(kh + 1) * HEAD_DIM]
        v = vwin[:, kh * HEAD_DIM:(kh + 1) * HEAD_DIM]
        q = jnp.concatenate(
            [qa_ref[rows, pl.ds((kh * GQA_GROUP + g) * HEAD_DIM, HEAD_DIM)] for g in range(GQA_GROUP)],
            axis=0)
        q = q * jnp.asarray(HEAD_DIM ** -0.5, BF16)
        s = _nt_dot(k, q) + bias_ref[kh]
        if key_lo is not None:
            s = jnp.where(key_ok, s, -jnp.inf)
        sink = jnp.zeros((1, GROUP_ROWS), F32)
        for g in range(GQA_GROUP):
            sink = jnp.where(lane_head == g, sinks_ref[layer, kh * GQA_GROUP + g], sink)
        m = jnp.maximum(jnp.max(s, axis=0, keepdims=True), sink)
        p = jnp.exp(s - m)
        den = jnp.sum(p, axis=0, keepdims=True) + jnp.exp(sink - m)
        o_t = lax.dot_general(v, p.astype(BF16), (((0,), (0,)), ((), ())), preferred_element_type=F32)
        o = jnp.transpose(o_t / den)
        for g in range(GQA_GROUP):
            h = kh * GQA_GROUP + g
            cols = pl.ds(h * HEAD_DIM, HEAD_DIM)
            z = za_ref[rows, cols].astype(F32)
            xa_ref[rows, cols] = (o[g * CHUNK:(g + 1) * CHUNK] * _silu(z)).astype(BF16)
        yield


def _gla_prepare(lr_ref, kg_ref, wfu_ref, bf_ref, k_scr, b_scr):
    n = lr_ref.shape[0]
    la2 = _log2_forget(lr_ref[...], wfu_ref, bf_ref)
    row = lax.broadcasted_iota(jnp.int32, (n, n), 0)
    col = lax.broadcasted_iota(jnp.int32, (n, n), 1)
    tri = jnp.where(row // CHUNK == col // CHUNK, row - col, -1) >= 0
    tri = jnp.where(tri, 1.0, 0.0).astype(BF16)
    la_hi = la2.astype(BF16)
    rem = la2 - la_hi.astype(F32)
    la_mid = rem.astype(BF16)
    la_lo = (rem - la_mid.astype(F32)).astype(BF16)
    b_scr[...] = (jnp.dot(tri, la_hi, preferred_element_type=F32)
                  + jnp.dot(tri, la_mid, preferred_element_type=F32)
                  + jnp.dot(tri, la_lo, preferred_element_type=F32))
    k_scr[...] = kg_ref[...].astype(F32)


def _gla_states(chunk_states, vg_ref, k_scr, b_scr, s0_scr):
    for hh in range(GLA_HEADS):
        kcols = pl.ds(hh * GLA_HK, GLA_HK)
        vcols = pl.ds(hh * GLA_HV, GLA_HV)
        steps = []
        for c in range(len(chunk_states)):
            rows = pl.ds(c * CHUNK, CHUNK)
            b_end = b_scr[c * CHUNK + CHUNK - 1:(c + 1) * CHUNK, kcols]
            k_til = (k_scr[rows, kcols] * jnp.exp2(b_end - b_scr[rows, kcols])).astype(BF16)
            upd = lax.dot_general(k_til, vg_ref[rows, vcols], (((0,), (0,)), ((), ())),
                                  preferred_element_type=F32)
            e_col = jnp.transpose(jnp.broadcast_to(jnp.exp2(b_end), (GLA_HK, GLA_HK)))
            steps.append((jnp.concatenate([e_col, e_col], axis=1), upd))
        s = None
        for c, ((in_ref, out_ref), (decay, upd)) in enumerate(zip(chunk_states, steps)):
            if c == 0 or in_ref is not chunk_states[c - 1][1]:
                s = in_ref[hh]
            s0_scr[c, hh] = s.astype(BF16)
            s = s * decay + upd
            if c + 1 == len(chunk_states) or chunk_states[c + 1][0] is not out_ref:
                out_ref[hh] = s
        yield


def _gla_chunk(c, qg_ref, vg_ref, zg_ref, s0_scr, gnorm, k_scr, b_scr, xg_ref):
    r = c * CHUNK
    rows = pl.ds(r, CHUNK)
    lane = lax.broadcasted_iota(jnp.int32, (SUB, CHUNK), 1)
    row0 = lax.broadcasted_iota(jnp.int32, (SUB, CHUNK), 0)

    for hh in range(GLA_HEADS):
        kcols = pl.ds(hh * GLA_HK, GLA_HK)
        vcols = pl.ds(hh * GLA_HV, GLA_HV)
        q = qg_ref[rows, kcols].astype(F32) * (GLA_HK ** -0.5)
        k = k_scr[rows, kcols]
        v = vg_ref[rows, vcols]
        b = b_scr[rows, kcols]
        blk = lambda x, j: x[j * SUB:(j + 1) * SUB]
        b_row = lambda t: b_scr[r + t:r + t + 1, kcols]
        b_last = [b_row(j * SUB + SUB - 1) for j in range(NSUB)]

        k_hat = jnp.concatenate([blk(k, j) * jnp.exp2(b_last[j] - blk(b, j)) for j in range(NSUB)], axis=0)
        lhs = jnp.concatenate([q[(j + 1) * SUB:] * jnp.exp2(b[(j + 1) * SUB:] - b_last[j])
                               for j in range(NSUB - 1)], axis=0)
        rr = _nt_dot(lhs.astype(BF16), k_hat.astype(BF16))

        score_rows = []
        for i in range(NSUB):
            acc = jnp.zeros((SUB, CHUNK), F32)
            for j in range(i):
                base = sum((NSUB - 1 - jj) * SUB for jj in range(j)) + (i - j - 1) * SUB
                acc = jnp.where(lane // SUB == j, rr[base:base + SUB], acc)
            qi, bi = blk(q, i), blk(b, i)
            for s in range(i * SUB, (i + 1) * SUB):
                prod = (qi * k_scr[r + s:r + s + 1, kcols]) * jnp.exp2(bi - b_row(s))
                acc = jnp.where(lane == s, jnp.sum(prod, axis=1, keepdims=True), acc)
            score_rows.append(jnp.where(lane <= row0 + i * SUB, acc, 0.0))
        scores = jnp.concatenate(score_rows, axis=0)

        o = (jnp.dot(scores.astype(BF16), v, preferred_element_type=F32)
             + jnp.dot((q * jnp.exp2(b)).astype(BF16), s0_scr[c, hh], preferred_element_type=F32))

        y = _rms(o, gnorm)
        z = zg_ref[rows, vcols].astype(F32)
        xg_ref[rows, vcols] = (y * _silu(z)).astype(BF16)
        yield


def _log2_forget(lr, wfu_ref, bf_ref):
    x = jnp.dot(lr, wfu_ref[...], preferred_element_type=F32) + bf_ref[...]
    return _log_sigmoid(x) * (LOG2E / GLA_TAU)


def _sigmoid(x):
    return 1.0 / (1.0 + jnp.exp(-x))


def _out_rows(xa, xg, ga, gg, h_in, wa_ref, wg_ref, wo_ref, bg, npost, npre):
    ya = jnp.dot(xa, wa_ref[...], preferred_element_type=F32)
    yg = jnp.dot(xg, wg_ref[...], preferred_element_type=F32)
    gate_a = _sigmoid(ga.astype(F32) + bg[:, :D_MODEL])
    gate_g = _sigmoid(gg.astype(F32) + bg[:, D_MODEL:])
    merged = (gate_a * ya + gate_g * yg).astype(BF16)
    z = jnp.dot(merged, wo_ref[...], preferred_element_type=F32)
    h = h_in + _rms(z, npost)
    return h, (None if npre is None else _rms(h, npre).astype(BF16))


def _mix_out_prompt_kernel(layer, n_steps, chunks, *refs):
    last = layer == DEPTH - 1
    n_carried = 3 if layer > 0 else 0
    (qa_ref, za_ref, vg_ref, zg_ref, qg_ref, kg_ref, ka_ref, va_ref, lr_ref,
     wfu_ref, bf_ref, gn_ref, sinks_ref,
     ga_ref, gg_ref, h_in_ref, wa_ref, wg_ref, wo_ref, bg_ref, npost_ref, npre_ref) = refs[:22]
    refs = refs[22 + n_carried:]
    out_refs, refs = refs[:1 if last else 2], refs[1 if last else 2:]
    (st_ref, knew_ref, vnew_ref,
     halo_k, halo_v, state, bias_scr, k_scr, b_scr, s0_scr, xa_buf, xg_buf, merged_scr, z_scr) = refs
    s = pl.program_id(0)
    b = s // n_steps
    t = s % n_steps
    assert chunks * CHUNK >= WINDOW
    xa_ref = xa_buf.at[s % 2]
    xg_ref = xg_buf.at[s % 2]

    @pl.when(s == 0)
    def _():
        xa_buf[1] = jnp.zeros(xa_buf.shape[1:], BF16)
        xg_buf[1] = jnp.zeros(xg_buf.shape[1:], BF16)

    xa_prev = xa_buf[1 - s % 2]
    xg_prev = xg_buf[1 - s % 2]
    n_col_blocks = D_MODEL // OUT_COLS

    def out_stages():
        bg = bg_ref[...]
        for n in range(n_col_blocks):
            cols = slice(n * OUT_COLS, (n + 1) * OUT_COLS)
            ya = jnp.dot(xa_prev, wa_ref[:, cols], preferred_element_type=F32)
            yield
            yg = jnp.dot(xg_prev, wg_ref[:, cols], preferred_element_type=F32)
            gate_a = _sigmoid(ga_ref[:, cols].astype(F32) + bg[:, n * OUT_COLS:(n + 1) * OUT_COLS])
            gate_g = _sigmoid(gg_ref[:, cols].astype(F32)
                              + bg[:, D_MODEL + n * OUT_COLS:D_MODEL + (n + 1) * OUT_COLS])
            merged_scr[:, cols] = (gate_a * ya + gate_g * yg).astype(BF16)
            yield
        for n in range(n_col_blocks):
            cols = slice(n * OUT_COLS, (n + 1) * OUT_COLS)
            z_scr[:, cols] = jnp.dot(merged_scr[...], wo_ref[:, cols], preferred_element_type=F32)
            yield
        h = h_in_ref[...] + _rms(z_scr[...], npost_ref[...])
        out_refs[0][...] = h
        if not last:
            out_refs[1][...] = _rms(h, npre_ref[...]).astype(BF16)
        yield

    cur = t % 2
    nxt = 1 - cur

    @pl.when((b == 0) & (t == 0))
    def _():
        _fill_alibi(bias_scr)

    @pl.when(t == 0)
    def _():
        halo_k[0] = jnp.zeros((WINDOW, KV_WIDTH), BF16)
        halo_v[0] = jnp.zeros((WINDOW, KV_WIDTH), BF16)
        state[...] = jnp.zeros(state.shape, F32)

    def window(halo, tile_ref, c):
        lo = c * CHUNK - WINDOW
        if lo >= 0:
            return tile_ref[lo:lo + KEYS, :]
        return jnp.concatenate([halo[cur, WINDOW + lo:, :], tile_ref[0:(c + 1) * CHUNK, :]], axis=0)

    def mix_stages():
        _gla_prepare(lr_ref, kg_ref, wfu_ref, bf_ref, k_scr, b_scr)
        yield
        yield from _gla_states([(state, state)] * chunks, vg_ref, k_scr, b_scr, s0_scr)
        gnorm = gn_ref[...]
        for c in range(chunks):
            r = c * CHUNK
            kwin = window(halo_k, ka_ref, c)
            vwin = window(halo_v, va_ref, c)
            key_lo = (WINDOW // CHUNK - (t * chunks + c)) * CHUNK if c < WINDOW // CHUNK else None
            yield from _attention_chunk(r, qa_ref, za_ref, kwin, vwin, sinks_ref, layer, bias_scr, key_lo, xa_ref)
            yield from _gla_chunk(c, qg_ref, vg_ref, zg_ref, s0_scr, gnorm, k_scr, b_scr, xg_ref)
        halo_k[nxt] = ka_ref[chunks * CHUNK - WINDOW:, :]
        halo_v[nxt] = va_ref[chunks * CHUNK - WINDOW:, :]

    n_mix = 1 + GLA_HEADS + chunks * (N_KV_HEADS + GLA_HEADS)
    n_out = 3 * n_col_blocks + 1
    _interleave(mix_stages(), n_mix, out_stages(), n_out)

    @pl.when(t == n_steps - 1)
    def _():
        st_ref[0] = state[...]
        knew_ref[0] = ka_ref[chunks * CHUNK - WINDOW:, :].astype(F32)
        vnew_ref[0] = va_ref[chunks * CHUNK - WINDOW:, :].astype(F32)


def _mix_sample_kernel(layer, n_seqs, *refs):
    n_carried = 3 if layer > 0 else 0
    (qa_ref, za_ref, vg_ref, zg_ref, qg_ref, kg_ref, ka_ref, va_ref, lr_ref,
     wfu_ref, bf_ref, gn_ref, sinks_ref, ck_ref, cv_ref, s0_ref) = refs[:16]
    xa_ref, xg_ref, st_ref, knew_ref, vnew_ref, bias_scr, k_scr, b_scr, s0_scr = refs[16 + n_carried:]

    @pl.when(pl.program_id(0) == 0)
    def _():
        _fill_alibi(bias_scr)

    _gla_prepare(lr_ref, kg_ref, wfu_ref, bf_ref, k_scr, b_scr)
    _run(_gla_states([(s0_ref.at[g], st_ref.at[g]) for g in range(n_seqs)], vg_ref, k_scr, b_scr, s0_scr))
    gnorm = gn_ref[...]
    for g in range(n_seqs):
        r = g * CHUNK
        k_new = ka_ref[r:r + CHUNK, :]
        v_new = va_ref[r:r + CHUNK, :]
        kwin = jnp.concatenate([ck_ref[g].astype(BF16), k_new], axis=0)
        vwin = jnp.concatenate([cv_ref[g].astype(BF16), v_new], axis=0)
        knew_ref[g, 0:WINDOW - CHUNK, :] = ck_ref[g, CHUNK:, :]
        knew_ref[g, WINDOW - CHUNK:, :] = k_new.astype(F32)
        vnew_ref[g, 0:WINDOW - CHUNK, :] = cv_ref[g, CHUNK:, :]
        vnew_ref[g, WINDOW - CHUNK:, :] = v_new.astype(F32)
        _run(_attention_chunk(r, qa_ref, za_ref, kwin, vwin, sinks_ref, layer, bias_scr, None, xa_ref))
        _run(_gla_chunk(g, qg_ref, vg_ref, zg_ref, s0_scr, gnorm, k_scr, b_scr, xg_ref))


def _section_specs(rows, row_index):
    def spec(width, col):
        return pl.BlockSpec((rows, width), functools.partial(
            lambda cb, *g: (row_index(*g), cb), col // width))
    return [spec(ATTN_WIDTH, COL_QA), spec(ATTN_WIDTH, COL_ZA), spec(GLA_DV, COL_VG), spec(GLA_DV, COL_ZG),
            spec(GLA_DK, COL_QG), spec(GLA_DK, COL_KG), spec(KV_WIDTH, COL_KA), spec(KV_WIDTH, COL_VA),
            spec(LR_PAD, 0)]


def _layer_weight_specs(layer, ngrid):
    zeros = (0,) * 2
    return [pl.BlockSpec((None, LR_PAD, GLA_DK), lambda *g: (layer,) + zeros),
            pl.BlockSpec((None, 1, GLA_DK), lambda *g: (layer,) + zeros),
            pl.BlockSpec((None, 1, GLA_HV), lambda *g: (layer,) + zeros),
            pl.BlockSpec(memory_space=pltpu.SMEM)]


def _stacked_outputs(layer, batch, seqs_per_step, batch_block, n_inputs, first_out, carried):
    lead = lambda *g: (layer, batch_block(*g))
    nb = seqs_per_step
    specs = [pl.BlockSpec((None, nb, GLA_HEADS, GLA_HK, GLA_HV), lambda *g: lead(*g) + (0, 0, 0)),
             pl.BlockSpec((None, nb, WINDOW, KV_WIDTH), lambda *g: lead(*g) + (0, 0)),
             pl.BlockSpec((None, nb, WINDOW, KV_WIDTH), lambda *g: lead(*g) + (0, 0))]
    shapes = [jax.ShapeDtypeStruct((DEPTH, batch, GLA_HEADS, GLA_HK, GLA_HV), F32),
              jax.ShapeDtypeStruct((DEPTH, batch, WINDOW, KV_WIDTH), F32),
              jax.ShapeDtypeStruct((DEPTH, batch, WINDOW, KV_WIDTH), F32)]
    if carried is None:
        return specs, shapes, [], {}
    alias_specs = [pl.BlockSpec(memory_space=pl.ANY)] * 3
    aliases = {n_inputs + k: first_out + k for k in range(3)}
    return specs, shapes, alias_specs, aliases


def _out_weight_specs(layer):
    wspec = lambda k: pl.BlockSpec((None, k, D_MODEL), lambda *g: (layer, 0, 0), pipeline_mode=pl.Buffered(1))
    vspec = lambda n, l: pl.BlockSpec((None, 1, n), lambda *g: (l, 0, 0))
    return [wspec(ATTN_WIDTH), wspec(GLA_DV), wspec(D_MODEL),
            vspec(2 * D_MODEL, layer), vspec(D_MODEL, layer), vspec(D_MODEL, min(layer + 1, DEPTH - 1))]


def _mix_out_prompt(p, p_lr, h_in, n_total, wfu, bfg, gn, sinks, wa, wg, wo, bg, npost, npre,
                    layer, batch, seq, carried):
    last = layer == DEPTH - 1
    chunks = PROMPT_CHUNKS_PER_STEP
    rows = chunks * CHUNK
    assert rows == TM_OUT
    n_steps = seq // rows
    n_tiles = batch * n_steps
    mix_tile = lambda s: jnp.minimum(s, n_tiles - 1)
    out_tile = lambda s: jnp.maximum(s - 1, 0)
    tile = lambda col: pl.BlockSpec((rows, D_MODEL), lambda s: (out_tile(s), col))
    in_specs = (_section_specs(rows, mix_tile) + _layer_weight_specs(layer, 1)
                + [tile(COL_GA // D_MODEL), tile(COL_GG // D_MODEL), tile(0)] + _out_weight_specs(layer))
    n_main = 1 if last else 2
    st_specs, st_shapes, alias_specs, aliases = _stacked_outputs(
        layer, batch, 1, lambda s: jnp.minimum(s // n_steps, batch - 1), len(in_specs), n_main, carried)
    if last:
        out_shape = [jax.ShapeDtypeStruct((n_tiles * rows, D_MODEL), F32)]
    else:
        out_shape = [jax.ShapeDtypeStruct((n_total, D_MODEL), F32), jax.ShapeDtypeStruct((n_total, D_MODEL), BF16)]
    outs = pl.pallas_call(
        functools.partial(_mix_out_prompt_kernel, layer, n_steps, chunks),
        grid=(n_tiles + 1,),
        in_specs=in_specs + alias_specs,
        out_specs=[tile(0)] * n_main + st_specs,
        out_shape=out_shape + st_shapes,
        input_output_aliases=aliases,
        scratch_shapes=[pltpu.VMEM((2, WINDOW, KV_WIDTH), BF16),
                        pltpu.VMEM((2, WINDOW, KV_WIDTH), BF16),
                        pltpu.VMEM((GLA_HEADS, GLA_HK, GLA_HV), F32),
                        pltpu.VMEM((N_KV_HEADS, KEYS, GROUP_ROWS), F32),
                        pltpu.VMEM((rows, GLA_DK), F32),
                        pltpu.VMEM((rows, GLA_DK), F32),
                        pltpu.VMEM((rows // CHUNK, GLA_HEADS, GLA_HK, GLA_HV), BF16),
                        pltpu.VMEM((2, rows, ATTN_WIDTH), BF16),
                        pltpu.VMEM((2, rows, GLA_DV), BF16),
                        pltpu.VMEM((rows, D_MODEL), BF16),
                        pltpu.VMEM((rows, D_MODEL), F32)],
        compiler_params=pltpu.CompilerParams(
            dimension_semantics=("arbitrary",), vmem_limit_bytes=VMEM_LIMIT),
        name="mix_out_prompt",
    )(*([p] * 8), p_lr, wfu, bfg, gn, sinks, p, p, h_in, wa, wg, wo, bg, npost, npre, *(carried or ()))
    return tuple(outs[:n_main]), tuple(outs[n_main:])


def _mix_sample(p, p_lr, wfu, bfg, gn, sinks, cache_k, cache_v, state, layer, row0, batch, carried):
    nb = SAMPLE_SEQS_PER_STEP
    rows = nb * CHUNK
    assert row0 % rows == 0 and batch % nb == 0
    row_block0 = row0 // rows
    row_index = lambda b: row_block0 + b
    cache_spec = pl.BlockSpec((None, nb, WINDOW, KV_WIDTH), lambda b: (layer, b, 0, 0))
    state_spec = pl.BlockSpec((None, nb, GLA_HEADS, GLA_HK, GLA_HV), lambda b: (layer, b, 0, 0, 0))
    in_specs = (_section_specs(rows, row_index) + _layer_weight_specs(layer, 1)
                + [cache_spec, cache_spec, state_spec])
    st_specs, st_shapes, alias_specs, aliases = _stacked_outputs(
        layer, batch, nb, lambda b: b, len(in_specs), 2, carried)
    outs = pl.pallas_call(
        functools.partial(_mix_sample_kernel, layer, nb),
        grid=(batch // nb,),
        in_specs=in_specs + alias_specs,
        out_specs=[pl.BlockSpec((rows, ATTN_WIDTH), lambda b: (b, 0)),
                   pl.BlockSpec((rows, GLA_DV), lambda b: (b, 0))] + st_specs,
        out_shape=[jax.ShapeDtypeStruct((batch * CHUNK, ATTN_WIDTH), BF16),
                   jax.ShapeDtypeStruct((batch * CHUNK, GLA_DV), BF16)] + st_shapes,
        input_output_aliases=aliases,
        scratch_shapes=[pltpu.VMEM((N_KV_HEADS, KEYS, GROUP_ROWS), F32),
                        pltpu.VMEM((rows, GLA_DK), F32),
                        pltpu.VMEM((rows, GLA_DK), F32),
                        pltpu.VMEM((rows // CHUNK, GLA_HEADS, GLA_HK, GLA_HV), BF16)],
        compiler_params=pltpu.CompilerParams(
            dimension_semantics=("arbitrary",), vmem_limit_bytes=VMEM_LIMIT),
        name="mix_sample",
    )(*([p] * 8), p_lr, wfu, bfg, gn, sinks, cache_k, cache_v, state, *(carried or ()))
    return outs[0], outs[1], tuple(outs[2:])


def _out_sample_kernel(layer, xa_ref, xg_ref, ga_ref, gg_ref, h_in_ref, wa_ref, wg_ref, wo_ref,
                       bg_ref, npost_ref, npre_ref, *refs):
    last = layer == DEPTH - 1
    out_refs = refs if last else refs[2:]
    h, u = _out_rows(xa_ref[...], xg_ref[...], ga_ref[...], gg_ref[...], h_in_ref[...],
                     wa_ref, wg_ref, wo_ref, bg_ref[...], npost_ref[...], None if last else npre_ref[...])
    out_refs[0][...] = h
    if not last:
        out_refs[1][...] = u


def _out_sample(xa, xg, p, h_in, row0_h, new_h_u, wa, wg, wo, bg, npost, npre, layer):
    last = layer == DEPTH - 1
    n = xa.shape[0]
    row0 = p.shape[0] - n
    assert n % TM_OUT == 0 and row0 % TM_OUT == 0 and row0_h % TM_OUT == 0
    blk0, blk0_h = row0 // TM_OUT, row0_h // TM_OUT
    tile = lambda base, col: pl.BlockSpec((TM_OUT, D_MODEL), lambda i: (base + i, col))
    in_specs = ([pl.BlockSpec((TM_OUT, ATTN_WIDTH), lambda i: (i, 0)),
                 pl.BlockSpec((TM_OUT, GLA_DV), lambda i: (i, 0)),
                 tile(blk0, COL_GA // D_MODEL), tile(blk0, COL_GG // D_MODEL), tile(blk0_h, 0)]
                + _out_weight_specs(layer))
    if last:
        extra, extra_specs, aliases = (), [], {}
        out_specs = [tile(0, 0)]
        out_shape = [jax.ShapeDtypeStruct((n, D_MODEL), F32)]
    else:
        extra = tuple(new_h_u)
        extra_specs = [pl.BlockSpec(memory_space=pl.ANY)] * 2
        aliases = {len(in_specs): 0, len(in_specs) + 1: 1}
        out_specs = [tile(blk0, 0), tile(blk0, 0)]
        out_shape = [jax.ShapeDtypeStruct(a.shape, a.dtype) for a in new_h_u]
    return pl.pallas_call(
        functools.partial(_out_sample_kernel, layer),
        grid=(n // TM_OUT,),
        in_specs=in_specs + extra_specs,
        out_specs=out_specs,
        out_shape=out_shape,
        input_output_aliases=aliases,
        compiler_params=pltpu.CompilerParams(
            dimension_semantics=("arbitrary",), vmem_limit_bytes=VMEM_LIMIT),
        name="out_sample",
    )(xa, xg, p, p, h_in, wa, wg, wo, bg, npost, npre, *extra)


def kernel(x_prompt, x_sample, cache_k, cache_v, state_gla, norm_pre, norm_post, w_in, b_gate,
           attn_sinks, w_forget_up, b_forget, gla_norm, w_branch_attn, w_branch_gla, w_out):
    batch, seq, _ = x_prompt.shape
    dec_batch, dec_seq, _ = x_sample.shape
    assert dec_seq == CHUNK and seq % (PROMPT_CHUNKS_PER_STEP * CHUNK) == 0
    assert cache_k.shape[2] == WINDOW
    n_prompt = batch * seq
    n_sample = dec_batch * dec_seq

    w_in_t = jnp.swapaxes(w_in, 1, 2)
    wa =w_branch_attn.astype(BF16)
    wg = w_branch_gla.astype(BF16)
    wo = w_out.astype(BF16)
    wfu = jnp.pad(w_forget_up, ((0, 0), (0, LR_PAD - GLA_RANK), (0, 0))).astype(BF16)
    bfg = b_forget.reshape(DEPTH, 1, GLA_DK)
    gn = gla_norm.reshape(DEPTH, 1, GLA_HV)
    bg = b_gate.reshape(DEPTH, 1, 2 * D_MODEL)
    npre = norm_pre.reshape(DEPTH, 1, D_MODEL)
    npost = norm_post.reshape(DEPTH, 1, D_MODEL)
    ck = cache_k.reshape(DEPTH, dec_batch, WINDOW, KV_WIDTH)
    cv = cache_v.reshape(DEPTH, dec_batch, WINDOW, KV_WIDTH)

    xp = x_prompt.reshape(n_prompt, D_MODEL)
    xs = x_sample.reshape(n_sample, D_MODEL)
    u = _prenorm(xp, xs, npre[0])
    h_prompt, h_sample, sample_row0 = xp, xs, 0

    stacked_p = stacked_s = None
    for l in range(DEPTH):
        p, p_lr = _project(u, w_in_t, l)
        out_p, stacked_p = _mix_out_prompt(p, p_lr, h_prompt, n_prompt + n_sample, wfu, bfg, gn, attn_sinks,
                                           wa, wg, wo, bg, npost, npre, l, batch, seq, stacked_p)
        xa_s, xg_s, stacked_s = _mix_sample(p, p_lr, wfu, bfg, gn, attn_sinks, ck, cv, state_gla, l, n_prompt,
                                            dec_batch, stacked_s)
        out_s = _out_sample(xa_s, xg_s, p, h_sample, sample_row0, out_p, wa, wg, wo, bg, npost, npre, l)
        if l < DEPTH - 1:
            h_prompt = h_sample = out_s[0]
            u = out_s[1]
            sample_row0 = n_prompt

    y_prompt = out_p[0].reshape(batch, seq, D_MODEL)
    y_sample = out_s[0].reshape(dec_batch, dec_seq, D_MODEL)
    heads = lambda a: a.reshape(a.shape[:3] + (N_KV_HEADS, HEAD_DIM))
    return (y_prompt, y_sample, heads(stacked_p[1]), heads(stacked_p[2]), stacked_p[0],
            heads(stacked_s[1]), heads(stacked_s[2]), stacked_s[0])
```

```python
import functools
import math

import jax
import jax.numpy as jnp
from jax import lax
from jax.experimental import pallas as pl
from jax.experimental.pallas import tpu as pltpu

F32 = jnp.float32
BF16 = jnp.bfloat16

D_MODEL = 2048
DEPTH = 4
CHUNK = 64
WINDOW = 128
HEAD_DIM = 64
N_Q_HEADS = 16
N_KV_HEADS = 4
GQA_GROUP = N_Q_HEADS // N_KV_HEADS
ATTN_WIDTH = N_Q_HEADS * HEAD_DIM
KV_WIDTH = N_KV_HEADS * HEAD_DIM
GLA_HEADS = 4
GLA_DK = 512
GLA_DV = 1024
GLA_HK = GLA_DK // GLA_HEADS
GLA_HV = GLA_DV // GLA_HEADS
GLA_RANK = 16
GLA_TAU = 16.0
NORM_EPS = 1e-6
LOG2E = math.log2(math.e)
IN_SIZES = (ATTN_WIDTH, KV_WIDTH, KV_WIDTH, ATTN_WIDTH,
            GLA_DK, GLA_DK, GLA_DV, GLA_DV, GLA_RANK, D_MODEL, D_MODEL)

KEYS = WINDOW + CHUNK
SUB = 8
NSUB = CHUNK // SUB

TM_PROJ = 3072
TM_PROJ_SUB = 512
TN_PROJ = 512
TM_OUT = 256
OUT_COLS = 256
TM_NORM = 512
PROMPT_CHUNKS_PER_STEP = 4
SAMPLE_SEQS_PER_STEP = 4

VMEM_LIMIT = 56 * 1024 * 1024

SRC_COLS = {}
_acc = 0
for _name, _size in zip(("qa", "ka", "va", "za", "qg", "kg", "vg", "zg", "lr", "ga", "gg"), IN_SIZES):
    SRC_COLS[_name] = (_acc, _size)
    _acc += _size
IN_COLS = _acc

PACKED_ORDER = ("ga", "gg", "qa", "za", "vg", "zg", "qg", "kg", "ka", "va")
PACKED_COL = {}
PACKED_SRC_ROWS = []
_acc = 0
for _name in PACKED_ORDER:
    _start, _size = SRC_COLS[_name]
    PACKED_COL[_name] = _acc
    if _acc % TN_PROJ == 0:
        PACKED_SRC_ROWS += [_start + k for k in range(0, max(_size, TN_PROJ), TN_PROJ)]
    _acc += _size
PACKED_COLS = _acc
COL_GA, COL_GG, COL_QA, COL_ZA, COL_VG, COL_ZG, COL_QG, COL_KG, COL_KA, COL_VA = (
    PACKED_COL[_n] for _n in PACKED_ORDER)
LR_PAD = 128
assert SRC_COLS["va"][0] == SRC_COLS["ka"][0] + KV_WIDTH and 2 * KV_WIDTH == TN_PROJ
assert len(PACKED_SRC_ROWS) * TN_PROJ == PACKED_COLS
assert all(_r % GLA_RANK == 0 for _r in PACKED_SRC_ROWS)


def _rms(x, g):
    return x * lax.rsqrt(jnp.mean(x * x, axis=-1, keepdims=True) + NORM_EPS) * g


def _two_group_index_maps(n_prompt_tiles):
    p_idx = lambda i: (jnp.minimum(i, n_prompt_tiles - 1), 0)
    s_idx = lambda i: (jnp.maximum(i - n_prompt_tiles, 0), 0)
    return p_idx, s_idx


def _prenorm_kernel(n_prompt_tiles, xp_ref, xs_ref, g_ref, u_ref):
    x = jnp.where(pl.program_id(0) < n_prompt_tiles, xp_ref[...], xs_ref[...])
    u_ref[...] = _rms(x, g_ref[...]).astype(BF16)


def _prenorm(x_prompt, x_sample, g):
    n_prompt_tiles = x_prompt.shape[0] // TM_NORM
    n_tiles = n_prompt_tiles + x_sample.shape[0] // TM_NORM
    p_idx, s_idx = _two_group_index_maps(n_prompt_tiles)
    return pl.pallas_call(
        functools.partial(_prenorm_kernel, n_prompt_tiles),
        grid=(n_tiles,),
        in_specs=[pl.BlockSpec((TM_NORM, D_MODEL), p_idx),
                  pl.BlockSpec((TM_NORM, D_MODEL), s_idx),
                  pl.BlockSpec((1, D_MODEL), lambda i: (0, 0))],
        out_specs=pl.BlockSpec((TM_NORM, D_MODEL), lambda i: (i, 0)),
        out_shape=jax.ShapeDtypeStruct((n_tiles * TM_NORM, D_MODEL), BF16),
        compiler_params=pltpu.CompilerParams(dimension_semantics=("arbitrary",)),
        name="prenorm",
    )(x_prompt, x_sample, g)


def _nt_dot(a, b):
    return lax.dot_general(a, b, (((1,), (1,)), ((), ())), preferred_element_type=F32)


def _proj_kernel(src_rows_ref, u_ref, w_ref, wlr_ref, o_ref, olr_ref):
    del src_rows_ref
    w = w_ref[...].astype(BF16)
    for m in range(TM_PROJ // TM_PROJ_SUB):
        rows = slice(m * TM_PROJ_SUB, (m + 1) * TM_PROJ_SUB)
        o_ref[rows, :] = _nt_dot(u_ref[rows, :], w).astype(BF16)

    @pl.when(pl.program_id(1) == 0)
    def _():
        wlr = wlr_ref[...].astype(BF16)
        for m in range(TM_PROJ // TM_PROJ_SUB):
            rows = slice(m * TM_PROJ_SUB, (m + 1) * TM_PROJ_SUB)
            olr_ref[rows, :] = _nt_dot(u_ref[rows, :], wlr).astype(BF16)


def _project(u, w_in_t, layer):
    t = u.shape[0]
    lr_row = SRC_COLS["lr"][0]
    wspec = lambda rows, index_map: pl.BlockSpec((None, pl.Element(rows), pl.Element(D_MODEL)), index_map)
    grid_spec = pltpu.PrefetchScalarGridSpec(
        num_scalar_prefetch=1,
        grid=(t // TM_PROJ, PACKED_COLS // TN_PROJ),
        in_specs=[pl.BlockSpec((TM_PROJ, D_MODEL), lambda i, j, src: (i, 0)),
                  wspec(TN_PROJ, lambda i, j, src: (layer, src[j] * GLA_RANK, 0)),
                  wspec(LR_PAD, lambda i, j, src: (layer, lr_row, 0))],
        out_specs=[pl.BlockSpec((TM_PROJ, TN_PROJ), lambda i, j, src: (i, j)),
                   pl.BlockSpec((TM_PROJ, LR_PAD), lambda i, j, src: (i, 0))])
    return pl.pallas_call(
        _proj_kernel,
        grid_spec=grid_spec,
        out_shape=[jax.ShapeDtypeStruct((t, PACKED_COLS), BF16),
                   jax.ShapeDtypeStruct((t, LR_PAD), BF16)],
        compiler_params=pltpu.CompilerParams(
            dimension_semantics=("arbitrary", "arbitrary"), vmem_limit_bytes=VMEM_LIMIT),
        name="proj",
    )(jnp.asarray([r // GLA_RANK for r in PACKED_SRC_ROWS], jnp.int32), u, w_in_t, w_in_t)


def _run(stages):
    for _ in stages:
        pass


def _pipelined(tasks):
    prev = None
    for task in tasks:
        next(task)
        if prev is not None:
            _run(prev)
            yield
        prev = task
    _run(prev)
    yield


def _interleave(major, n_major, minor, n_minor):
    done = 0
    for i in range(n_major):
        while done < n_minor and done * n_major < (i + 1) * n_minor:
            next(minor)
            done += 1
        next(major)
    _run(minor)
    _run(major)


def _silu(x):
    return x / (1.0 + jnp.exp(-x))


def _log_sigmoid(x):
    return jnp.minimum(x, 0.0) - jnp.log(1.0 + jnp.exp(-jnp.abs(x)))


def _alibi_slope(h):
    return math.pow(2.0, -8.0 * (h + 1) / N_Q_HEADS)


GROUP_ROWS = GQA_GROUP * CHUNK


def _fill_alibi(bias_ref):
    s = lax.broadcasted_iota(jnp.int32, (KEYS, CHUNK), 0)
    t = lax.broadcasted_iota(jnp.int32, (KEYS, CHUNK), 1)
    dist = jnp.abs(WINDOW + t - s).astype(F32)
    for kh in range(N_KV_HEADS):
        for g in range(GQA_GROUP):
            bias_ref[kh, :, g * CHUNK:(g + 1) * CHUNK] = dist * (-_alibi_slope(kh * GQA_GROUP + g))


def _attention_chunk(r, qa_ref, za_ref, kwin, vwin, sinks_ref, layer, bias_ref, key_lo, xa_ref):
    rows = pl.ds(r, CHUNK)
    lane_head = lax.broadcasted_iota(jnp.int32, (1, GROUP_ROWS), 1) // CHUNK
    if key_lo is not None:
        key_ok = lax.broadcasted_iota(jnp.int32, (KEYS, GROUP_ROWS), 0) >= key_lo

    def head(kh):
        k = kwin(pl.ds(kh * HEAD_DIM, HEAD_DIM))
        v = vwin(pl.ds(kh * HEAD_DIM, HEAD_DIM))
        q = jnp.concatenate(
            [qa_ref[rows, pl.ds((kh * GQA_GROUP + g) * HEAD_DIM, HEAD_DIM)] for g in range(GQA_GROUP)],
            axis=0)
        q = q * jnp.asarray(HEAD_DIM ** -0.5, BF16)
        s = _nt_dot(k, q) + bias_ref[kh]
        if key_lo is not None:
            s = jnp.where(key_ok, s, -jnp.inf)
        sink = jnp.zeros((1, GROUP_ROWS), F32)
        for g in range(GQA_GROUP):
            sink = jnp.where(lane_head == g, sinks_ref[layer, kh * GQA_GROUP + g], sink)
        m = jnp.maximum(jnp.max(s, axis=0, keepdims=True), sink)
        p = jnp.exp(s - m)
        den = jnp.sum(p, axis=0, keepdims=True) + jnp.exp(sink - m)
        p = p.astype(BF16)
        yield
        o_t = lax.dot_general(v, p, (((0,), (0,)), ((), ())), preferred_element_type=F32)
        o = jnp.transpose(o_t / den)
        for g in range(GQA_GROUP):
            h = kh * GQA_GROUP + g
            cols = pl.ds(h * HEAD_DIM, HEAD_DIM)
            z = za_ref[rows, cols].astype(F32)
            xa_ref[rows, cols] = (o[g * CHUNK:(g + 1) * CHUNK] * _silu(z)).astype(BF16)
        yield

    return [head(kh) for kh in range(N_KV_HEADS)]


def _gla_prepare(lr_ref, kg_ref, wfu_ref, bf_ref, k_scr, b_scr):
    n = lr_ref.shape[0]
    la2 = _log2_forget(lr_ref[...], wfu_ref, bf_ref)
    row = lax.broadcasted_iota(jnp.int32, (n, n), 0)
    col = lax.broadcasted_iota(jnp.int32, (n, n), 1)
    tri = jnp.where(row // CHUNK == col // CHUNK, row - col, -1) >= 0
    tri = jnp.where(tri, 1.0, 0.0).astype(BF16)
    la_hi = la2.astype(BF16)
    rem = la2 - la_hi.astype(F32)
    la_mid = rem.astype(BF16)
    la_lo = (rem - la_mid.astype(F32)).astype(BF16)
    b_scr[...] = (jnp.dot(tri, la_hi, preferred_element_type=F32)
                  + jnp.dot(tri, la_mid, preferred_element_type=F32)
                  + jnp.dot(tri, la_lo, preferred_element_type=F32))
    k_scr[...] = kg_ref[...].astype(F32)


def _gla_states(chunk_states, vg_ref, k_scr, b_scr, s0_scr):
    for hh in range(GLA_HEADS):
        kcols = pl.ds(hh * GLA_HK, GLA_HK)
        vcols = pl.ds(hh * GLA_HV, GLA_HV)
        steps = []
        for c in range(len(chunk_states)):
            rows = pl.ds(c * CHUNK, CHUNK)
            b_end = b_scr[c * CHUNK + CHUNK - 1:(c + 1) * CHUNK, kcols]
            k_til = (k_scr[rows, kcols] * jnp.exp2(b_end - b_scr[rows, kcols])).astype(BF16)
            upd = lax.dot_general(k_til, vg_ref[rows, vcols], (((0,), (0,)), ((), ())),
                                  preferred_element_type=F32)
            e_col = jnp.transpose(jnp.broadcast_to(jnp.exp2(b_end), (GLA_HK, GLA_HK)))
            steps.append((jnp.concatenate([e_col, e_col], axis=1), upd))
        s = None
        for c, ((in_ref, out_ref), (decay, upd)) in enumerate(zip(chunk_states, steps)):
            if c == 0 or in_ref is not chunk_states[c - 1][1]:
                s = in_ref[hh]
            s0_scr[c, hh] = s.astype(BF16)
            s = s * decay + upd
            if c + 1 == len(chunk_states) or chunk_states[c + 1][0] is not out_ref:
                out_ref[hh] = s
        yield


def _gla_chunk(c, qg_ref, vg_ref, zg_ref, s0_scr, gnorm, k_scr, b_scr, xg_ref):
    r = c * CHUNK
    rows = pl.ds(r, CHUNK)
    lane = lax.broadcasted_iota(jnp.int32, (SUB, CHUNK), 1)
    row0 = lax.broadcasted_iota(jnp.int32, (SUB, CHUNK), 0)

    def head(hh):
        kcols = pl.ds(hh * GLA_HK, GLA_HK)
        vcols = pl.ds(hh * GLA_HV, GLA_HV)
        q = qg_ref[rows, kcols].astype(F32) * (GLA_HK ** -0.5)
        k = k_scr[rows, kcols]
        v = vg_ref[rows, vcols]
        b = b_scr[rows, kcols]
        blk = lambda x, j: x[j * SUB:(j + 1) * SUB]
        b_row = lambda t: b_scr[r + t:r + t + 1, kcols]
        b_last = [b_row(j * SUB + SUB - 1) for j in range(NSUB)]

        k_hat = jnp.concatenate([blk(k, j) * jnp.exp2(b_last[j] - blk(b, j)) for j in range(NSUB)], axis=0)
        lhs = jnp.concatenate([q[(j + 1) * SUB:] * jnp.exp2(b[(j + 1) * SUB:] - b_last[j])
                               for j in range(NSUB - 1)], axis=0)
        rr = _nt_dot(lhs.astype(BF16), k_hat.astype(BF16))
        o_inter = jnp.dot((q * jnp.exp2(b)).astype(BF16), s0_scr[c, hh], preferred_element_type=F32)
        yield

        score_rows = []
        for i in range(NSUB):
            acc = jnp.zeros((SUB, CHUNK), F32)
            for j in range(i):
                base = sum((NSUB - 1 - jj) * SUB for jj in range(j)) + (i - j - 1) * SUB
                acc = jnp.where(lane // SUB == j, rr[base:base + SUB], acc)
            qi, bi = blk(q, i), blk(b, i)
            for s in range(i * SUB, (i + 1) * SUB):
                prod = (qi * k_scr[r + s:r + s + 1, kcols]) * jnp.exp2(bi - b_row(s))
                acc = jnp.where(lane == s, jnp.sum(prod, axis=1, keepdims=True), acc)
            score_rows.append(jnp.where(lane <= row0 + i * SUB, acc, 0.0))
        scores = jnp.concatenate(score_rows, axis=0)

        o = jnp.dot(scores.astype(BF16), v, preferred_element_type=F32) + o_inter

        y = _rms(o, gnorm)
        z = zg_ref[rows, vcols].astype(F32)
        xg_ref[rows, vcols] = (y * _silu(z)).astype(BF16)
        yield

    return [head(hh) for hh in range(GLA_HEADS)]


def _log2_forget(lr, wfu_ref, bf_ref):
    x = jnp.dot(lr, wfu_ref[...], preferred_element_type=F32) + bf_ref[...]
    return _log_sigmoid(x) * (LOG2E / GLA_TAU)


def _sigmoid(x):
    return 1.0 / (1.0 + jnp.exp(-x))


def _out_rows(xa, xg, ga, gg, h_in, wa_ref, wg_ref, wo_ref, bg, npost, npre):
    ya = jnp.dot(xa, wa_ref[...], preferred_element_type=F32)
    yg = jnp.dot(xg, wg_ref[...], preferred_element_type=F32)
    gate_a = _sigmoid(ga.astype(F32) + bg[:, :D_MODEL])
    gate_g = _sigmoid(gg.astype(F32) + bg[:, D_MODEL:])
    merged = (gate_a * ya + gate_g * yg).astype(BF16)
    z = jnp.dot(merged, wo_ref[...], preferred_element_type=F32)
    h = h_in + _rms(z, npost)
    return h, (None if npre is None else _rms(h, npre).astype(BF16))


def _mix_out_prompt_kernel(layer, n_steps, chunks, *refs):
    last = layer == DEPTH - 1
    n_carried = 3 if layer > 0 else 0
    (qa_ref, za_ref, vg_ref, zg_ref, qg_ref, kg_ref, ka_ref, va_ref, lr_ref,
     wfu_ref, bf_ref, gn_ref, sinks_ref,
     ga_ref, gg_ref, h_in_ref, wa_ref, wg_ref, wo_ref, bg_ref, npost_ref, npre_ref) = refs[:22]
    refs = refs[22 + n_carried:]
    out_refs, refs = refs[:1 if last else 2], refs[1 if last else 2:]
    (st_ref, knew_ref, vnew_ref,
     halo_k, halo_v, state, bias_scr, k_scr, b_scr, s0_scr, xa_buf, xg_buf, merged_scr, z_scr) = refs
    s = pl.program_id(0)
    b = s // n_steps
    t = s % n_steps
    assert chunks * CHUNK >= WINDOW
    xa_ref = xa_buf.at[s % 2]
    xg_ref = xg_buf.at[s % 2]

    @pl.when(s == 0)
    def _():
        xa_buf[1] = jnp.zeros(xa_buf.shape[1:], BF16)
        xg_buf[1] = jnp.zeros(xg_buf.shape[1:], BF16)

    xa_prev = xa_buf[1 - s % 2]
    xg_prev = xg_buf[1 - s % 2]
    n_col_blocks = D_MODEL // OUT_COLS

    def out_stages():
        bg = bg_ref[...]
        for n in range(n_col_blocks):
            cols = slice(n * OUT_COLS, (n + 1) * OUT_COLS)
            ya = jnp.dot(xa_prev, wa_ref[:, cols], preferred_element_type=F32)
            yield
            yg = jnp.dot(xg_prev, wg_ref[:, cols], preferred_element_type=F32)
            gate_a = _sigmoid(ga_ref[:, cols].astype(F32) + bg[:, n * OUT_COLS:(n + 1) * OUT_COLS])
            gate_g = _sigmoid(gg_ref[:, cols].astype(F32)
                              + bg[:, D_MODEL + n * OUT_COLS:D_MODEL + (n + 1) * OUT_COLS])
            merged_scr[:, cols] = (gate_a * ya + gate_g * yg).astype(BF16)
            yield
        for n in range(n_col_blocks):
            cols = slice(n * OUT_COLS, (n + 1) * OUT_COLS)
            z_scr[:, cols] = jnp.dot(merged_scr[...], wo_ref[:, cols], preferred_element_type=F32)
            yield
        h = h_in_ref[...] + _rms(z_scr[...], npost_ref[...])
        out_refs[0][...] = h
        if not last:
            out_refs[1][...] = _rms(h, npre_ref[...]).astype(BF16)
        yield

    cur = t % 2
    nxt = 1 - cur

    @pl.when((b == 0) & (t == 0))
    def _():
        _fill_alibi(bias_scr)

    @pl.when(t == 0)
    def _():
        halo_k[0] = jnp.zeros((WINDOW, KV_WIDTH), BF16)
        halo_v[0] = jnp.zeros((WINDOW, KV_WIDTH), BF16)
        state[...] = jnp.zeros(state.shape, F32)

    def window(halo, tile_ref, c, cols):
        lo = c * CHUNK - WINDOW
        if lo >= 0:
            return tile_ref[lo:lo + KEYS, cols]
        return jnp.concatenate([halo[cur, WINDOW + lo:, cols], tile_ref[0:(c + 1) * CHUNK, cols]], axis=0)

    def mix_stages():
        _gla_prepare(lr_ref, kg_ref, wfu_ref, bf_ref, k_scr, b_scr)
        yield
        yield from _gla_states([(state, state)] * chunks, vg_ref, k_scr, b_scr, s0_scr)
        gnorm = gn_ref[...]
        heads = []
        for c in range(chunks):
            key_lo = (WINDOW // CHUNK - (t * chunks + c)) * CHUNK if c < WINDOW // CHUNK else None
            heads += _attention_chunk(c * CHUNK, qa_ref, za_ref, functools.partial(window, halo_k, ka_ref, c),
                                      functools.partial(window, halo_v, va_ref, c), sinks_ref, layer, bias_scr,
                                      key_lo, xa_ref)
            heads += _gla_chunk(c, qg_ref, vg_ref, zg_ref, s0_scr, gnorm, k_scr, b_scr, xg_ref)
        yield from _pipelined(heads)
        halo_k[nxt] = ka_ref[chunks * CHUNK - WINDOW:, :]
        halo_v[nxt] = va_ref[chunks * CHUNK - WINDOW:, :]

    n_mix = 1 + GLA_HEADS + chunks * (N_KV_HEADS + GLA_HEADS)
    n_out = 3 * n_col_blocks + 1
    _interleave(mix_stages(), n_mix, out_stages(), n_out)

    @pl.when(t == n_steps - 1)
    def _():
        st_ref[0] = state[...]
        knew_ref[0] = ka_ref[chunks * CHUNK - WINDOW:, :].astype(F32)
        vnew_ref[0] = va_ref[chunks * CHUNK - WINDOW:, :].astype(F32)


def _mix_sample_kernel(layer, n_seqs, *refs):
    n_carried = 3 if layer > 0 else 0
    (qa_ref, za_ref, vg_ref, zg_ref, qg_ref, kg_ref, ka_ref, va_ref, lr_ref,
     wfu_ref, bf_ref, gn_ref, sinks_ref, ck_ref, cv_ref, s0_ref) = refs[:16]
    xa_ref, xg_ref, st_ref, knew_ref, vnew_ref, bias_scr, k_scr, b_scr, s0_scr = refs[16 + n_carried:]

    @pl.when(pl.program_id(0) == 0)
    def _():
        _fill_alibi(bias_scr)

    _gla_prepare(lr_ref, kg_ref, wfu_ref, bf_ref, k_scr, b_scr)
    _run(_gla_states([(s0_ref.at[g], st_ref.at[g]) for g in range(n_seqs)], vg_ref, k_scr, b_scr, s0_scr))
    gnorm = gn_ref[...]

    def window(cache_ref, new_ref, g, cols):
        return jnp.concatenate([cache_ref[g, :, cols].astype(BF16), new_ref[g * CHUNK:(g + 1) * CHUNK, cols]], axis=0)

    heads = []
    for g in range(n_seqs):
        r = g * CHUNK
        knew_ref[g, 0:WINDOW - CHUNK, :] = ck_ref[g, CHUNK:, :]
        knew_ref[g, WINDOW - CHUNK:, :] = ka_ref[r:r + CHUNK, :].astype(F32)
        vnew_ref[g, 0:WINDOW - CHUNK, :] = cv_ref[g, CHUNK:, :]
        vnew_ref[g, WINDOW - CHUNK:, :] = va_ref[r:r + CHUNK, :].astype(F32)
        heads += _attention_chunk(r, qa_ref, za_ref, functools.partial(window, ck_ref, ka_ref, g),
                                  functools.partial(window, cv_ref, va_ref, g), sinks_ref, layer, bias_scr,
                                  None, xa_ref)
        heads += _gla_chunk(g, qg_ref, vg_ref, zg_ref, s0_scr, gnorm, k_scr, b_scr, xg_ref)
    for head in heads:
        _run(head)


def _section_specs(rows, row_index):
    def spec(width, col):
        return pl.BlockSpec((rows, width), functools.partial(
            lambda cb, *g: (row_index(*g), cb), col // width))
    return [spec(ATTN_WIDTH, COL_QA), spec(ATTN_WIDTH, COL_ZA), spec(GLA_DV, COL_VG), spec(GLA_DV, COL_ZG),
            spec(GLA_DK, COL_QG), spec(GLA_DK, COL_KG), spec(KV_WIDTH, COL_KA), spec(KV_WIDTH, COL_VA),
            spec(LR_PAD, 0)]


def _layer_weight_specs(layer, ngrid):
    zeros = (0,) * 2
    return [pl.BlockSpec((None, LR_PAD, GLA_DK), lambda *g: (layer,) + zeros),
            pl.BlockSpec((None, 1, GLA_DK), lambda *g: (layer,) + zeros),
            pl.BlockSpec((None, 1, GLA_HV), lambda *g: (layer,) + zeros),
            pl.BlockSpec(memory_space=pltpu.SMEM)]


def _stacked_outputs(layer, batch, seqs_per_step, batch_block, n_inputs, first_out, carried):
    lead = lambda *g: (layer, batch_block(*g))
    nb = seqs_per_step
    specs = [pl.BlockSpec((None, nb, GLA_HEADS, GLA_HK, GLA_HV), lambda *g: lead(*g) + (0, 0, 0)),
             pl.BlockSpec((None, nb, WINDOW, KV_WIDTH), lambda *g: lead(*g) + (0, 0)),
             pl.BlockSpec((None, nb, WINDOW, KV_WIDTH), lambda *g: lead(*g) + (0, 0))]
    shapes = [jax.ShapeDtypeStruct((DEPTH, batch, GLA_HEADS, GLA_HK, GLA_HV), F32),
              jax.ShapeDtypeStruct((DEPTH, batch, WINDOW, KV_WIDTH), F32),
              jax.ShapeDtypeStruct((DEPTH, batch, WINDOW, KV_WIDTH), F32)]
    if carried is None:
        return specs, shapes, [], {}
    alias_specs = [pl.BlockSpec(memory_space=pl.ANY)] * 3
    aliases = {n_inputs + k: first_out + k for k in range(3)}
    return specs, shapes, alias_specs, aliases


def _out_weight_specs(layer):
    wspec = lambda k: pl.BlockSpec((None, k, D_MODEL), lambda *g: (layer, 0, 0), pipeline_mode=pl.Buffered(1))
    vspec = lambda n, l: pl.BlockSpec((None, 1, n), lambda *g: (l, 0, 0))
    return [wspec(ATTN_WIDTH), wspec(GLA_DV), wspec(D_MODEL),
            vspec(2 * D_MODEL, layer), vspec(D_MODEL, layer), vspec(D_MODEL, min(layer + 1, DEPTH - 1))]


def _mix_out_prompt(p, p_lr, h_in, n_total, wfu, bfg, gn, sinks, wa, wg, wo, bg, npost, npre,
                    layer, batch, seq, carried):
    last = layer == DEPTH - 1
    chunks = PROMPT_CHUNKS_PER_STEP
    rows = chunks * CHUNK
    assert rows == TM_OUT
    n_steps = seq // rows
    n_tiles = batch * n_steps
    mix_tile = lambda s: jnp.minimum(s, n_tiles - 1)
    out_tile = lambda s: jnp.maximum(s - 1, 0)
    tile = lambda col: pl.BlockSpec((rows, D_MODEL), lambda s: (out_tile(s), col))
    in_specs = (_section_specs(rows, mix_tile) + _layer_weight_specs(layer, 1)
                + [tile(COL_GA // D_MODEL), tile(COL_GG // D_MODEL), tile(0)] + _out_weight_specs(layer))
    n_main = 1 if last else 2
    st_specs, st_shapes, alias_specs, aliases = _stacked_outputs(
        layer, batch, 1, lambda s: jnp.minimum(s // n_steps, batch - 1), len(in_specs), n_main, carried)
    if last:
        out_shape = [jax.ShapeDtypeStruct((n_tiles * rows, D_MODEL), F32)]
    else:
        out_shape = [jax.ShapeDtypeStruct((n_total, D_MODEL), F32), jax.ShapeDtypeStruct((n_total, D_MODEL), BF16)]
    outs = pl.pallas_call(
        functools.partial(_mix_out_prompt_kernel, layer, n_steps, chunks),
        grid=(n_tiles + 1,),
        in_specs=in_specs + alias_specs,
        out_specs=[tile(0)] * n_main + st_specs,
        out_shape=out_shape + st_shapes,
        input_output_aliases=aliases,
        scratch_shapes=[pltpu.VMEM((2, WINDOW, KV_WIDTH), BF16),
                        pltpu.VMEM((2, WINDOW, KV_WIDTH), BF16),
                        pltpu.VMEM((GLA_HEADS, GLA_HK, GLA_HV), F32),
                        pltpu.VMEM((N_KV_HEADS, KEYS, GROUP_ROWS), F32),
                        pltpu.VMEM((rows, GLA_DK), F32),
                        pltpu.VMEM((rows, GLA_DK), F32),
                        pltpu.VMEM((rows // CHUNK, GLA_HEADS, GLA_HK, GLA_HV), BF16),
                        pltpu.VMEM((2, rows, ATTN_WIDTH), BF16),
                        pltpu.VMEM((2, rows, GLA_DV), BF16),
                        pltpu.VMEM((rows, D_MODEL), BF16),
                        pltpu.VMEM((rows, D_MODEL), F32)],
        compiler_params=pltpu.CompilerParams(
            dimension_semantics=("arbitrary",), vmem_limit_bytes=VMEM_LIMIT),
        name="mix_out_prompt",
    )(*([p] * 8), p_lr, wfu, bfg, gn, sinks, p, p, h_in, wa, wg, wo, bg, npost, npre, *(carried or ()))
    return tuple(outs[:n_main]), tuple(outs[n_main:])


def _mix_sample(p, p_lr, wfu, bfg, gn, sinks, cache_k, cache_v, state, layer, row0, batch, carried):
    nb = SAMPLE_SEQS_PER_STEP
    rows = nb * CHUNK
    assert row0 % rows == 0 and batch % nb == 0
    row_block0 = row0 // rows
    row_index = lambda b: row_block0 + b
    cache_spec = pl.BlockSpec((None, nb, WINDOW, KV_WIDTH), lambda b: (layer, b, 0, 0))
    state_spec = pl.BlockSpec((None, nb, GLA_HEADS, GLA_HK, GLA_HV), lambda b: (layer, b, 0, 0, 0))
    in_specs = (_section_specs(rows, row_index) + _layer_weight_specs(layer, 1)
                + [cache_spec, cache_spec, state_spec])
    st_specs, st_shapes, alias_specs, aliases = _stacked_outputs(
        layer, batch, nb, lambda b: b, len(in_specs), 2, carried)
    outs = pl.pallas_call(
        functools.partial(_mix_sample_kernel, layer, nb),
        grid=(batch // nb,),
        in_specs=in_specs + alias_specs,
        out_specs=[pl.BlockSpec((rows, ATTN_WIDTH), lambda b: (b, 0)),
                   pl.BlockSpec((rows, GLA_DV), lambda b: (b, 0))] + st_specs,
        out_shape=[jax.ShapeDtypeStruct((batch * CHUNK, ATTN_WIDTH), BF16),
                   jax.ShapeDtypeStruct((batch * CHUNK, GLA_DV), BF16)] + st_shapes,
        input_output_aliases=aliases,
        scratch_shapes=[pltpu.VMEM((N_KV_HEADS, KEYS, GROUP_ROWS), F32),
                        pltpu.VMEM((rows, GLA_DK), F32),
                        pltpu.VMEM((rows, GLA_DK), F32),
                        pltpu.VMEM((rows // CHUNK, GLA_HEADS, GLA_HK, GLA_HV), BF16)],
        compiler_params=pltpu.CompilerParams(
            dimension_semantics=("arbitrary",), vmem_limit_bytes=VMEM_LIMIT),
        name="mix_sample",
    )(*([p] * 8), p_lr, wfu, bfg, gn, sinks, cache_k, cache_v, state, *(carried or ()))
    return outs[0], outs[1], tuple(outs[2:])


def _out_sample_kernel(layer, xa_ref, xg_ref, ga_ref, gg_ref, h_in_ref, wa_ref, wg_ref, wo_ref,
                       bg_ref, npost_ref, npre_ref, *refs):
    last = layer == DEPTH - 1
    out_refs = refs if last else refs[2:]
    h, u = _out_rows(xa_ref[...], xg_ref[...], ga_ref[...], gg_ref[...], h_in_ref[...],
                     wa_ref, wg_ref, wo_ref, bg_ref[...], npost_ref[...], None if last else npre_ref[...])
    out_refs[0][...] = h
    if not last:
        out_refs[1][...] = u


def _out_sample(xa, xg, p, h_in, row0_h, new_h_u, wa, wg, wo, bg, npost, npre, layer):
    last = layer == DEPTH - 1
    n = xa.shape[0]
    row0 = p.shape[0] - n
    assert n % TM_OUT == 0 and row0 % TM_OUT == 0 and row0_h % TM_OUT == 0
    blk0, blk0_h = row0 // TM_OUT, row0_h // TM_OUT
    tile = lambda base, col: pl.BlockSpec((TM_OUT, D_MODEL), lambda i: (base + i, col))
    in_specs = ([pl.BlockSpec((TM_OUT, ATTN_WIDTH), lambda i: (i, 0)),
                 pl.BlockSpec((TM_OUT, GLA_DV), lambda i: (i, 0)),
                 tile(blk0, COL_GA // D_MODEL), tile(blk0, COL_GG // D_MODEL), tile(blk0_h, 0)]
                + _out_weight_specs(layer))
    if last:
        extra, extra_specs, aliases = (), [], {}
        out_specs = [tile(0, 0)]
        out_shape = [jax.ShapeDtypeStruct((n, D_MODEL), F32)]
    else:
        extra = tuple(new_h_u)
        extra_specs = [pl.BlockSpec(memory_space=pl.ANY)] * 2
        aliases = {len(in_specs): 0, len(in_specs) + 1: 1}
        out_specs = [tile(blk0, 0), tile(blk0, 0)]
        out_shape = [jax.ShapeDtypeStruct(a.shape, a.dtype) for a in new_h_u]
    return pl.pallas_call(
        functools.partial(_out_sample_kernel, layer),
        grid=(n // TM_OUT,),
        in_specs=in_specs + extra_specs,
        out_specs=out_specs,
        out_shape=out_shape,
        input_output_aliases=aliases,
        compiler_params=pltpu.CompilerParams(
            dimension_semantics=("arbitrary",), vmem_limit_bytes=VMEM_LIMIT),
        name="out_sample",
    )(xa, xg, p, p, h_in, wa, wg, wo, bg, npost, npre, *extra)


def kernel(x_prompt, x_sample, cache_k, cache_v, state_gla, norm_pre, norm_post, w_in, b_gate,
           attn_sinks, w_forget_up, b_forget, gla_norm, w_branch_attn, w_branch_gla, w_out):
    batch, seq, _ = x_prompt.shape
    dec_batch, dec_seq, _ = x_sample.shape
    assert dec_seq == CHUNK and seq % (PROMPT_CHUNKS_PER_STEP * CHUNK) == 0
    assert cache_k.shape[2] == WINDOW
    n_prompt = batch * seq
    n_sample = dec_batch * dec_seq

    w_in_t = jnp.swapaxes(w_in, 1, 2)
    wa =w_branch_attn.astype(BF16)
    wg = w_branch_gla.astype(BF16)
    wo = w_out.astype(BF16)
    wfu = jnp.pad(w_forget_up, ((0, 0), (0, LR_PAD - GLA_RANK), (0, 0))).astype(BF16)
    bfg = b_forget.reshape(DEPTH, 1, GLA_DK)
    gn = gla_norm.reshape(DEPTH, 1, GLA_HV)
    bg = b_gate.reshape(DEPTH, 1, 2 * D_MODEL)
    npre = norm_pre.reshape(DEPTH, 1, D_MODEL)
    npost = norm_post.reshape(DEPTH, 1, D_MODEL)
    ck = cache_k.reshape(DEPTH, dec_batch, WINDOW, KV_WIDTH)
    cv = cache_v.reshape(DEPTH, dec_batch, WINDOW, KV_WIDTH)

    xp = x_prompt.reshape(n_prompt, D_MODEL)
    xs = x_sample.reshape(n_sample, D_MODEL)
    u = _prenorm(xp, xs, npre[0])
    h_prompt, h_sample, sample_row0 = xp, xs, 0

    stacked_p = stacked_s = None
    for l in range(DEPTH):
        p, p_lr = _project(u, w_in_t, l)
        out_p, stacked_p = _mix_out_prompt(p, p_lr, h_prompt, n_prompt + n_sample, wfu, bfg, gn, attn_sinks,
                                           wa, wg, wo, bg, npost, npre, l, batch, seq, stacked_p)
        xa_s, xg_s, stacked_s = _mix_sample(p, p_lr, wfu, bfg, gn, attn_sinks, ck, cv, state_gla, l, n_prompt,
                                            dec_batch, stacked_s)
        out_s = _out_sample(xa_s, xg_s, p, h_sample, sample_row0, out_p, wa, wg, wo, bg, npost, npre, l)
        if l < DEPTH - 1:
            h_prompt = h_sample = out_s[0]
            u = out_s[1]
            sample_row0 = n_prompt

    y_prompt = out_p[0].reshape(batch, seq, D_MODEL)
    y_sample = out_s[0].reshape(dec_batch, dec_seq, D_MODEL)
    heads = lambda a: a.reshape(a.shape[:3] + (N_KV_HEADS, HEAD_DIM))
    return (y_prompt, y_sample, heads(stacked_p[1]), heads(stacked_p[2]), stacked_p[0],
            heads(stacked_s[1]), heads(stacked_s[2]), stacked_s[0])
```

```python
import functools
import math

import jax
import jax.numpy as jnp
from jax import lax
from jax.experimental import pallas as pl
from jax.experimental.pallas import tpu as pltpu

F32 = jnp.float32
BF16 = jnp.bfloat16

D_MODEL = 2048
DEPTH = 4
CHUNK = 64
WINDOW = 128
HEAD_DIM = 64
N_Q_HEADS = 16
N_KV_HEADS = 4
GQA_GROUP = N_Q_HEADS // N_KV_HEADS
ATTN_WIDTH = N_Q_HEADS * HEAD_DIM
KV_WIDTH = N_KV_HEADS * HEAD_DIM
GLA_HEADS = 4
GLA_DK = 512
GLA_DV = 1024
GLA_HK = GLA_DK // GLA_HEADS
GLA_HV = GLA_DV // GLA_HEADS
GLA_RANK = 16
GLA_TAU = 16.0
NORM_EPS = 1e-6
LOG2E = math.log2(math.e)
IN_SIZES = (ATTN_WIDTH, KV_WIDTH, KV_WIDTH, ATTN_WIDTH,
            GLA_DK, GLA_DK, GLA_DV, GLA_DV, GLA_RANK, D_MODEL, D_MODEL)

KEYS = WINDOW + CHUNK
SUB = 8
NSUB = CHUNK // SUB

TM_PROJ = 3072
TM_PROJ_SUB = 512
TN_PROJ = 512
TM_OUT = 256
OUT_COLS = 256
TM_NORM = 512
PROMPT_CHUNKS_PER_STEP = 4
SAMPLE_SEQS_PER_STEP = 2

VMEM_LIMIT = 56 * 1024 * 1024

SRC_COLS = {}
_acc = 0
for _name, _size in zip(("qa", "ka", "va", "za", "qg", "kg", "vg", "zg", "lr", "ga", "gg"), IN_SIZES):
    SRC_COLS[_name] = (_acc, _size)
    _acc += _size
IN_COLS = _acc

PACKED_ORDER = ("ga", "gg", "qa", "za", "vg", "zg", "qg", "kg", "ka", "va")
PACKED_COL = {}
PACKED_SRC_ROWS = []
_acc = 0
for _name in PACKED_ORDER:
    _start, _size = SRC_COLS[_name]
    PACKED_COL[_name] = _acc
    if _acc % TN_PROJ == 0:
        PACKED_SRC_ROWS += [_start + k for k in range(0, max(_size, TN_PROJ), TN_PROJ)]
    _acc += _size
PACKED_COLS = _acc
COL_GA, COL_GG, COL_QA, COL_ZA, COL_VG, COL_ZG, COL_QG, COL_KG, COL_KA, COL_VA = (
    PACKED_COL[_n] for _n in PACKED_ORDER)
LR_PAD = 128
assert SRC_COLS["va"][0] == SRC_COLS["ka"][0] + KV_WIDTH and 2 * KV_WIDTH == TN_PROJ
assert len(PACKED_SRC_ROWS) * TN_PROJ == PACKED_COLS
assert all(_r % GLA_RANK == 0 for _r in PACKED_SRC_ROWS)


def _rms(x, g):
    return x * lax.rsqrt(jnp.mean(x * x, axis=-1, keepdims=True) + NORM_EPS) * g


def _two_group_index_maps(n_prompt_tiles):
    p_idx = lambda i: (jnp.minimum(i, n_prompt_tiles - 1), 0)
    s_idx = lambda i: (jnp.maximum(i - n_prompt_tiles, 0), 0)
    return p_idx, s_idx


def _prenorm_kernel(n_prompt_tiles, xp_ref, xs_ref, g_ref, u_ref):
    x = jnp.where(pl.program_id(0) < n_prompt_tiles, xp_ref[...], xs_ref[...])
    u_ref[...] = _rms(x, g_ref[...]).astype(BF16)


def _prenorm(x_prompt, x_sample, g):
    n_prompt_tiles = x_prompt.shape[0] // TM_NORM
    n_tiles = n_prompt_tiles + x_sample.shape[0] // TM_NORM
    p_idx, s_idx = _two_group_index_maps(n_prompt_tiles)
    return pl.pallas_call(
        functools.partial(_prenorm_kernel, n_prompt_tiles),
        grid=(n_tiles,),
        in_specs=[pl.BlockSpec((TM_NORM, D_MODEL), p_idx),
                  pl.BlockSpec((TM_NORM, D_MODEL), s_idx),
                  pl.BlockSpec((1, D_MODEL), lambda i: (0, 0))],
        out_specs=pl.BlockSpec((TM_NORM, D_MODEL), lambda i: (i, 0)),
        out_shape=jax.ShapeDtypeStruct((n_tiles * TM_NORM, D_MODEL), BF16),
        compiler_params=pltpu.CompilerParams(dimension_semantics=("arbitrary",)),
        name="prenorm",
    )(x_prompt, x_sample, g)


def _nt_dot(a, b):
    return lax.dot_general(a, b, (((1,), (1,)), ((), ())), preferred_element_type=F32)


def _proj_kernel(src_rows_ref, u_ref, w_ref, wlr_ref, o_ref, olr_ref):
    del src_rows_ref
    w = w_ref[...].astype(BF16)
    for m in range(TM_PROJ // TM_PROJ_SUB):
        rows = slice(m * TM_PROJ_SUB, (m + 1) * TM_PROJ_SUB)
        o_ref[rows, :] = _nt_dot(u_ref[rows, :], w).astype(BF16)

    @pl.when(pl.program_id(1) == 0)
    def _():
        wlr = wlr_ref[...].astype(BF16)
        for m in range(TM_PROJ // TM_PROJ_SUB):
            rows = slice(m * TM_PROJ_SUB, (m + 1) * TM_PROJ_SUB)
            olr_ref[rows, :] = _nt_dot(u_ref[rows, :], wlr).astype(BF16)


def _project(u, w_in_t, layer):
    t = u.shape[0]
    lr_row = SRC_COLS["lr"][0]
    wspec = lambda rows, index_map: pl.BlockSpec((None, pl.Element(rows), pl.Element(D_MODEL)), index_map)
    grid_spec = pltpu.PrefetchScalarGridSpec(
        num_scalar_prefetch=1,
        grid=(t // TM_PROJ, PACKED_COLS // TN_PROJ),
        in_specs=[pl.BlockSpec((TM_PROJ, D_MODEL), lambda i, j, src: (i, 0)),
                  wspec(TN_PROJ, lambda i, j, src: (layer, src[j] * GLA_RANK, 0)),
                  wspec(LR_PAD, lambda i, j, src: (layer, lr_row, 0))],
        out_specs=[pl.BlockSpec((TM_PROJ, TN_PROJ), lambda i, j, src: (i, j)),
                   pl.BlockSpec((TM_PROJ, LR_PAD), lambda i, j, src: (i, 0))])
    return pl.pallas_call(
        _proj_kernel,
        grid_spec=grid_spec,
        out_shape=[jax.ShapeDtypeStruct((t, PACKED_COLS), BF16),
                   jax.ShapeDtypeStruct((t, LR_PAD), BF16)],
        compiler_params=pltpu.CompilerParams(
            dimension_semantics=("arbitrary", "arbitrary"), vmem_limit_bytes=VMEM_LIMIT),
        name="proj",
    )(jnp.asarray([r // GLA_RANK for r in PACKED_SRC_ROWS], jnp.int32), u, w_in_t, w_in_t)


def _run(stages):
    for _ in stages:
        pass


def _pipelined(tasks, n_stages):
    active = []
    for task in list(tasks) + [None] * (n_stages - 1):
        if task is not None:
            active.insert(0, (task, 0))
        advanced = []
        for running, done in active:
            next(running)
            if done + 1 == n_stages:
                _run(running)
                yield
            else:
                advanced.append((running, done + 1))
        active = advanced


def _interleave(major, n_major, minor, n_minor):
    done = 0
    for i in range(n_major):
        while done < n_minor and done * n_major < (i + 1) * n_minor:
            next(minor)
            done += 1
        next(major)
    _run(minor)
    _run(major)


def _silu(x):
    return x / (1.0 + jnp.exp(-x))


def _log_sigmoid(x):
    return jnp.minimum(x, 0.0) - jnp.log(1.0 + jnp.exp(-jnp.abs(x)))


def _alibi_slope(h):
    return math.pow(2.0, -8.0 * (h + 1) / N_Q_HEADS)


HEAD_STAGES = 3
GROUP_ROWS = GQA_GROUP * CHUNK


def _fill_alibi(bias_ref):
    s = lax.broadcasted_iota(jnp.int32, (KEYS, CHUNK), 0)
    t = lax.broadcasted_iota(jnp.int32, (KEYS, CHUNK), 1)
    dist = jnp.abs(WINDOW + t - s).astype(F32)
    for kh in range(N_KV_HEADS):
        for g in range(GQA_GROUP):
            bias_ref[kh, :, g * CHUNK:(g + 1) * CHUNK] = dist * (-_alibi_slope(kh * GQA_GROUP + g))


def _attention_chunk(r, qa_ref, za_ref, kwin, vwin, sinks_ref, layer, bias_ref, key_lo, xa_ref):
    rows = pl.ds(r, CHUNK)
    lane_head = lax.broadcasted_iota(jnp.int32, (1, GROUP_ROWS), 1) // CHUNK
    if key_lo is not None:
        key_ok = lax.broadcasted_iota(jnp.int32, (KEYS, GROUP_ROWS), 0) >= key_lo

    def head(kh):
        k = kwin(pl.ds(kh * HEAD_DIM, HEAD_DIM))
        v = vwin(pl.ds(kh * HEAD_DIM, HEAD_DIM))
        q = jnp.concatenate(
            [qa_ref[rows, pl.ds((kh * GQA_GROUP + g) * HEAD_DIM, HEAD_DIM)] for g in range(GQA_GROUP)],
            axis=0)
        q = q * jnp.asarray(HEAD_DIM ** -0.5, BF16)
        s = _nt_dot(k, q) + bias_ref[kh]
        if key_lo is not None:
            s = jnp.where(key_ok, s, -jnp.inf)
        yield
        sink = jnp.zeros((1, GROUP_ROWS), F32)
        for g in range(GQA_GROUP):
            sink = jnp.where(lane_head == g, sinks_ref[layer, kh * GQA_GROUP + g], sink)
        m = jnp.maximum(jnp.max(s, axis=0, keepdims=True), sink)
        p = jnp.exp(s - m)
        den = jnp.sum(p, axis=0, keepdims=True) + jnp.exp(sink - m)
        p = p.astype(BF16)
        yield
        o_t = lax.dot_general(v, p, (((0,), (0,)), ((), ())), preferred_element_type=F32)
        o = jnp.transpose(o_t / den)
        for g in range(GQA_GROUP):
            h = kh * GQA_GROUP + g
            cols = pl.ds(h * HEAD_DIM, HEAD_DIM)
            z = za_ref[rows, cols].astype(F32)
            xa_ref[rows, cols] = (o[g * CHUNK:(g + 1) * CHUNK] * _silu(z)).astype(BF16)
        yield

    return [head(kh) for kh in range(N_KV_HEADS)]


def _gla_prepare(lr_ref, kg_ref, wfu_ref, bf_ref, k_scr, b_scr):
    n = lr_ref.shape[0]
    la2 = _log2_forget(lr_ref[...], wfu_ref, bf_ref)
    row = lax.broadcasted_iota(jnp.int32, (n, n), 0)
    col = lax.broadcasted_iota(jnp.int32, (n, n), 1)
    tri = jnp.where(row // CHUNK == col // CHUNK, row - col, -1) >= 0
    tri = jnp.where(tri, 1.0, 0.0).astype(BF16)
    la_hi = la2.astype(BF16)
    rem = la2 - la_hi.astype(F32)
    la_mid = rem.astype(BF16)
    la_lo = (rem - la_mid.astype(F32)).astype(BF16)
    b_scr[...] = (jnp.dot(tri, la_hi, preferred_element_type=F32)
                  + jnp.dot(tri, la_mid, preferred_element_type=F32)
                  + jnp.dot(tri, la_lo, preferred_element_type=F32))
    k_scr[...] = kg_ref[...].astype(F32)


def _gla_states(chunk_states, vg_ref, k_scr, b_scr, s0_scr):
    for hh in range(GLA_HEADS):
        kcols = pl.ds(hh * GLA_HK, GLA_HK)
        vcols = pl.ds(hh * GLA_HV, GLA_HV)
        steps = []
        for c in range(len(chunk_states)):
            rows = pl.ds(c * CHUNK, CHUNK)
            b_end = b_scr[c * CHUNK + CHUNK - 1:(c + 1) * CHUNK, kcols]
            k_til = (k_scr[rows, kcols] * jnp.exp2(b_end - b_scr[rows, kcols])).astype(BF16)
            upd = lax.dot_general(k_til, vg_ref[rows, vcols], (((0,), (0,)), ((), ())),
                                  preferred_element_type=F32)
            e_col = jnp.transpose(jnp.broadcast_to(jnp.exp2(b_end), (GLA_HK, GLA_HK)))
            steps.append((jnp.concatenate([e_col, e_col], axis=1), upd))
        s = None
        for c, ((in_ref, out_ref), (decay, upd)) in enumerate(zip(chunk_states, steps)):
            if c == 0 or in_ref is not chunk_states[c - 1][1]:
                s = in_ref[hh]
            s0_scr[c, hh] = s.astype(BF16)
            s = s * decay + upd
            if c + 1 == len(chunk_states) or chunk_states[c + 1][0] is not out_ref:
                out_ref[hh] = s
        yield


def _gla_chunk(c, qg_ref, vg_ref, zg_ref, s0_scr, gnorm, k_scr, b_scr, xg_ref):
    r = c * CHUNK
    rows = pl.ds(r, CHUNK)
    lane = lax.broadcasted_iota(jnp.int32, (SUB, CHUNK), 1)
    row0 = lax.broadcasted_iota(jnp.int32, (SUB, CHUNK), 0)

    def head(hh):
        kcols = pl.ds(hh * GLA_HK, GLA_HK)
        vcols = pl.ds(hh * GLA_HV, GLA_HV)
        q = qg_ref[rows, kcols].astype(F32) * (GLA_HK ** -0.5)
        k = k_scr[rows, kcols]
        v = vg_ref[rows, vcols]
        b = b_scr[rows, kcols]
        blk = lambda x, j: x[j * SUB:(j + 1) * SUB]
        b_row = lambda t: b_scr[r + t:r + t + 1, kcols]
        b_last = [b_row(j * SUB + SUB - 1) for j in range(NSUB)]

        k_hat = jnp.concatenate([blk(k, j) * jnp.exp2(b_last[j] - blk(b, j)) for j in range(NSUB)], axis=0)
        lhs = jnp.concatenate([q[(j + 1) * SUB:] * jnp.exp2(b[(j + 1) * SUB:] - b_last[j])
                               for j in range(NSUB - 1)], axis=0)
        rr = _nt_dot(lhs.astype(BF16), k_hat.astype(BF16))
        o_inter = jnp.dot((q * jnp.exp2(b)).astype(BF16), s0_scr[c, hh], preferred_element_type=F32)
        yield

        score_rows = []
        for i in range(NSUB):
            acc = jnp.zeros((SUB, CHUNK), F32)
            for j in range(i):
                base = sum((NSUB - 1 - jj) * SUB for jj in range(j)) + (i - j - 1) * SUB
                acc = jnp.where(lane // SUB == j, rr[base:base + SUB], acc)
            qi, bi = blk(q, i), blk(b, i)
            for s in range(i * SUB, (i + 1) * SUB):
                prod = (qi * k_scr[r + s:r + s + 1, kcols]) * jnp.exp2(bi - b_row(s))
                acc = jnp.where(lane == s, jnp.sum(prod, axis=1, keepdims=True), acc)
            score_rows.append(jnp.where(lane <= row0 + i * SUB, acc, 0.0))
        scores = jnp.concatenate(score_rows, axis=0).astype(BF16)
        yield

        o = jnp.dot(scores, v, preferred_element_type=F32) + o_inter

        y = _rms(o, gnorm)
        z = zg_ref[rows, vcols].astype(F32)
        xg_ref[rows, vcols] = (y * _silu(z)).astype(BF16)
        yield

    return [head(hh) for hh in range(GLA_HEADS)]


def _log2_forget(lr, wfu_ref, bf_ref):
    x = jnp.dot(lr, wfu_ref[...], preferred_element_type=F32) + bf_ref[...]
    return _log_sigmoid(x) * (LOG2E / GLA_TAU)


def _sigmoid(x):
    return 1.0 / (1.0 + jnp.exp(-x))


def _swap_mixer_outputs(s, xa_buf, xg_buf, xa_prev, xg_prev):
    @pl.when(s == 0)
    def _():
        xa_buf[1] = jnp.zeros(xa_buf.shape[1:], BF16)
        xg_buf[1] = jnp.zeros(xg_buf.shape[1:], BF16)

    xa_prev[...] = xa_buf[1 - s % 2]
    xg_prev[...] = xg_buf[1 - s % 2]
    return xa_buf.at[s % 2], xg_buf.at[s % 2]


N_OUT_STAGES = 3 * (D_MODEL // OUT_COLS) + 1


def _out_stages(xa_ref, xg_ref, ga_ref, gg_ref, h_in_ref, wa_ref, wg_ref, wo_ref, bg_ref, npost_ref, npre_ref,
                merged_scr, z_scr, out_refs):
    bg = bg_ref[...]
    for n in range(D_MODEL // OUT_COLS):
        cols = slice(n * OUT_COLS, (n + 1) * OUT_COLS)
        ya = jnp.dot(xa_ref[...], wa_ref[:, cols], preferred_element_type=F32)
        yield
        yg = jnp.dot(xg_ref[...], wg_ref[:, cols], preferred_element_type=F32)
        gate_a = _sigmoid(ga_ref[:, cols].astype(F32) + bg[:, n * OUT_COLS:(n + 1) * OUT_COLS])
        gate_g = _sigmoid(gg_ref[:, cols].astype(F32)
                          + bg[:, D_MODEL + n * OUT_COLS:D_MODEL + (n + 1) * OUT_COLS])
        merged_scr[:, cols] = (gate_a * ya + gate_g * yg).astype(BF16)
        yield
    for n in range(D_MODEL // OUT_COLS):
        cols = slice(n * OUT_COLS, (n + 1) * OUT_COLS)
        z_scr[:, cols] = jnp.dot(merged_scr[...], wo_ref[:, cols], preferred_element_type=F32)
        yield
    h = h_in_ref[...] + _rms(z_scr[...], npost_ref[...])
    out_refs[0][...] = h
    if npre_ref is not None:
        out_refs[1][...] = _rms(h, npre_ref[...]).astype(BF16)
    yield


def _mix_out_prompt_kernel(layer, n_steps, chunks, *refs):
    last = layer == DEPTH - 1
    n_carried = 3 if layer > 0 else 0
    (qa_ref, za_ref, vg_ref, zg_ref, qg_ref, kg_ref, ka_ref, va_ref, lr_ref,
     wfu_ref, bf_ref, gn_ref, sinks_ref,
     ga_ref, gg_ref, h_in_ref, wa_ref, wg_ref, wo_ref, bg_ref, npost_ref, npre_ref) = refs[:22]
    refs = refs[22 + n_carried:]
    out_refs, refs = refs[:1 if last else 2], refs[1 if last else 2:]
    (st_ref, knew_ref, vnew_ref,
     halo_k, halo_v, state, bias_scr, k_scr, b_scr, s0_scr, xa_buf, xg_buf, xa_prev, xg_prev,
     merged_scr, z_scr) = refs
    s = pl.program_id(0)
    b = s // n_steps
    t = s % n_steps
    assert chunks * CHUNK >= WINDOW
    xa_ref, xg_ref = _swap_mixer_outputs(s, xa_buf, xg_buf, xa_prev, xg_prev)
    out_stages = functools.partial(
        _out_stages, xa_prev, xg_prev, ga_ref, gg_ref, h_in_ref, wa_ref, wg_ref, wo_ref, bg_ref, npost_ref,
        None if last else npre_ref, merged_scr, z_scr, out_refs)

    cur = t % 2
    nxt = 1 - cur

    @pl.when((b == 0) & (t == 0))
    def _():
        _fill_alibi(bias_scr)

    @pl.when(t == 0)
    def _():
        halo_k[0] = jnp.zeros((WINDOW, KV_WIDTH), BF16)
        halo_v[0] = jnp.zeros((WINDOW, KV_WIDTH), BF16)
        state[...] = jnp.zeros(state.shape, F32)

    def window(halo, tile_ref, c, cols):
        lo = c * CHUNK - WINDOW
        if lo >= 0:
            return tile_ref[lo:lo + KEYS, cols]
        return jnp.concatenate([halo[cur, WINDOW + lo:, cols], tile_ref[0:(c + 1) * CHUNK, cols]], axis=0)

    def mix_stages():
        _gla_prepare(lr_ref, kg_ref, wfu_ref, bf_ref, k_scr, b_scr)
        yield
        yield from _gla_states([(state, state)] * chunks, vg_ref, k_scr, b_scr, s0_scr)
        gnorm = gn_ref[...]
        heads = []
        for c in range(chunks):
            key_lo = (WINDOW // CHUNK - (t * chunks + c)) * CHUNK if c < WINDOW // CHUNK else None
            heads += _attention_chunk(c * CHUNK, qa_ref, za_ref, functools.partial(window, halo_k, ka_ref, c),
                                      functools.partial(window, halo_v, va_ref, c), sinks_ref, layer, bias_scr,
                                      key_lo, xa_ref)
            heads += _gla_chunk(c, qg_ref, vg_ref, zg_ref, s0_scr, gnorm, k_scr, b_scr, xg_ref)
        yield from _pipelined(heads, HEAD_STAGES)
        halo_k[nxt] = ka_ref[chunks * CHUNK - WINDOW:, :]
        halo_v[nxt] = va_ref[chunks * CHUNK - WINDOW:, :]

    n_mix = 1 + GLA_HEADS + chunks * (N_KV_HEADS + GLA_HEADS)
    _interleave(mix_stages(), n_mix, out_stages(), N_OUT_STAGES)

    @pl.when(t == n_steps - 1)
    def _():
        st_ref[0] = state[...]
        knew_ref[0] = ka_ref[chunks * CHUNK - WINDOW:, :].astype(F32)
        vnew_ref[0] = va_ref[chunks * CHUNK - WINDOW:, :].astype(F32)


def _mix_out_sample_kernel(layer, n_seqs, *refs):
    last = layer == DEPTH - 1
    n_untouched = (3 if layer > 0 else 0) + (0 if last else 2)
    (qa_ref, za_ref, vg_ref, zg_ref, qg_ref, kg_ref, ka_ref, va_ref, lr_ref,
     wfu_ref, bf_ref, gn_ref, sinks_ref, ck_ref, cv_ref, s0_ref,
     ga_ref, gg_ref, h_in_ref, wa_ref, wg_ref, wo_ref, bg_ref, npost_ref, npre_ref) = refs[:25]
    refs = refs[25 + n_untouched:]
    out_refs, refs = refs[:1 if last else 2], refs[1 if last else 2:]
    (st_ref, knew_ref, vnew_ref,
     bias_scr, k_scr, b_scr, s0_scr, xa_buf, xg_buf, xa_prev, xg_prev, merged_scr, z_scr) = refs
    s = pl.program_id(0)
    xa_ref, xg_ref = _swap_mixer_outputs(s, xa_buf, xg_buf, xa_prev, xg_prev)
    out_stages = _out_stages(xa_prev, xg_prev, ga_ref, gg_ref, h_in_ref, wa_ref, wg_ref, wo_ref, bg_ref,
                             npost_ref, None if last else npre_ref, merged_scr, z_scr, out_refs)

    @pl.when(s == 0)
    def _():
        _fill_alibi(bias_scr)

    def window(cache_ref, new_ref, g, cols):
        return jnp.concatenate([cache_ref[g, :, cols].astype(BF16), new_ref[g * CHUNK:(g + 1) * CHUNK, cols]], axis=0)

    def mix_stages():
        _gla_prepare(lr_ref, kg_ref, wfu_ref, bf_ref, k_scr, b_scr)
        yield
        yield from _gla_states([(s0_ref.at[g], st_ref.at[g]) for g in range(n_seqs)], vg_ref, k_scr, b_scr, s0_scr)
        gnorm = gn_ref[...]
        heads = []
        for g in range(n_seqs):
            r = g * CHUNK
            knew_ref[g, 0:WINDOW - CHUNK, :] = ck_ref[g, CHUNK:, :]
            knew_ref[g, WINDOW - CHUNK:, :] = ka_ref[r:r + CHUNK, :].astype(F32)
            vnew_ref[g, 0:WINDOW - CHUNK, :] = cv_ref[g, CHUNK:, :]
            vnew_ref[g, WINDOW - CHUNK:, :] = va_ref[r:r + CHUNK, :].astype(F32)
            heads += _attention_chunk(r, qa_ref, za_ref, functools.partial(window, ck_ref, ka_ref, g),
                                      functools.partial(window, cv_ref, va_ref, g), sinks_ref, layer, bias_scr,
                                      None, xa_ref)
            heads += _gla_chunk(g, qg_ref, vg_ref, zg_ref, s0_scr, gnorm, k_scr, b_scr, xg_ref)
        yield from _pipelined(heads, HEAD_STAGES)

    n_mix = 1 + GLA_HEADS + n_seqs * (N_KV_HEADS + GLA_HEADS)
    _interleave(mix_stages(), n_mix, out_stages, N_OUT_STAGES)


def _section_specs(rows, row_index):
    def spec(width, col):
        return pl.BlockSpec((rows, width), functools.partial(
            lambda cb, *g: (row_index(*g), cb), col // width))
    return [spec(ATTN_WIDTH, COL_QA), spec(ATTN_WIDTH, COL_ZA), spec(GLA_DV, COL_VG), spec(GLA_DV, COL_ZG),
            spec(GLA_DK, COL_QG), spec(GLA_DK, COL_KG), spec(KV_WIDTH, COL_KA), spec(KV_WIDTH, COL_VA),
            spec(LR_PAD, 0)]


def _layer_weight_specs(layer, ngrid):
    zeros = (0,) * 2
    return [pl.BlockSpec((None, LR_PAD, GLA_DK), lambda *g: (layer,) + zeros),
            pl.BlockSpec((None, 1, GLA_DK), lambda *g: (layer,) + zeros),
            pl.BlockSpec((None, 1, GLA_HV), lambda *g: (layer,) + zeros),
            pl.BlockSpec(memory_space=pltpu.SMEM)]


def _stacked_outputs(layer, batch, seqs_per_step, batch_block, n_inputs, first_out, carried):
    lead = lambda *g: (layer, batch_block(*g))
    nb = seqs_per_step
    specs = [pl.BlockSpec((None, nb, GLA_HEADS, GLA_HK, GLA_HV), lambda *g: lead(*g) + (0, 0, 0)),
             pl.BlockSpec((None, nb, WINDOW, KV_WIDTH), lambda *g: lead(*g) + (0, 0)),
             pl.BlockSpec((None, nb, WINDOW, KV_WIDTH), lambda *g: lead(*g) + (0, 0))]
    shapes = [jax.ShapeDtypeStruct((DEPTH, batch, GLA_HEADS, GLA_HK, GLA_HV), F32),
              jax.ShapeDtypeStruct((DEPTH, batch, WINDOW, KV_WIDTH), F32),
              jax.ShapeDtypeStruct((DEPTH, batch, WINDOW, KV_WIDTH), F32)]
    if carried is None:
        return specs, shapes, [], {}
    alias_specs = [pl.BlockSpec(memory_space=pl.ANY)] * 3
    aliases = {n_inputs + k: first_out + k for k in range(3)}
    return specs, shapes, alias_specs, aliases


def _out_weight_specs(layer):
    wspec = lambda k: pl.BlockSpec((None, k, D_MODEL), lambda *g: (layer, 0, 0), pipeline_mode=pl.Buffered(1))
    vspec = lambda n, l: pl.BlockSpec((None, 1, n), lambda *g: (l, 0, 0))
    return [wspec(ATTN_WIDTH), wspec(GLA_DV), wspec(D_MODEL),
            vspec(2 * D_MODEL, layer), vspec(D_MODEL, layer), vspec(D_MODEL, min(layer + 1, DEPTH - 1))]


def _mix_out_prompt(p, p_lr, h_in, n_total, wfu, bfg, gn, sinks, wa, wg, wo, bg, npost, npre,
                    layer, batch, seq, carried):
    last = layer == DEPTH - 1
    chunks = PROMPT_CHUNKS_PER_STEP
    rows = chunks * CHUNK
    assert rows == TM_OUT
    n_steps = seq // rows
    n_tiles = batch * n_steps
    mix_tile = lambda s: jnp.minimum(s, n_tiles - 1)
    out_tile = lambda s: jnp.maximum(s - 1, 0)
    tile = lambda col: pl.BlockSpec((rows, D_MODEL), lambda s: (out_tile(s), col))
    in_specs = (_section_specs(rows, mix_tile) + _layer_weight_specs(layer, 1)
                + [tile(COL_GA // D_MODEL), tile(COL_GG // D_MODEL), tile(0)] + _out_weight_specs(layer))
    n_main = 1 if last else 2
    st_specs, st_shapes, alias_specs, aliases = _stacked_outputs(
        layer, batch, 1, lambda s: jnp.minimum(s // n_steps, batch - 1), len(in_specs), n_main, carried)
    if last:
        out_shape = [jax.ShapeDtypeStruct((n_tiles * rows, D_MODEL), F32)]
    else:
        out_shape = [jax.ShapeDtypeStruct((n_total, D_MODEL), F32), jax.ShapeDtypeStruct((n_total, D_MODEL), BF16)]
    outs = pl.pallas_call(
        functools.partial(_mix_out_prompt_kernel, layer, n_steps, chunks),
        grid=(n_tiles + 1,),
        in_specs=in_specs + alias_specs,
        out_specs=[tile(0)] * n_main + st_specs,
        out_shape=out_shape + st_shapes,
        input_output_aliases=aliases,
        scratch_shapes=[pltpu.VMEM((2, WINDOW, KV_WIDTH), BF16),
                        pltpu.VMEM((2, WINDOW, KV_WIDTH), BF16),
                        pltpu.VMEM((GLA_HEADS, GLA_HK, GLA_HV), F32),
                        pltpu.VMEM((N_KV_HEADS, KEYS, GROUP_ROWS), F32),
                        pltpu.VMEM((rows, GLA_DK), F32),
                        pltpu.VMEM((rows, GLA_DK), F32),
                        pltpu.VMEM((rows // CHUNK, GLA_HEADS, GLA_HK, GLA_HV), BF16),
                        pltpu.VMEM((2, rows, ATTN_WIDTH), BF16),
                        pltpu.VMEM((2, rows, GLA_DV), BF16),
                        pltpu.VMEM((rows, ATTN_WIDTH), BF16),
                        pltpu.VMEM((rows, GLA_DV), BF16),
                        pltpu.VMEM((rows, D_MODEL), BF16),
                        pltpu.VMEM((rows, D_MODEL), F32)],
        compiler_params=pltpu.CompilerParams(
            dimension_semantics=("arbitrary",), vmem_limit_bytes=VMEM_LIMIT),
        name="mix_out_prompt",
    )(*([p] * 8), p_lr, wfu, bfg, gn, sinks, p, p, h_in, wa, wg, wo, bg, npost, npre, *(carried or ()))
    return tuple(outs[:n_main]), tuple(outs[n_main:])


def _mix_out_sample(p, p_lr, h_in, row0_h, new_h_u, wfu, bfg, gn, sinks, cache_k, cache_v, state,
                    wa, wg, wo, bg, npost, npre, layer, batch, carried):
    last = layer == DEPTH - 1
    nb = SAMPLE_SEQS_PER_STEP
    rows = nb * CHUNK
    n_tiles = batch // nb
    row0 = p.shape[0] - batch * CHUNK
    assert row0 % rows == 0 and row0_h % rows == 0 and batch % nb == 0
    blk0, blk0_h = row0 // rows, row0_h // rows
    mix_tile = lambda s: jnp.minimum(s, n_tiles - 1)
    out_tile = lambda s: jnp.maximum(s - 1, 0)
    tile = lambda base, col: pl.BlockSpec((rows, D_MODEL), lambda s: (base + out_tile(s), col))
    cache_spec = pl.BlockSpec((None, nb, WINDOW, KV_WIDTH), lambda s: (layer, mix_tile(s), 0, 0))
    state_spec = pl.BlockSpec((None, nb, GLA_HEADS, GLA_HK, GLA_HV), lambda s: (layer, mix_tile(s), 0, 0, 0))
    in_specs = (_section_specs(rows, lambda s: blk0 + mix_tile(s)) + _layer_weight_specs(layer, 1)
                + [cache_spec, cache_spec, state_spec]
                + [tile(blk0, COL_GA // D_MODEL), tile(blk0, COL_GG // D_MODEL), tile(blk0_h, 0)]
                + _out_weight_specs(layer))
    n_main = 1 if last else 2
    st_specs, st_shapes, alias_specs, aliases = _stacked_outputs(
        layer, batch, nb, mix_tile, len(in_specs), n_main, carried)
    if last:
        extra, out_specs = (), [tile(0, 0)]
        out_shape = [jax.ShapeDtypeStruct((batch * CHUNK, D_MODEL), F32)]
    else:
        extra, out_specs = tuple(new_h_u), [tile(blk0, 0), tile(blk0, 0)]
        out_shape = [jax.ShapeDtypeStruct(a.shape, a.dtype) for a in new_h_u]
        first_extra = len(in_specs) + len(alias_specs)
        aliases = {**aliases, first_extra: 0, first_extra + 1: 1}
    outs = pl.pallas_call(
        functools.partial(_mix_out_sample_kernel, layer, nb),
        grid=(n_tiles + 1,),
        in_specs=in_specs + alias_specs + [pl.BlockSpec(memory_space=pl.ANY)] * len(extra),
        out_specs=out_specs + st_specs,
        out_shape=out_shape + st_shapes,
        input_output_aliases=aliases,
        scratch_shapes=[pltpu.VMEM((N_KV_HEADS, KEYS, GROUP_ROWS), F32),
                        pltpu.VMEM((rows, GLA_DK), F32),
                        pltpu.VMEM((rows, GLA_DK), F32),
                        pltpu.VMEM((rows // CHUNK, GLA_HEADS, GLA_HK, GLA_HV), BF16),
                        pltpu.VMEM((2, rows, ATTN_WIDTH), BF16),
                        pltpu.VMEM((2, rows, GLA_DV), BF16),
                        pltpu.VMEM((rows, ATTN_WIDTH), BF16),
                        pltpu.VMEM((rows, GLA_DV), BF16),
                        pltpu.VMEM((rows, D_MODEL), BF16),
                        pltpu.VMEM((rows, D_MODEL), F32)],
        compiler_params=pltpu.CompilerParams(
            dimension_semantics=("arbitrary",), vmem_limit_bytes=VMEM_LIMIT),
        name="mix_out_sample",
    )(*([p] * 8), p_lr, wfu, bfg, gn, sinks, cache_k, cache_v, state, p, p, h_in, wa, wg, wo, bg, npost, npre,
      *(carried or ()), *extra)
    return tuple(outs[:n_main]), tuple(outs[n_main:])


def kernel(x_prompt, x_sample, cache_k, cache_v, state_gla, norm_pre, norm_post, w_in, b_gate,
           attn_sinks, w_forget_up, b_forget, gla_norm, w_branch_attn, w_branch_gla, w_out):
    batch, seq, _ = x_prompt.shape
    dec_batch, dec_seq, _ = x_sample.shape
    assert dec_seq == CHUNK and seq % (PROMPT_CHUNKS_PER_STEP * CHUNK) == 0
    assert cache_k.shape[2] == WINDOW
    n_prompt = batch * seq
    n_sample = dec_batch * dec_seq

    w_in_t = jnp.swapaxes(w_in, 1, 2)
    wa =w_branch_attn.astype(BF16)
    wg = w_branch_gla.astype(BF16)
    wo = w_out.astype(BF16)
    wfu = jnp.pad(w_forget_up, ((0, 0), (0, LR_PAD - GLA_RANK), (0, 0))).astype(BF16)
    bfg = b_forget.reshape(DEPTH, 1, GLA_DK)
    gn = gla_norm.reshape(DEPTH, 1, GLA_HV)
    bg = b_gate.reshape(DEPTH, 1, 2 * D_MODEL)
    npre = norm_pre.reshape(DEPTH, 1, D_MODEL)
    npost = norm_post.reshape(DEPTH, 1, D_MODEL)
    ck = cache_k.reshape(DEPTH, dec_batch, WINDOW, KV_WIDTH)
    cv = cache_v.reshape(DEPTH, dec_batch, WINDOW, KV_WIDTH)

    xp = x_prompt.reshape(n_prompt, D_MODEL)
    xs = x_sample.reshape(n_sample, D_MODEL)
    u = _prenorm(xp, xs, npre[0])
    h_prompt, h_sample, sample_row0 = xp, xs, 0

    stacked_p = stacked_s = None
    for l in range(DEPTH):
        p, p_lr = _project(u, w_in_t, l)
        out_p, stacked_p = _mix_out_prompt(p, p_lr, h_prompt, n_prompt + n_sample, wfu, bfg, gn, attn_sinks,
                                           wa, wg, wo, bg, npost, npre, l, batch, seq, stacked_p)
        out_s, stacked_s = _mix_out_sample(p, p_lr, h_sample, sample_row0, out_p, wfu, bfg, gn, attn_sinks,
                                           ck, cv, state_gla, wa, wg, wo, bg, npost, npre, l, dec_batch, stacked_s)
        if l < DEPTH - 1:
            h_prompt = h_sample = out_s[0]
            u = out_s[1]
            sample_row0 = n_prompt

    y_prompt = out_p[0].reshape(batch, seq, D_MODEL)
    y_sample = out_s[0].reshape(dec_batch, dec_seq, D_MODEL)
    heads = lambda a: a.reshape(a.shape[:3] + (N_KV_HEADS, HEAD_DIM))
    return (y_prompt, y_sample, heads(stacked_p[1]), heads(stacked_p[2]), stacked_p[0],
            heads(stacked_s[1]), heads(stacked_s[2]), stacked_s[0])
```

```python
import functools
import math

import jax
import jax.numpy as jnp
from jax import lax
from jax.experimental import pallas as pl
from jax.experimental.pallas import tpu as pltpu

F32 = jnp.float32
BF16 = jnp.bfloat16

D_MODEL = 2048
DEPTH = 4
CHUNK = 64
WINDOW = 128
HEAD_DIM = 64
N_Q_HEADS = 16
N_KV_HEADS = 4
GQA_GROUP = N_Q_HEADS // N_KV_HEADS
ATTN_WIDTH = N_Q_HEADS * HEAD_DIM
KV_WIDTH = N_KV_HEADS * HEAD_DIM
GLA_HEADS = 4
GLA_DK = 512
GLA_DV = 1024
GLA_HK = GLA_DK // GLA_HEADS
GLA_HV = GLA_DV // GLA_HEADS
GLA_RANK = 16
GLA_TAU = 16.0
NORM_EPS = 1e-6
LOG2E = math.log2(math.e)
IN_SIZES = (ATTN_WIDTH, KV_WIDTH, KV_WIDTH, ATTN_WIDTH,
            GLA_DK, GLA_DK, GLA_DV, GLA_DV, GLA_RANK, D_MODEL, D_MODEL)

KEYS = WINDOW + CHUNK
SUB = 8
NSUB = CHUNK // SUB

TM_PROJ = 3072
TM_PROJ_SUB = 512
TN_PROJ = 512
TM_OUT = 256
OUT_COLS = 256
TM_NORM = 512
PROMPT_CHUNKS_PER_STEP = 4
SAMPLE_SEQS_PER_STEP = 2

VMEM_LIMIT = 56 * 1024 * 1024

SRC_COLS = {}
_acc = 0
for _name, _size in zip(("qa", "ka", "va", "za", "qg", "kg", "vg", "zg", "lr", "ga", "gg"), IN_SIZES):
    SRC_COLS[_name] = (_acc, _size)
    _acc += _size
IN_COLS = _acc

PACKED_ORDER = ("ga", "gg", "qa", "za", "vg", "zg", "qg", "kg", "ka", "va")
PACKED_COL = {}
PACKED_SRC_ROWS = []
_acc = 0
for _name in PACKED_ORDER:
    _start, _size = SRC_COLS[_name]
    PACKED_COL[_name] = _acc
    if _acc % TN_PROJ == 0:
        PACKED_SRC_ROWS += [_start + k for k in range(0, max(_size, TN_PROJ), TN_PROJ)]
    _acc += _size
PACKED_COLS = _acc
COL_GA, COL_GG, COL_QA, COL_ZA, COL_VG, COL_ZG, COL_QG, COL_KG, COL_KA, COL_VA = (
    PACKED_COL[_n] for _n in PACKED_ORDER)
LR_PAD = 128
assert SRC_COLS["va"][0] == SRC_COLS["ka"][0] + KV_WIDTH and 2 * KV_WIDTH == TN_PROJ
assert len(PACKED_SRC_ROWS) * TN_PROJ == PACKED_COLS
assert all(_r % GLA_RANK == 0 for _r in PACKED_SRC_ROWS)


def _rms(x, g):
    return x * lax.rsqrt(jnp.mean(x * x, axis=-1, keepdims=True) + NORM_EPS) * g


def _two_group_index_maps(n_prompt_tiles):
    p_idx = lambda i: (jnp.minimum(i, n_prompt_tiles - 1), 0)
    s_idx = lambda i: (jnp.maximum(i - n_prompt_tiles, 0), 0)
    return p_idx, s_idx


def _prenorm_kernel(n_prompt_tiles, xp_ref, xs_ref, g_ref, u_ref):
    x = jnp.where(pl.program_id(0) < n_prompt_tiles, xp_ref[...], xs_ref[...])
    u_ref[...] = _rms(x, g_ref[...]).astype(BF16)


def _prenorm(x_prompt, x_sample, g):
    n_prompt_tiles = x_prompt.shape[0] // TM_NORM
    n_tiles = n_prompt_tiles + x_sample.shape[0] // TM_NORM
    p_idx, s_idx = _two_group_index_maps(n_prompt_tiles)
    return pl.pallas_call(
        functools.partial(_prenorm_kernel, n_prompt_tiles),
        grid=(n_tiles,),
        in_specs=[pl.BlockSpec((TM_NORM, D_MODEL), p_idx),
                  pl.BlockSpec((TM_NORM, D_MODEL), s_idx),
                  pl.BlockSpec((1, D_MODEL), lambda i: (0, 0))],
        out_specs=pl.BlockSpec((TM_NORM, D_MODEL), lambda i: (i, 0)),
        out_shape=jax.ShapeDtypeStruct((n_tiles * TM_NORM, D_MODEL), BF16),
        compiler_params=pltpu.CompilerParams(dimension_semantics=("arbitrary",)),
        name="prenorm",
    )(x_prompt, x_sample, g)


def _nt_dot(a, b):
    return lax.dot_general(a, b, (((1,), (1,)), ((), ())), preferred_element_type=F32)


def _proj_kernel(src_rows_ref, u_ref, w_ref, wlr_ref, o_ref, olr_ref):
    del src_rows_ref
    w = w_ref[...].astype(BF16)
    for m in range(TM_PROJ // TM_PROJ_SUB):
        rows = slice(m * TM_PROJ_SUB, (m + 1) * TM_PROJ_SUB)
        o_ref[rows, :] = _nt_dot(u_ref[rows, :], w).astype(BF16)

    @pl.when(pl.program_id(1) == 0)
    def _():
        wlr = wlr_ref[...].astype(BF16)
        for m in range(TM_PROJ // TM_PROJ_SUB):
            rows = slice(m * TM_PROJ_SUB, (m + 1) * TM_PROJ_SUB)
            olr_ref[rows, :] = _nt_dot(u_ref[rows, :], wlr).astype(BF16)


def _project(u, w_in_t, layer):
    t = u.shape[0]
    lr_row = SRC_COLS["lr"][0]
    wspec = lambda rows, index_map: pl.BlockSpec((None, pl.Element(rows), pl.Element(D_MODEL)), index_map)
    grid_spec = pltpu.PrefetchScalarGridSpec(
        num_scalar_prefetch=1,
        grid=(t // TM_PROJ, PACKED_COLS // TN_PROJ),
        in_specs=[pl.BlockSpec((TM_PROJ, D_MODEL), lambda i, j, src: (i, 0)),
                  wspec(TN_PROJ, lambda i, j, src: (layer, src[j] * GLA_RANK, 0)),
                  wspec(LR_PAD, lambda i, j, src: (layer, lr_row, 0))],
        out_specs=[pl.BlockSpec((TM_PROJ, TN_PROJ), lambda i, j, src: (i, j)),
                   pl.BlockSpec((TM_PROJ, LR_PAD), lambda i, j, src: (i, 0))])
    return pl.pallas_call(
        _proj_kernel,
        grid_spec=grid_spec,
        out_shape=[jax.ShapeDtypeStruct((t, PACKED_COLS), BF16),
                   jax.ShapeDtypeStruct((t, LR_PAD), BF16)],
        compiler_params=pltpu.CompilerParams(
            dimension_semantics=("arbitrary", "arbitrary"), vmem_limit_bytes=VMEM_LIMIT),
        name="proj",
    )(jnp.asarray([r // GLA_RANK for r in PACKED_SRC_ROWS], jnp.int32), u, w_in_t, w_in_t)


def _run(stages):
    for _ in stages:
        pass


def _pipelined(tasks, n_stages):
    active = []
    for task in list(tasks) + [None] * (n_stages - 1):
        if task is not None:
            active.insert(0, (task, 0))
        advanced = []
        for running, done in active:
            next(running)
            if done + 1 == n_stages:
                _run(running)
                yield
            else:
                advanced.append((running, done + 1))
        active = advanced


def _interleave(major, n_major, minor, n_minor):
    done = 0
    for i in range(n_major):
        while done < n_minor and done * n_major < (i + 1) * n_minor:
            next(minor)
            done += 1
        next(major)
    _run(minor)
    _run(major)


def _silu(x):
    return x / (1.0 + jnp.exp(-x))


def _log_sigmoid(x):
    return jnp.minimum(x, 0.0) - jnp.log(1.0 + jnp.exp(-jnp.abs(x)))


def _alibi_slope(h):
    return math.pow(2.0, -8.0 * (h + 1) / N_Q_HEADS)


HEAD_STAGES = 2
GROUP_ROWS = GQA_GROUP * CHUNK


def _fill_alibi(bias_ref):
    s = lax.broadcasted_iota(jnp.int32, (KEYS, CHUNK), 0)
    t = lax.broadcasted_iota(jnp.int32, (KEYS, CHUNK), 1)
    dist = jnp.abs(WINDOW + t - s).astype(F32)
    for kh in range(N_KV_HEADS):
        for g in range(GQA_GROUP):
            bias_ref[kh, :, g * CHUNK:(g + 1) * CHUNK] = dist * (-_alibi_slope(kh * GQA_GROUP + g))


def _attention_chunk(r, qa_ref, za_ref, kwin, vwin, sinks_ref, layer, bias_ref, key_lo, xa_ref):
    rows = pl.ds(r, CHUNK)
    lane_head = lax.broadcasted_iota(jnp.int32, (1, GROUP_ROWS), 1) // CHUNK
    if key_lo is not None:
        key_ok = lax.broadcasted_iota(jnp.int32, (KEYS, GROUP_ROWS), 0) >= key_lo

    def head(kh):
        k = kwin(pl.ds(kh * HEAD_DIM, HEAD_DIM))
        v = vwin(pl.ds(kh * HEAD_DIM, HEAD_DIM))
        q = jnp.concatenate(
            [qa_ref[rows, pl.ds((kh * GQA_GROUP + g) * HEAD_DIM, HEAD_DIM)] for g in range(GQA_GROUP)],
            axis=0)
        q = q * jnp.asarray(HEAD_DIM ** -0.5, BF16)
        s = _nt_dot(k, q) + bias_ref[kh]
        if key_lo is not None:
            s = jnp.where(key_ok, s, -jnp.inf)
        sink = jnp.zeros((1, GROUP_ROWS), F32)
        for g in range(GQA_GROUP):
            sink = jnp.where(lane_head == g, sinks_ref[layer, kh * GQA_GROUP + g], sink)
        m = jnp.maximum(jnp.max(s, axis=0, keepdims=True), sink)
        p = jnp.exp(s - m)
        den = jnp.sum(p, axis=0, keepdims=True) + jnp.exp(sink - m)
        p = p.astype(BF16)
        yield
        o_t = lax.dot_general(v, p, (((0,), (0,)), ((), ())), preferred_element_type=F32)
        o = jnp.transpose(o_t / den)
        for g in range(GQA_GROUP):
            h = kh * GQA_GROUP + g
            cols = pl.ds(h * HEAD_DIM, HEAD_DIM)
            z = za_ref[rows, cols].astype(F32)
            xa_ref[rows, cols] = (o[g * CHUNK:(g + 1) * CHUNK] * _silu(z)).astype(BF16)
        yield

    return [head(kh) for kh in range(N_KV_HEADS)]


def _gla_prepare(lr_ref, kg_ref, wfu_ref, bf_ref, k_scr, b_scr):
    n = lr_ref.shape[0]
    la2 = _log2_forget(lr_ref[...], wfu_ref, bf_ref)
    row = lax.broadcasted_iota(jnp.int32, (n, n), 0)
    col = lax.broadcasted_iota(jnp.int32, (n, n), 1)
    tri = jnp.where(row // CHUNK == col // CHUNK, row - col, -1) >= 0
    tri = jnp.where(tri, 1.0, 0.0).astype(BF16)
    la_hi = la2.astype(BF16)
    rem = la2 - la_hi.astype(F32)
    la_mid = rem.astype(BF16)
    la_lo = (rem - la_mid.astype(F32)).astype(BF16)
    b_scr[...] = (jnp.dot(tri, la_hi, preferred_element_type=F32)
                  + jnp.dot(tri, la_mid, preferred_element_type=F32)
                  + jnp.dot(tri, la_lo, preferred_element_type=F32))
    k_scr[...] = kg_ref[...].astype(F32)


def _gla_states(chunk_states, vg_ref, k_scr, b_scr, s0_scr):
    for hh in range(GLA_HEADS):
        kcols = pl.ds(hh * GLA_HK, GLA_HK)
        vcols = pl.ds(hh * GLA_HV, GLA_HV)
        steps = []
        for c in range(len(chunk_states)):
            rows = pl.ds(c * CHUNK, CHUNK)
            b_end = b_scr[c * CHUNK + CHUNK - 1:(c + 1) * CHUNK, kcols]
            k_til = (k_scr[rows, kcols] * jnp.exp2(b_end - b_scr[rows, kcols])).astype(BF16)
            upd = lax.dot_general(k_til, vg_ref[rows, vcols], (((0,), (0,)), ((), ())),
                                  preferred_element_type=F32)
            e_col = jnp.transpose(jnp.broadcast_to(jnp.exp2(b_end), (GLA_HK, GLA_HK)))
            steps.append((jnp.concatenate([e_col, e_col], axis=1), upd))
        s = None
        for c, ((in_ref, out_ref), (decay, upd)) in enumerate(zip(chunk_states, steps)):
            if c == 0 or in_ref is not chunk_states[c - 1][1]:
                s = in_ref[hh]
            s0_scr[c, hh] = s.astype(BF16)
            s = s * decay + upd
            if c + 1 == len(chunk_states) or chunk_states[c + 1][0] is not out_ref:
                out_ref[hh] = s
        yield


def _gla_chunk(c, qg_ref, vg_ref, zg_ref, s0_scr, gnorm, k_scr, b_scr, xg_ref):
    r = c * CHUNK
    rows = pl.ds(r, CHUNK)
    lane = lax.broadcasted_iota(jnp.int32, (SUB, CHUNK), 1)
    row0 = lax.broadcasted_iota(jnp.int32, (SUB, CHUNK), 0)

    def head(hh):
        kcols = pl.ds(hh * GLA_HK, GLA_HK)
        vcols = pl.ds(hh * GLA_HV, GLA_HV)
        q = qg_ref[rows, kcols].astype(F32) * (GLA_HK ** -0.5)
        k = k_scr[rows, kcols]
        v = vg_ref[rows, vcols]
        b = b_scr[rows, kcols]
        blk = lambda x, j: x[j * SUB:(j + 1) * SUB]
        b_row = lambda t: b_scr[r + t:r + t + 1, kcols]
        b_last = [b_row(j * SUB + SUB - 1) for j in range(NSUB)]

        k_hat = jnp.concatenate([blk(k, j) * jnp.exp2(b_last[j] - blk(b, j)) for j in range(NSUB)], axis=0)
        lhs = jnp.concatenate([q[(j + 1) * SUB:] * jnp.exp2(b[(j + 1) * SUB:] - b_last[j])
                               for j in range(NSUB - 1)], axis=0)
        rr = _nt_dot(lhs.astype(BF16), k_hat.astype(BF16))
        o_inter = jnp.dot((q * jnp.exp2(b)).astype(BF16), s0_scr[c, hh], preferred_element_type=F32)
        yield

        score_rows = []
        for i in range(NSUB):
            acc = jnp.zeros((SUB, CHUNK), F32)
            for j in range(i):
                base = sum((NSUB - 1 - jj) * SUB for jj in range(j)) + (i - j - 1) * SUB
                acc = jnp.where(lane // SUB == j, rr[base:base + SUB], acc)
            qi, bi = blk(q, i), blk(b, i)
            for s in range(i * SUB, (i + 1) * SUB):
                prod = (qi * k_scr[r + s:r + s + 1, kcols]) * jnp.exp2(bi - b_row(s))
                acc = jnp.where(lane == s, jnp.sum(prod, axis=1, keepdims=True), acc)
            score_rows.append(jnp.where(lane <= row0 + i * SUB, acc, 0.0))
        scores = jnp.concatenate(score_rows, axis=0).astype(BF16)

        o = jnp.dot(scores, v, preferred_element_type=F32) + o_inter

        y = _rms(o, gnorm)
        z = zg_ref[rows, vcols].astype(F32)
        xg_ref[rows, vcols] = (y * _silu(z)).astype(BF16)
        yield

    return [head(hh) for hh in range(GLA_HEADS)]


def _log2_forget(lr, wfu_ref, bf_ref):
    x = jnp.dot(lr, wfu_ref[...], preferred_element_type=F32) + bf_ref[...]
    return _log_sigmoid(x) * (LOG2E / GLA_TAU)


def _sigmoid(x):
    return 1.0 / (1.0 + jnp.exp(-x))


def _swap_mixer_outputs(s, xa_buf, xg_buf):
    @pl.when(s == 0)
    def _():
        xa_buf[1] = jnp.zeros(xa_buf.shape[1:], BF16)
        xg_buf[1] = jnp.zeros(xg_buf.shape[1:], BF16)

    return xa_buf.at[s % 2], xg_buf.at[s % 2], xa_buf[1 - s % 2], xg_buf[1 - s % 2]


N_OUT_STAGES = 3 * (D_MODEL // OUT_COLS) + 1


def _out_stages(xa, xg, ga_ref, gg_ref, h_in_ref, wa_ref, wg_ref, wo_ref, bg_ref, npost_ref, npre_ref,
                merged_scr, z_scr, out_refs):
    bg = bg_ref[...]
    for n in range(D_MODEL // OUT_COLS):
        cols = slice(n * OUT_COLS, (n + 1) * OUT_COLS)
        ya = jnp.dot(xa, wa_ref[:, cols], preferred_element_type=F32)
        yield
        yg = jnp.dot(xg, wg_ref[:, cols], preferred_element_type=F32)
        gate_a = _sigmoid(ga_ref[:, cols].astype(F32) + bg[:, n * OUT_COLS:(n + 1) * OUT_COLS])
        gate_g = _sigmoid(gg_ref[:, cols].astype(F32)
                          + bg[:, D_MODEL + n * OUT_COLS:D_MODEL + (n + 1) * OUT_COLS])
        merged_scr[:, cols] = (gate_a * ya + gate_g * yg).astype(BF16)
        yield
    for n in range(D_MODEL // OUT_COLS):
        cols = slice(n * OUT_COLS, (n + 1) * OUT_COLS)
        z_scr[:, cols] = jnp.dot(merged_scr[...], wo_ref[:, cols], preferred_element_type=F32)
        yield
    h = h_in_ref[...] + _rms(z_scr[...], npost_ref[...])
    out_refs[0][...] = h
    if npre_ref is not None:
        out_refs[1][...] = _rms(h, npre_ref[...]).astype(BF16)
    yield


def _mix_out_prompt_kernel(layer, n_steps, chunks, *refs):
    last = layer == DEPTH - 1
    n_carried = 3 if layer > 0 else 0
    (qa_ref, za_ref, vg_ref, zg_ref, qg_ref, kg_ref, ka_ref, va_ref, lr_ref,
     wfu_ref, bf_ref, gn_ref, sinks_ref,
     ga_ref, gg_ref, h_in_ref, wa_ref, wg_ref, wo_ref, bg_ref, npost_ref, npre_ref) = refs[:22]
    refs = refs[22 + n_carried:]
    out_refs, refs = refs[:1 if last else 2], refs[1 if last else 2:]
    (st_ref, knew_ref, vnew_ref,
     halo_k, halo_v, state, bias_scr, k_scr, b_scr, s0_scr, xa_buf, xg_buf, merged_scr, z_scr) = refs
    s = pl.program_id(0)
    b = s // n_steps
    t = s % n_steps
    assert chunks * CHUNK >= WINDOW
    xa_ref, xg_ref, xa_prev, xg_prev = _swap_mixer_outputs(s, xa_buf, xg_buf)
    out_stages = functools.partial(
        _out_stages, xa_prev, xg_prev, ga_ref, gg_ref, h_in_ref, wa_ref, wg_ref, wo_ref, bg_ref, npost_ref,
        None if last else npre_ref, merged_scr, z_scr, out_refs)

    cur = t % 2
    nxt = 1 - cur

    @pl.when((b == 0) & (t == 0))
    def _():
        _fill_alibi(bias_scr)

    @pl.when(t == 0)
    def _():
        halo_k[0] = jnp.zeros((WINDOW, KV_WIDTH), BF16)
        halo_v[0] = jnp.zeros((WINDOW, KV_WIDTH), BF16)
        state[...] = jnp.zeros(state.shape, F32)

    def window(halo, tile_ref, c, cols):
        lo = c * CHUNK - WINDOW
        if lo >= 0:
            return tile_ref[lo:lo + KEYS, cols]
        return jnp.concatenate([halo[cur, WINDOW + lo:, cols], tile_ref[0:(c + 1) * CHUNK, cols]], axis=0)

    def mix_stages():
        _gla_prepare(lr_ref, kg_ref, wfu_ref, bf_ref, k_scr, b_scr)
        yield
        yield from _gla_states([(state, state)] * chunks, vg_ref, k_scr, b_scr, s0_scr)
        gnorm = gn_ref[...]
        heads = []
        for c in range(chunks):
            key_lo = (WINDOW // CHUNK - (t * chunks + c)) * CHUNK if c < WINDOW // CHUNK else None
            heads += _attention_chunk(c * CHUNK, qa_ref, za_ref, functools.partial(window, halo_k, ka_ref, c),
                                      functools.partial(window, halo_v, va_ref, c), sinks_ref, layer, bias_scr,
                                      key_lo, xa_ref)
            heads += _gla_chunk(c, qg_ref, vg_ref, zg_ref, s0_scr, gnorm, k_scr, b_scr, xg_ref)
        yield from _pipelined(heads, HEAD_STAGES)
        halo_k[nxt] = ka_ref[chunks * CHUNK - WINDOW:, :]
        halo_v[nxt] = va_ref[chunks * CHUNK - WINDOW:, :]

    n_mix = 1 + GLA_HEADS + chunks * (N_KV_HEADS + GLA_HEADS)
    _interleave(mix_stages(), n_mix, out_stages(), N_OUT_STAGES)

    @pl.when(t == n_steps - 1)
    def _():
        st_ref[0] = state[...]
        knew_ref[0] = ka_ref[chunks * CHUNK - WINDOW:, :].astype(F32)
        vnew_ref[0] = va_ref[chunks * CHUNK - WINDOW:, :].astype(F32)


def _mix_out_sample_kernel(layer, n_seqs, *refs):
    last = layer == DEPTH - 1
    n_untouched = (3 if layer > 0 else 0) + (0 if last else 2)
    (qa_ref, za_ref, vg_ref, zg_ref, qg_ref, kg_ref, ka_ref, va_ref, lr_ref,
     wfu_ref, bf_ref, gn_ref, sinks_ref, ck_ref, cv_ref, s0_ref,
     ga_ref, gg_ref, h_in_ref, wa_ref, wg_ref, wo_ref, bg_ref, npost_ref, npre_ref) = refs[:25]
    refs = refs[25 + n_untouched:]
    out_refs, refs = refs[:1 if last else 2], refs[1 if last else 2:]
    (st_ref, knew_ref, vnew_ref,
     bias_scr, k_scr, b_scr, s0_scr, xa_buf, xg_buf, merged_scr, z_scr) = refs
    s = pl.program_id(0)
    xa_ref, xg_ref, xa_prev, xg_prev = _swap_mixer_outputs(s, xa_buf, xg_buf)
    out_stages = _out_stages(xa_prev, xg_prev, ga_ref, gg_ref, h_in_ref, wa_ref, wg_ref, wo_ref, bg_ref,
                             npost_ref, None if last else npre_ref, merged_scr, z_scr, out_refs)

    @pl.when(s == 0)
    def _():
        _fill_alibi(bias_scr)

    def window(cache_ref, new_ref, g, cols):
        return jnp.concatenate([cache_ref[g, :, cols].astype(BF16), new_ref[g * CHUNK:(g + 1) * CHUNK, cols]], axis=0)

    def mix_stages():
        _gla_prepare(lr_ref, kg_ref, wfu_ref, bf_ref, k_scr, b_scr)
        yield
        yield from _gla_states([(s0_ref.at[g], st_ref.at[g]) for g in range(n_seqs)], vg_ref, k_scr, b_scr, s0_scr)
        gnorm = gn_ref[...]
        heads = []
        for g in range(n_seqs):
            r = g * CHUNK
            knew_ref[g, 0:WINDOW - CHUNK, :] = ck_ref[g, CHUNK:, :]
            knew_ref[g, WINDOW - CHUNK:, :] = ka_ref[r:r + CHUNK, :].astype(F32)
            vnew_ref[g, 0:WINDOW - CHUNK, :] = cv_ref[g, CHUNK:, :]
            vnew_ref[g, WINDOW - CHUNK:, :] = va_ref[r:r + CHUNK, :].astype(F32)
            heads += _attention_chunk(r, qa_ref, za_ref, functools.partial(window, ck_ref, ka_ref, g),
                                      functools.partial(window, cv_ref, va_ref, g), sinks_ref, layer, bias_scr,
                                      None, xa_ref)
            heads += _gla_chunk(g, qg_ref, vg_ref, zg_ref, s0_scr, gnorm, k_scr, b_scr, xg_ref)
        yield from _pipelined(heads, HEAD_STAGES)

    n_mix = 1 + GLA_HEADS + n_seqs * (N_KV_HEADS + GLA_HEADS)
    _interleave(mix_stages(), n_mix, out_stages, N_OUT_STAGES)


def _section_specs(rows, row_index):
    def spec(width, col):
        return pl.BlockSpec((rows, width), functools.partial(
            lambda cb, *g: (row_index(*g), cb), col // width))
    return [spec(ATTN_WIDTH, COL_QA), spec(ATTN_WIDTH, COL_ZA), spec(GLA_DV, COL_VG), spec(GLA_DV, COL_ZG),
            spec(GLA_DK, COL_QG), spec(GLA_DK, COL_KG), spec(KV_WIDTH, COL_KA), spec(KV_WIDTH, COL_VA),
            spec(LR_PAD, 0)]


def _layer_weight_specs(layer, ngrid):
    zeros = (0,) * 2
    return [pl.BlockSpec((None, LR_PAD, GLA_DK), lambda *g: (layer,) + zeros),
            pl.BlockSpec((None, 1, GLA_DK), lambda *g: (layer,) + zeros),
            pl.BlockSpec((None, 1, GLA_HV), lambda *g: (layer,) + zeros),
            pl.BlockSpec(memory_space=pltpu.SMEM)]


def _stacked_outputs(layer, batch, seqs_per_step, batch_block, n_inputs, first_out, carried):
    lead = lambda *g: (layer, batch_block(*g))
    nb = seqs_per_step
    specs = [pl.BlockSpec((None, nb, GLA_HEADS, GLA_HK, GLA_HV), lambda *g: lead(*g) + (0, 0, 0)),
             pl.BlockSpec((None, nb, WINDOW, KV_WIDTH), lambda *g: lead(*g) + (0, 0)),
             pl.BlockSpec((None, nb, WINDOW, KV_WIDTH), lambda *g: lead(*g) + (0, 0))]
    shapes = [jax.ShapeDtypeStruct((DEPTH, batch, GLA_HEADS, GLA_HK, GLA_HV), F32),
              jax.ShapeDtypeStruct((DEPTH, batch, WINDOW, KV_WIDTH), F32),
              jax.ShapeDtypeStruct((DEPTH, batch, WINDOW, KV_WIDTH), F32)]
    if carried is None:
        return specs, shapes, [], {}
    alias_specs = [pl.BlockSpec(memory_space=pl.ANY)] * 3
    aliases = {n_inputs + k: first_out + k for k in range(3)}
    return specs, shapes, alias_specs, aliases


def _out_weight_specs(layer):
    wspec = lambda k: pl.BlockSpec((None, k, D_MODEL), lambda *g: (layer, 0, 0), pipeline_mode=pl.Buffered(1))
    vspec = lambda n, l: pl.BlockSpec((None, 1, n), lambda *g: (l, 0, 0))
    return [wspec(ATTN_WIDTH), wspec(GLA_DV), wspec(D_MODEL),
            vspec(2 * D_MODEL, layer), vspec(D_MODEL, layer), vspec(D_MODEL, min(layer + 1, DEPTH - 1))]


def _mix_out_prompt(p, p_lr, h_in, n_total, wfu, bfg, gn, sinks, wa, wg, wo, bg, npost, npre,
                    layer, batch, seq, carried):
    last = layer == DEPTH - 1
    chunks = PROMPT_CHUNKS_PER_STEP
    rows = chunks * CHUNK
    assert rows == TM_OUT
    n_steps = seq // rows
    n_tiles = batch * n_steps
    mix_tile = lambda s: jnp.minimum(s, n_tiles - 1)
    out_tile = lambda s: jnp.maximum(s - 1, 0)
    tile = lambda col: pl.BlockSpec((rows, D_MODEL), lambda s: (out_tile(s), col))
    in_specs = (_section_specs(rows, mix_tile) + _layer_weight_specs(layer, 1)
                + [tile(COL_GA // D_MODEL), tile(COL_GG // D_MODEL), tile(0)] + _out_weight_specs(layer))
    n_main = 1 if last else 2
    st_specs, st_shapes, alias_specs, aliases = _stacked_outputs(
        layer, batch, 1, lambda s: jnp.minimum(s // n_steps, batch - 1), len(in_specs), n_main, carried)
    if last:
        out_shape = [jax.ShapeDtypeStruct((n_tiles * rows, D_MODEL), F32)]
    else:
        out_shape = [jax.ShapeDtypeStruct((n_total, D_MODEL), F32), jax.ShapeDtypeStruct((n_total, D_MODEL), BF16)]
    outs = pl.pallas_call(
        functools.partial(_mix_out_prompt_kernel, layer, n_steps, chunks),
        grid=(n_tiles + 1,),
        in_specs=in_specs + alias_specs,
        out_specs=[tile(0)] * n_main + st_specs,
        out_shape=out_shape + st_shapes,
        input_output_aliases=aliases,
        scratch_shapes=[pltpu.VMEM((2, WINDOW, KV_WIDTH), BF16),
                        pltpu.VMEM((2, WINDOW, KV_WIDTH), BF16),
                        pltpu.VMEM((GLA_HEADS, GLA_HK, GLA_HV), F32),
                        pltpu.VMEM((N_KV_HEADS, KEYS, GROUP_ROWS), F32),
                        pltpu.VMEM((rows, GLA_DK), F32),
                        pltpu.VMEM((rows, GLA_DK), F32),
                        pltpu.VMEM((rows // CHUNK, GLA_HEADS, GLA_HK, GLA_HV), BF16),
                        pltpu.VMEM((2, rows, ATTN_WIDTH), BF16),
                        pltpu.VMEM((2, rows, GLA_DV), BF16),
                        pltpu.VMEM((rows, D_MODEL), BF16),
                        pltpu.VMEM((rows, D_MODEL), F32)],
        compiler_params=pltpu.CompilerParams(
            dimension_semantics=("arbitrary",), vmem_limit_bytes=VMEM_LIMIT),
        name="mix_out_prompt",
    )(*([p] * 8), p_lr, wfu, bfg, gn, sinks, p, p, h_in, wa, wg, wo, bg, npost, npre, *(carried or ()))
    return tuple(outs[:n_main]), tuple(outs[n_main:])


def _mix_out_sample(p, p_lr, h_in, row0_h, new_h_u, wfu, bfg, gn, sinks, cache_k, cache_v, state,
                    wa, wg, wo, bg, npost, npre, layer, batch, carried):
    last = layer == DEPTH - 1
    nb = SAMPLE_SEQS_PER_STEP
    rows = nb * CHUNK
    n_tiles = batch // nb
    row0 = p.shape[0] - batch * CHUNK
    assert row0 % rows == 0 and row0_h % rows == 0 and batch % nb == 0
    blk0, blk0_h = row0 // rows, row0_h // rows
    mix_tile = lambda s: jnp.minimum(s, n_tiles - 1)
    out_tile = lambda s: jnp.maximum(s - 1, 0)
    tile = lambda base, col: pl.BlockSpec((rows, D_MODEL), lambda s: (base + out_tile(s), col))
    cache_spec = pl.BlockSpec((None, nb, WINDOW, KV_WIDTH), lambda s: (layer, mix_tile(s), 0, 0))
    state_spec = pl.BlockSpec((None, nb, GLA_HEADS, GLA_HK, GLA_HV), lambda s: (layer, mix_tile(s), 0, 0, 0))
    in_specs = (_section_specs(rows, lambda s: blk0 + mix_tile(s)) + _layer_weight_specs(layer, 1)
                + [cache_spec, cache_spec, state_spec]
                + [tile(blk0, COL_GA // D_MODEL), tile(blk0, COL_GG // D_MODEL), tile(blk0_h, 0)]
                + _out_weight_specs(layer))
    n_main = 1 if last else 2
    st_specs, st_shapes, alias_specs, aliases = _stacked_outputs(
        layer, batch, nb, mix_tile, len(in_specs), n_main, carried)
    if last:
        extra, out_specs = (), [tile(0, 0)]
        out_shape = [jax.ShapeDtypeStruct((batch * CHUNK, D_MODEL), F32)]
    else:
        extra, out_specs = tuple(new_h_u), [tile(blk0, 0), tile(blk0, 0)]
        out_shape = [jax.ShapeDtypeStruct(a.shape, a.dtype) for a in new_h_u]
        first_extra = len(in_specs) + len(alias_specs)
        aliases = {**aliases, first_extra: 0, first_extra + 1: 1}
    outs = pl.pallas_call(
        functools.partial(_mix_out_sample_kernel, layer, nb),
        grid=(n_tiles + 1,),
        in_specs=in_specs + alias_specs + [pl.BlockSpec(memory_space=pl.ANY)] * len(extra),
        out_specs=out_specs + st_specs,
        out_shape=out_shape + st_shapes,
        input_output_aliases=aliases,
        scratch_shapes=[pltpu.VMEM((N_KV_HEADS, KEYS, GROUP_ROWS), F32),
                        pltpu.VMEM((rows, GLA_DK), F32),
                        pltpu.VMEM((rows, GLA_DK), F32),
                        pltpu.VMEM((rows // CHUNK, GLA_HEADS, GLA_HK, GLA_HV), BF16),
                        pltpu.VMEM((2, rows, ATTN_WIDTH), BF16),
                        pltpu.VMEM((2, rows, GLA_DV), BF16),
                        pltpu.VMEM((rows, D_MODEL), BF16),
                        pltpu.VMEM((rows, D_MODEL), F32)],
        compiler_params=pltpu.CompilerParams(
            dimension_semantics=("arbitrary",), vmem_limit_bytes=VMEM_LIMIT),
        name="mix_out_sample",
    )(*([p] * 8), p_lr, wfu, bfg, gn, sinks, cache_k, cache_v, state, p, p, h_in, wa, wg, wo, bg, npost, npre,
      *(carried or ()), *extra)
    return tuple(outs[:n_main]), tuple(outs[n_main:])


def kernel(x_prompt, x_sample, cache_k, cache_v, state_gla, norm_pre, norm_post, w_in, b_gate,
           attn_sinks, w_forget_up, b_forget, gla_norm, w_branch_attn, w_branch_gla, w_out):
    batch, seq, _ = x_prompt.shape
    dec_batch, dec_seq, _ = x_sample.shape
    assert dec_seq == CHUNK and seq % (PROMPT_CHUNKS_PER_STEP * CHUNK) == 0
    assert cache_k.shape[2] == WINDOW
    n_prompt = batch * seq
    n_sample = dec_batch * dec_seq

    w_in_t = jnp.swapaxes(w_in, 1, 2)
    wa =w_branch_attn.astype(BF16)
    wg = w_branch_gla.astype(BF16)
    wo = w_out.astype(BF16)
    wfu = jnp.pad(w_forget_up, ((0, 0), (0, LR_PAD - GLA_RANK), (0, 0))).astype(BF16)
    bfg = b_forget.reshape(DEPTH, 1, GLA_DK)
    gn = gla_norm.reshape(DEPTH, 1, GLA_HV)
    bg = b_gate.reshape(DEPTH, 1, 2 * D_MODEL)
    npre = norm_pre.reshape(DEPTH, 1, D_MODEL)
    npost = norm_post.reshape(DEPTH, 1, D_MODEL)
    ck = cache_k.reshape(DEPTH, dec_batch, WINDOW, KV_WIDTH)
    cv = cache_v.reshape(DEPTH, dec_batch, WINDOW, KV_WIDTH)

    xp = x_prompt.reshape(n_prompt, D_MODEL)
    xs = x_sample.reshape(n_sample, D_MODEL)
    u = _prenorm(xp, xs, npre[0])
    h_prompt, h_sample, sample_row0 = xp, xs, 0

    stacked_p = stacked_s = None
    for l in range(DEPTH):
        p, p_lr = _project(u, w_in_t, l)
        out_p, stacked_p = _mix_out_prompt(p, p_lr, h_prompt, n_prompt + n_sample, wfu, bfg, gn, attn_sinks,
                                           wa, wg, wo, bg, npost, npre, l, batch, seq, stacked_p)
        out_s, stacked_s = _mix_out_sample(p, p_lr, h_sample, sample_row0, out_p, wfu, bfg, gn, attn_sinks,
                                           ck, cv, state_gla, wa, wg, wo, bg, npost, npre, l, dec_batch, stacked_s)
        if l < DEPTH - 1:
            h_prompt = h_sample = out_s[0]
            u = out_s[1]
            sample_row0 = n_prompt

    y_prompt = out_p[0].reshape(batch, seq, D_MODEL)
    y_sample = out_s[0].reshape(dec_batch, dec_seq, D_MODEL)
    heads = lambda a: a.reshape(a.shape[:3] + (N_KV_HEADS, HEAD_DIM))
    return (y_prompt, y_sample, heads(stacked_p[1]), heads(stacked_p[2]), stacked_p[0],
            heads(stacked_s[1]), heads(stacked_s[2]), stacked_s[0])
```

```python
import functools
import math

import jax
import jax.numpy as jnp
from jax import lax
from jax.experimental import pallas as pl
from jax.experimental.pallas import tpu as pltpu

F32 = jnp.float32
BF16 = jnp.bfloat16

D_MODEL = 2048
DEPTH = 4
CHUNK = 64
WINDOW = 128
HEAD_DIM = 64
N_Q_HEADS = 16
N_KV_HEADS = 4
GQA_GROUP = N_Q_HEADS // N_KV_HEADS
ATTN_WIDTH = N_Q_HEADS * HEAD_DIM
KV_WIDTH = N_KV_HEADS * HEAD_DIM
GLA_HEADS = 4
GLA_DK = 512
GLA_DV = 1024
GLA_HK = GLA_DK // GLA_HEADS
GLA_HV = GLA_DV // GLA_HEADS
GLA_RANK = 16
GLA_TAU = 16.0
NORM_EPS = 1e-6
LOG2E = math.log2(math.e)
IN_SIZES = (ATTN_WIDTH, KV_WIDTH, KV_WIDTH, ATTN_WIDTH,
            GLA_DK, GLA_DK, GLA_DV, GLA_DV, GLA_RANK, D_MODEL, D_MODEL)

KEYS = WINDOW + CHUNK
SUB = 8
NSUB = CHUNK // SUB

TM_PROJ = 3072
TM_PROJ_SUB = 512
TN_PROJ = 512
OUT_COLS = 256
TM_NORM = 512
PROMPT_CHUNKS_PER_STEP = 4
SAMPLE_SEQS_PER_STEP = 2

VMEM_LIMIT = 56 * 1024 * 1024

SRC_COLS = {}
_acc = 0
for _name, _size in zip(("qa", "ka", "va", "za", "qg", "kg", "vg", "zg", "lr", "ga", "gg"), IN_SIZES):
    SRC_COLS[_name] = (_acc, _size)
    _acc += _size
IN_COLS = _acc

PACKED_ORDER = ("ga", "gg", "qa", "za", "vg", "zg", "qg", "kg", "ka", "va")
PACKED_COL = {}
PACKED_SRC_ROWS = []
_acc = 0
for _name in PACKED_ORDER:
    _start, _size = SRC_COLS[_name]
    PACKED_COL[_name] = _acc
    if _acc % TN_PROJ == 0:
        PACKED_SRC_ROWS += [_start + k for k in range(0, max(_size, TN_PROJ), TN_PROJ)]
    _acc += _size
PACKED_COLS = _acc
COL_GA, COL_GG, COL_QA, COL_ZA, COL_VG, COL_ZG, COL_QG, COL_KG, COL_KA, COL_VA = (
    PACKED_COL[_n] for _n in PACKED_ORDER)
LR_PAD = 128
assert SRC_COLS["va"][0] == SRC_COLS["ka"][0] + KV_WIDTH and 2 * KV_WIDTH == TN_PROJ
assert len(PACKED_SRC_ROWS) * TN_PROJ == PACKED_COLS
assert all(_r % GLA_RANK == 0 for _r in PACKED_SRC_ROWS)


def _rms(x, g):
    return x * lax.rsqrt(jnp.mean(x * x, axis=-1, keepdims=True) + NORM_EPS) * g


def _two_group_index_maps(n_prompt_tiles):
    p_idx = lambda i: (jnp.minimum(i, n_prompt_tiles - 1), 0)
    s_idx = lambda i: (jnp.maximum(i - n_prompt_tiles, 0), 0)
    return p_idx, s_idx


def _prenorm_kernel(n_prompt_tiles, xp_ref, xs_ref, g_ref, u_ref):
    x = jnp.where(pl.program_id(0) < n_prompt_tiles, xp_ref[...], xs_ref[...])
    u_ref[...] = _rms(x, g_ref[...]).astype(BF16)


def _prenorm(x_prompt, x_sample, g):
    n_prompt_tiles = x_prompt.shape[0] // TM_NORM
    n_tiles = n_prompt_tiles + x_sample.shape[0] // TM_NORM
    p_idx, s_idx = _two_group_index_maps(n_prompt_tiles)
    return pl.pallas_call(
        functools.partial(_prenorm_kernel, n_prompt_tiles),
        grid=(n_tiles,),
        in_specs=[pl.BlockSpec((TM_NORM, D_MODEL), p_idx),
                  pl.BlockSpec((TM_NORM, D_MODEL), s_idx),
                  pl.BlockSpec((1, D_MODEL), lambda i: (0, 0))],
        out_specs=pl.BlockSpec((TM_NORM, D_MODEL), lambda i: (i, 0)),
        out_shape=jax.ShapeDtypeStruct((n_tiles * TM_NORM, D_MODEL), BF16),
        compiler_params=pltpu.CompilerParams(dimension_semantics=("arbitrary",)),
        name="prenorm",
    )(x_prompt, x_sample, g)


def _nt_dot(a, b):
    return lax.dot_general(a, b, (((1,), (1,)), ((), ())), preferred_element_type=F32)


def _proj_kernel(src_rows_ref, u_ref, w_ref, wlr_ref, o_ref, olr_ref):
    del src_rows_ref
    w = w_ref[...].astype(BF16)
    for m in range(TM_PROJ // TM_PROJ_SUB):
        rows = slice(m * TM_PROJ_SUB, (m + 1) * TM_PROJ_SUB)
        o_ref[rows, :] = _nt_dot(u_ref[rows, :], w).astype(BF16)

    @pl.when(pl.program_id(1) == 0)
    def _():
        wlr = wlr_ref[...].astype(BF16)
        for m in range(TM_PROJ // TM_PROJ_SUB):
            rows = slice(m * TM_PROJ_SUB, (m + 1) * TM_PROJ_SUB)
            olr_ref[rows, :] = _nt_dot(u_ref[rows, :], wlr).astype(BF16)


def _project(u, w_in_t, layer):
    t = u.shape[0]
    lr_row = SRC_COLS["lr"][0]
    wspec = lambda rows, index_map: pl.BlockSpec((None, pl.Element(rows), pl.Element(D_MODEL)), index_map)
    grid_spec = pltpu.PrefetchScalarGridSpec(
        num_scalar_prefetch=1,
        grid=(t // TM_PROJ, PACKED_COLS // TN_PROJ),
        in_specs=[pl.BlockSpec((TM_PROJ, D_MODEL), lambda i, j, src: (i, 0)),
                  wspec(TN_PROJ, lambda i, j, src: (layer, src[j] * GLA_RANK, 0)),
                  wspec(LR_PAD, lambda i, j, src: (layer, lr_row, 0))],
        out_specs=[pl.BlockSpec((TM_PROJ, TN_PROJ), lambda i, j, src: (i, j)),
                   pl.BlockSpec((TM_PROJ, LR_PAD), lambda i, j, src: (i, 0))])
    return pl.pallas_call(
        _proj_kernel,
        grid_spec=grid_spec,
        out_shape=[jax.ShapeDtypeStruct((t, PACKED_COLS), BF16),
                   jax.ShapeDtypeStruct((t, LR_PAD), BF16)],
        compiler_params=pltpu.CompilerParams(
            dimension_semantics=("arbitrary", "arbitrary"), vmem_limit_bytes=VMEM_LIMIT),
        name="proj",
    )(jnp.asarray([r // GLA_RANK for r in PACKED_SRC_ROWS], jnp.int32), u, w_in_t, w_in_t)


def _run(stages):
    for _ in stages:
        pass


def _pipelined(tasks, lag):
    started = []
    for task in list(tasks) + [None] * lag:
        if task is not None:
            next(task)
            started.append(task)
        if len(started) > lag or task is None:
            _run(started.pop(0))
            yield


def _interleave(major, n_major, minor, n_minor):
    done = 0
    for i in range(n_major):
        while done < n_minor and done * n_major < (i + 1) * n_minor:
            next(minor)
            done += 1
        next(major)
    _run(minor)
    _run(major)


def _silu(x):
    half = 0.5 * x
    return half + half * jnp.tanh(half)


def _log_sigmoid(x):
    return jnp.minimum(x, 0.0) - jnp.log(1.0 + jnp.exp(-jnp.abs(x)))


def _alibi_slope(h):
    return math.pow(2.0, -8.0 * (h + 1) / N_Q_HEADS)


HEAD_LAG = 1
GROUP_ROWS = GQA_GROUP * CHUNK


def _fill_alibi(bias_ref):
    s = lax.broadcasted_iota(jnp.int32, (KEYS, CHUNK), 0)
    t = lax.broadcasted_iota(jnp.int32, (KEYS, CHUNK), 1)
    dist = jnp.abs(WINDOW + t - s).astype(F32)
    for kh in range(N_KV_HEADS):
        for g in range(GQA_GROUP):
            bias_ref[kh, :, g * CHUNK:(g + 1) * CHUNK] = dist * (-_alibi_slope(kh * GQA_GROUP + g))


def _attention_chunk(r, qa_ref, za_ref, kwin, vwin, sinks_ref, layer, bias_ref, key_lo, xa_ref):
    rows = pl.ds(r, CHUNK)
    lane_head = lax.broadcasted_iota(jnp.int32, (1, GROUP_ROWS), 1) // CHUNK
    if key_lo is not None:
        key_ok = lax.broadcasted_iota(jnp.int32, (KEYS, GROUP_ROWS), 0) >= key_lo

    def head(kh):
        k = kwin(pl.ds(kh * HEAD_DIM, HEAD_DIM))
        v = vwin(pl.ds(kh * HEAD_DIM, HEAD_DIM))
        q = jnp.concatenate(
            [qa_ref[rows, pl.ds((kh * GQA_GROUP + g) * HEAD_DIM, HEAD_DIM)] for g in range(GQA_GROUP)],
            axis=0)
        q = q * jnp.asarray(HEAD_DIM ** -0.5, BF16)
        s = _nt_dot(k, q) + bias_ref[kh]
        if key_lo is not None:
            s = jnp.where(key_ok, s, -jnp.inf)
        sink = jnp.zeros((1, GROUP_ROWS), F32)
        for g in range(GQA_GROUP):
            sink = jnp.where(lane_head == g, sinks_ref[layer, kh * GQA_GROUP + g], sink)
        m = jnp.maximum(jnp.max(s, axis=0, keepdims=True), sink)
        p = jnp.exp(s - m)
        den = jnp.sum(p, axis=0, keepdims=True) + jnp.exp(sink - m)
        p = p.astype(BF16)
        yield
        o_t = lax.dot_general(v, p, (((0,), (0,)), ((), ())), preferred_element_type=F32)
        o = jnp.transpose(o_t / den)
        for g in range(GQA_GROUP):
            h = kh * GQA_GROUP + g
            cols = pl.ds(h * HEAD_DIM, HEAD_DIM)
            z = za_ref[rows, cols].astype(F32)
            xa_ref[rows, cols] = (o[g * CHUNK:(g + 1) * CHUNK] * _silu(z)).astype(BF16)
        yield

    return [head(kh) for kh in range(N_KV_HEADS)]


def _gla_prepare(lr_ref, kg_ref, wfu_ref, bf_ref, k_scr, b_scr):
    n = lr_ref.shape[0]
    la2 = _log2_forget(lr_ref[...], wfu_ref, bf_ref)
    row = lax.broadcasted_iota(jnp.int32, (n, n), 0)
    col = lax.broadcasted_iota(jnp.int32, (n, n), 1)
    tri = jnp.where(row // CHUNK == col // CHUNK, row - col, -1) >= 0
    tri = jnp.where(tri, 1.0, 0.0).astype(BF16)
    la_hi = la2.astype(BF16)
    rem = la2 - la_hi.astype(F32)
    la_mid = rem.astype(BF16)
    la_lo = (rem - la_mid.astype(F32)).astype(BF16)
    b_scr[...] = (jnp.dot(tri, la_hi, preferred_element_type=F32)
                  + jnp.dot(tri, la_mid, preferred_element_type=F32)
                  + jnp.dot(tri, la_lo, preferred_element_type=F32))
    k_scr[...] = kg_ref[...].astype(F32)


def _gla_states(chunk_states, vg_ref, k_scr, b_scr, s0_scr):
    for hh in range(GLA_HEADS):
        kcols = pl.ds(hh * GLA_HK, GLA_HK)
        vcols = pl.ds(hh * GLA_HV, GLA_HV)
        steps = []
        for c in range(len(chunk_states)):
            rows = pl.ds(c * CHUNK, CHUNK)
            b_end = b_scr[c * CHUNK + CHUNK - 1:(c + 1) * CHUNK, kcols]
            k_til = (k_scr[rows, kcols] * jnp.exp2(b_end - b_scr[rows, kcols])).astype(BF16)
            upd = lax.dot_general(k_til, vg_ref[rows, vcols], (((0,), (0,)), ((), ())),
                                  preferred_element_type=F32)
            e_col = jnp.transpose(jnp.broadcast_to(jnp.exp2(b_end), (GLA_HK, GLA_HK)))
            steps.append((jnp.concatenate([e_col, e_col], axis=1), upd))
        s = None
        for c, ((in_ref, out_ref), (decay, upd)) in enumerate(zip(chunk_states, steps)):
            if c == 0 or in_ref is not chunk_states[c - 1][1]:
                s = in_ref[hh]
            s0_scr[c, hh] = s.astype(BF16)
            s = s * decay + upd
            if c + 1 == len(chunk_states) or chunk_states[c + 1][0] is not out_ref:
                out_ref[hh] = s
        yield


def _gla_chunk(c, qg_ref, vg_ref, zg_ref, s0_scr, gnorm, k_scr, b_scr, xg_ref):
    r = c * CHUNK
    rows = pl.ds(r, CHUNK)
    lane = lax.broadcasted_iota(jnp.int32, (SUB, CHUNK), 1)
    row0 = lax.broadcasted_iota(jnp.int32, (SUB, CHUNK), 0)

    def head(hh):
        kcols = pl.ds(hh * GLA_HK, GLA_HK)
        vcols = pl.ds(hh * GLA_HV, GLA_HV)
        q = qg_ref[rows, kcols].astype(F32) * (GLA_HK ** -0.5)
        k = k_scr[rows, kcols]
        v = vg_ref[rows, vcols]
        b = b_scr[rows, kcols]
        blk = lambda x, j: x[j * SUB:(j + 1) * SUB]
        b_row = lambda t: b_scr[r + t:r + t + 1, kcols]
        b_last = [b_row(j * SUB + SUB - 1) for j in range(NSUB)]

        k_hat = jnp.concatenate([blk(k, j) * jnp.exp2(b_last[j] - blk(b, j)) for j in range(NSUB)], axis=0)
        lhs = jnp.concatenate([q[(j + 1) * SUB:] * jnp.exp2(b[(j + 1) * SUB:] - b_last[j])
                               for j in range(NSUB - 1)], axis=0)
        rr = _nt_dot(lhs.astype(BF16), k_hat.astype(BF16))
        o_inter = jnp.dot((q * jnp.exp2(b)).astype(BF16), s0_scr[c, hh], preferred_element_type=F32)
        yield

        score_rows = []
        for i in range(NSUB):
            acc = jnp.zeros((SUB, CHUNK), F32)
            for j in range(i):
                base = sum((NSUB - 1 - jj) * SUB for jj in range(j)) + (i - j - 1) * SUB
                acc = jnp.where(lane // SUB == j, rr[base:base + SUB], acc)
            qi, bi = blk(q, i), blk(b, i)
            for s in range(i * SUB, (i + 1) * SUB):
                prod = (qi * k_scr[r + s:r + s + 1, kcols]) * jnp.exp2(bi - b_row(s))
                acc = jnp.where(lane == s, jnp.sum(prod, axis=1, keepdims=True), acc)
            score_rows.append(jnp.where(lane <= row0 + i * SUB, acc, 0.0))
        scores = jnp.concatenate(score_rows, axis=0).astype(BF16)

        o = jnp.dot(scores, v, preferred_element_type=F32) + o_inter

        y = _rms(o, gnorm)
        z = zg_ref[rows, vcols].astype(F32)
        xg_ref[rows, vcols] = (y * _silu(z)).astype(BF16)
        yield

    return [head(hh) for hh in range(GLA_HEADS)]


def _log2_forget(lr, wfu_ref, bf_ref):
    x = jnp.dot(lr, wfu_ref[...], preferred_element_type=F32) + bf_ref[...]
    return _log_sigmoid(x) * (LOG2E / GLA_TAU)


def _sigmoid(x):
    return 1.0 / (1.0 + jnp.exp(-x))


def _swap_mixer_outputs(s, xa_buf, xg_buf):
    @pl.when(s == 0)
    def _():
        xa_buf[1] = jnp.zeros(xa_buf.shape[1:], BF16)
        xg_buf[1] = jnp.zeros(xg_buf.shape[1:], BF16)

    return xa_buf.at[s % 2], xg_buf.at[s % 2], xa_buf[1 - s % 2], xg_buf[1 - s % 2]


N_OUT_STAGES = 3 * (D_MODEL // OUT_COLS) + 1


def _out_stages(xa, xg, ga_ref, gg_ref, h_in_ref, wa_ref, wg_ref, wo_ref, bg_ref, npost_ref, npre_ref,
                merged_scr, z_scr, out_refs):
    bg = bg_ref[...]
    for n in range(D_MODEL // OUT_COLS):
        cols = slice(n * OUT_COLS, (n + 1) * OUT_COLS)
        ya = jnp.dot(xa, wa_ref[:, cols], preferred_element_type=F32)
        yield
        yg = jnp.dot(xg, wg_ref[:, cols], preferred_element_type=F32)
        gate_a = _sigmoid(ga_ref[:, cols].astype(F32) + bg[:, n * OUT_COLS:(n + 1) * OUT_COLS])
        gate_g = _sigmoid(gg_ref[:, cols].astype(F32)
                          + bg[:, D_MODEL + n * OUT_COLS:D_MODEL + (n + 1) * OUT_COLS])
        merged_scr[:, cols] = (gate_a * ya + gate_g * yg).astype(BF16)
        yield
    for n in range(D_MODEL // OUT_COLS):
        cols = slice(n * OUT_COLS, (n + 1) * OUT_COLS)
        z_scr[:, cols] = jnp.dot(merged_scr[...], wo_ref[:, cols], preferred_element_type=F32)
        yield
    h = h_in_ref[...] + _rms(z_scr[...], npost_ref[...])
    out_refs[0][...] = h
    if npre_ref is not None:
        out_refs[1][...] = _rms(h, npre_ref[...]).astype(BF16)
    yield


def _mix_out_prompt_kernel(layer, n_steps, chunks, *refs):
    last = layer == DEPTH - 1
    n_carried = 3 if layer > 0 else 0
    (qa_ref, za_ref, vg_ref, zg_ref, qg_ref, kg_ref, ka_ref, va_ref, lr_ref,
     wfu_ref, bf_ref, gn_ref, sinks_ref,
     ga_ref, gg_ref, h_in_ref, wa_ref, wg_ref, wo_ref, bg_ref, npost_ref, npre_ref) = refs[:22]
    refs = refs[22 + n_carried:]
    out_refs, refs = refs[:1 if last else 2], refs[1 if last else 2:]
    (st_ref, knew_ref, vnew_ref,
     halo_k, halo_v, state, bias_scr, k_scr, b_scr, s0_scr, xa_buf, xg_buf, merged_scr, z_scr) = refs
    s = pl.program_id(0)
    b = s // n_steps
    t = s % n_steps
    assert chunks * CHUNK >= WINDOW
    xa_ref, xg_ref, xa_prev, xg_prev = _swap_mixer_outputs(s, xa_buf, xg_buf)
    out_stages = functools.partial(
        _out_stages, xa_prev, xg_prev, ga_ref, gg_ref, h_in_ref, wa_ref, wg_ref, wo_ref, bg_ref, npost_ref,
        None if last else npre_ref, merged_scr, z_scr, out_refs)

    cur = t % 2
    nxt = 1 - cur

    @pl.when((b == 0) & (t == 0))
    def _():
        _fill_alibi(bias_scr)

    @pl.when(t == 0)
    def _():
        halo_k[0] = jnp.zeros((WINDOW, KV_WIDTH), BF16)
        halo_v[0] = jnp.zeros((WINDOW, KV_WIDTH), BF16)
        state[...] = jnp.zeros(state.shape, F32)

    def window(halo, tile_ref, c, cols):
        lo = c * CHUNK - WINDOW
        if lo >= 0:
            return tile_ref[lo:lo + KEYS, cols]
        return jnp.concatenate([halo[cur, WINDOW + lo:, cols], tile_ref[0:(c + 1) * CHUNK, cols]], axis=0)

    def mix_stages():
        _gla_prepare(lr_ref, kg_ref, wfu_ref, bf_ref, k_scr, b_scr)
        yield
        yield from _gla_states([(state, state)] * chunks, vg_ref, k_scr, b_scr, s0_scr)
        gnorm = gn_ref[...]
        heads = []
        for c in range(chunks):
            key_lo = (WINDOW // CHUNK - (t * chunks + c)) * CHUNK if c < WINDOW // CHUNK else None
            heads += _attention_chunk(c * CHUNK, qa_ref, za_ref, functools.partial(window, halo_k, ka_ref, c),
                                      functools.partial(window, halo_v, va_ref, c), sinks_ref, layer, bias_scr,
                                      key_lo, xa_ref)
            heads += _gla_chunk(c, qg_ref, vg_ref, zg_ref, s0_scr, gnorm, k_scr, b_scr, xg_ref)
        yield from _pipelined(heads, HEAD_LAG)
        halo_k[nxt] = ka_ref[chunks * CHUNK - WINDOW:, :]
        halo_v[nxt] = va_ref[chunks * CHUNK - WINDOW:, :]

    n_mix = 1 + GLA_HEADS + chunks * (N_KV_HEADS + GLA_HEADS)
    _interleave(mix_stages(), n_mix, out_stages(), N_OUT_STAGES)

    @pl.when(t == n_steps - 1)
    def _():
        st_ref[0] = state[...]
        knew_ref[0] = ka_ref[chunks * CHUNK - WINDOW:, :].astype(F32)
        vnew_ref[0] = va_ref[chunks * CHUNK - WINDOW:, :].astype(F32)


def _mix_out_sample_kernel(layer, n_seqs, *refs):
    last = layer == DEPTH - 1
    n_untouched = (3 if layer > 0 else 0) + (0 if last else 2)
    (qa_ref, za_ref, vg_ref, zg_ref, qg_ref, kg_ref, ka_ref, va_ref, lr_ref,
     wfu_ref, bf_ref, gn_ref, sinks_ref, ck_ref, cv_ref, s0_ref,
     ga_ref, gg_ref, h_in_ref, wa_ref, wg_ref, wo_ref, bg_ref, npost_ref, npre_ref) = refs[:25]
    refs = refs[25 + n_untouched:]
    out_refs, refs = refs[:1 if last else 2], refs[1 if last else 2:]
    (st_ref, knew_ref, vnew_ref,
     bias_scr, k_scr, b_scr, s0_scr, xa_buf, xg_buf, merged_scr, z_scr) = refs
    s = pl.program_id(0)
    xa_ref, xg_ref, xa_prev, xg_prev = _swap_mixer_outputs(s, xa_buf, xg_buf)
    out_stages = _out_stages(xa_prev, xg_prev, ga_ref, gg_ref, h_in_ref, wa_ref, wg_ref, wo_ref, bg_ref,
                             npost_ref, None if last else npre_ref, merged_scr, z_scr, out_refs)

    @pl.when(s == 0)
    def _():
        _fill_alibi(bias_scr)

    def window(cache_ref, new_ref, g, cols):
        return jnp.concatenate([cache_ref[g, :, cols].astype(BF16), new_ref[g * CHUNK:(g + 1) * CHUNK, cols]], axis=0)

    def mix_stages():
        _gla_prepare(lr_ref, kg_ref, wfu_ref, bf_ref, k_scr, b_scr)
        yield
        yield from _gla_states([(s0_ref.at[g], st_ref.at[g]) for g in range(n_seqs)], vg_ref, k_scr, b_scr, s0_scr)
        gnorm = gn_ref[...]
        heads = []
        for g in range(n_seqs):
            r = g * CHUNK
            knew_ref[g, 0:WINDOW - CHUNK, :] = ck_ref[g, CHUNK:, :]
            knew_ref[g, WINDOW - CHUNK:, :] = ka_ref[r:r + CHUNK, :].astype(F32)
            vnew_ref[g, 0:WINDOW - CHUNK, :] = cv_ref[g, CHUNK:, :]
            vnew_ref[g, WINDOW - CHUNK:, :] = va_ref[r:r + CHUNK, :].astype(F32)
            heads += _attention_chunk(r, qa_ref, za_ref, functools.partial(window, ck_ref, ka_ref, g),
                                      functools.partial(window, cv_ref, va_ref, g), sinks_ref, layer, bias_scr,
                                      None, xa_ref)
            heads += _gla_chunk(g, qg_ref, vg_ref, zg_ref, s0_scr, gnorm, k_scr, b_scr, xg_ref)
        yield from _pipelined(heads, HEAD_LAG)

    n_mix = 1 + GLA_HEADS + n_seqs * (N_KV_HEADS + GLA_HEADS)
    _interleave(mix_stages(), n_mix, out_stages, N_OUT_STAGES)


def _section_specs(rows, row_index):
    def spec(width, col):
        return pl.BlockSpec((rows, width), functools.partial(
            lambda cb, *g: (row_index(*g), cb), col // width))
    return [spec(ATTN_WIDTH, COL_QA), spec(ATTN_WIDTH, COL_ZA), spec(GLA_DV, COL_VG), spec(GLA_DV, COL_ZG),
            spec(GLA_DK, COL_QG), spec(GLA_DK, COL_KG), spec(KV_WIDTH, COL_KA), spec(KV_WIDTH, COL_VA),
            spec(LR_PAD, 0)]


def _layer_weight_specs(layer):
    zeros = (0,) * 2
    return [pl.BlockSpec((None, LR_PAD, GLA_DK), lambda *g: (layer,) + zeros),
            pl.BlockSpec((None, 1, GLA_DK), lambda *g: (layer,) + zeros),
            pl.BlockSpec((None, 1, GLA_HV), lambda *g: (layer,) + zeros),
            pl.BlockSpec(memory_space=pltpu.SMEM)]


def _stacked_outputs(layer, batch, seqs_per_step, batch_block, n_inputs, first_out, carried):
    lead = lambda *g: (layer, batch_block(*g))
    nb = seqs_per_step
    specs = [pl.BlockSpec((None, nb, GLA_HEADS, GLA_HK, GLA_HV), lambda *g: lead(*g) + (0, 0, 0)),
             pl.BlockSpec((None, nb, WINDOW, KV_WIDTH), lambda *g: lead(*g) + (0, 0)),
             pl.BlockSpec((None, nb, WINDOW, KV_WIDTH), lambda *g: lead(*g) + (0, 0))]
    shapes = [jax.ShapeDtypeStruct((DEPTH, batch, GLA_HEADS, GLA_HK, GLA_HV), F32),
              jax.ShapeDtypeStruct((DEPTH, batch, WINDOW, KV_WIDTH), F32),
              jax.ShapeDtypeStruct((DEPTH, batch, WINDOW, KV_WIDTH), F32)]
    if carried is None:
        return specs, shapes, [], {}
    alias_specs = [pl.BlockSpec(memory_space=pl.ANY)] * 3
    aliases = {n_inputs + k: first_out + k for k in range(3)}
    return specs, shapes, alias_specs, aliases


def _out_weight_specs(layer):
    wspec = lambda k: pl.BlockSpec((None, k, D_MODEL), lambda *g: (layer, 0, 0), pipeline_mode=pl.Buffered(1))
    vspec = lambda n, l: pl.BlockSpec((None, 1, n), lambda *g: (l, 0, 0))
    return [wspec(ATTN_WIDTH), wspec(GLA_DV), wspec(D_MODEL),
            vspec(2 * D_MODEL, layer), vspec(D_MODEL, layer), vspec(D_MODEL, min(layer + 1, DEPTH - 1))]


def _mix_out_prompt(p, p_lr, h_in, n_total, wfu, bfg, gn, sinks, wa, wg, wo, bg, npost, npre,
                    layer, batch, seq, carried):
    last = layer == DEPTH - 1
    chunks = PROMPT_CHUNKS_PER_STEP
    rows = chunks * CHUNK
    n_steps = seq // rows
    n_tiles = batch * n_steps
    mix_tile = lambda s: jnp.minimum(s, n_tiles - 1)
    out_tile = lambda s: jnp.maximum(s - 1, 0)
    tile = lambda col: pl.BlockSpec((rows, D_MODEL), lambda s: (out_tile(s), col))
    in_specs = (_section_specs(rows, mix_tile) + _layer_weight_specs(layer)
                + [tile(COL_GA // D_MODEL), tile(COL_GG // D_MODEL), tile(0)] + _out_weight_specs(layer))
    n_main = 1 if last else 2
    st_specs, st_shapes, alias_specs, aliases = _stacked_outputs(
        layer, batch, 1, lambda s: jnp.minimum(s // n_steps, batch - 1), len(in_specs), n_main, carried)
    if last:
        out_shape = [jax.ShapeDtypeStruct((n_tiles * rows, D_MODEL), F32)]
    else:
        out_shape = [jax.ShapeDtypeStruct((n_total, D_MODEL), F32), jax.ShapeDtypeStruct((n_total, D_MODEL), BF16)]
    outs = pl.pallas_call(
        functools.partial(_mix_out_prompt_kernel, layer, n_steps, chunks),
        grid=(n_tiles + 1,),
        in_specs=in_specs + alias_specs,
        out_specs=[tile(0)] * n_main + st_specs,
        out_shape=out_shape + st_shapes,
        input_output_aliases=aliases,
        scratch_shapes=[pltpu.VMEM((2, WINDOW, KV_WIDTH), BF16),
                        pltpu.VMEM((2, WINDOW, KV_WIDTH), BF16),
                        pltpu.VMEM((GLA_HEADS, GLA_HK, GLA_HV), F32),
                        pltpu.VMEM((N_KV_HEADS, KEYS, GROUP_ROWS), F32),
                        pltpu.VMEM((rows, GLA_DK), F32),
                        pltpu.VMEM((rows, GLA_DK), F32),
                        pltpu.VMEM((rows // CHUNK, GLA_HEADS, GLA_HK, GLA_HV), BF16),
                        pltpu.VMEM((2, rows, ATTN_WIDTH), BF16),
                        pltpu.VMEM((2, rows, GLA_DV), BF16),
                        pltpu.VMEM((rows, D_MODEL), BF16),
                        pltpu.VMEM((rows, D_MODEL), F32)],
        compiler_params=pltpu.CompilerParams(
            dimension_semantics=("arbitrary",), vmem_limit_bytes=VMEM_LIMIT),
        name="mix_out_prompt",
    )(*([p] * 8), p_lr, wfu, bfg, gn, sinks, p, p, h_in, wa, wg, wo, bg, npost, npre, *(carried or ()))
    return tuple(outs[:n_main]), tuple(outs[n_main:])


def _mix_out_sample(p, p_lr, h_in, row0_h, new_h_u, wfu, bfg, gn, sinks, cache_k, cache_v, state,
                    wa, wg, wo, bg, npost, npre, layer, batch, carried):
    last = layer == DEPTH - 1
    nb = SAMPLE_SEQS_PER_STEP
    rows = nb * CHUNK
    n_tiles = batch // nb
    row0 = p.shape[0] - batch * CHUNK
    assert row0 % rows == 0 and row0_h % rows == 0 and batch % nb == 0
    blk0, blk0_h = row0 // rows, row0_h // rows
    mix_tile = lambda s: jnp.minimum(s, n_tiles - 1)
    out_tile = lambda s: jnp.maximum(s - 1, 0)
    tile = lambda base, col: pl.BlockSpec((rows, D_MODEL), lambda s: (base + out_tile(s), col))
    cache_spec = pl.BlockSpec((None, nb, WINDOW, KV_WIDTH), lambda s: (layer, mix_tile(s), 0, 0))
    state_spec = pl.BlockSpec((None, nb, GLA_HEADS, GLA_HK, GLA_HV), lambda s: (layer, mix_tile(s), 0, 0, 0))
    in_specs = (_section_specs(rows, lambda s: blk0 + mix_tile(s)) + _layer_weight_specs(layer)
                + [cache_spec, cache_spec, state_spec]
                + [tile(blk0, COL_GA // D_MODEL), tile(blk0, COL_GG // D_MODEL), tile(blk0_h, 0)]
                + _out_weight_specs(layer))
    n_main = 1 if last else 2
    st_specs, st_shapes, alias_specs, aliases = _stacked_outputs(
        layer, batch, nb, mix_tile, len(in_specs), n_main, carried)
    if last:
        extra, out_specs = (), [tile(0, 0)]
        out_shape = [jax.ShapeDtypeStruct((batch * CHUNK, D_MODEL), F32)]
    else:
        extra, out_specs = tuple(new_h_u), [tile(blk0, 0), tile(blk0, 0)]
        out_shape = [jax.ShapeDtypeStruct(a.shape, a.dtype) for a in new_h_u]
        first_extra = len(in_specs) + len(alias_specs)
        aliases = {**aliases, first_extra: 0, first_extra + 1: 1}
    outs = pl.pallas_call(
        functools.partial(_mix_out_sample_kernel, layer, nb),
        grid=(n_tiles + 1,),
        in_specs=in_specs + alias_specs + [pl.BlockSpec(memory_space=pl.ANY)] * len(extra),
        out_specs=out_specs + st_specs,
        out_shape=out_shape + st_shapes,
        input_output_aliases=aliases,
        scratch_shapes=[pltpu.VMEM((N_KV_HEADS, KEYS, GROUP_ROWS), F32),
                        pltpu.VMEM((rows, GLA_DK), F32),
                        pltpu.VMEM((rows, GLA_DK), F32),
                        pltpu.VMEM((rows // CHUNK, GLA_HEADS, GLA_HK, GLA_HV), BF16),
                        pltpu.VMEM((2, rows, ATTN_WIDTH), BF16),
                        pltpu.VMEM((2, rows, GLA_DV), BF16),
                        pltpu.VMEM((rows, D_MODEL), BF16),
                        pltpu.VMEM((rows, D_MODEL), F32)],
        compiler_params=pltpu.CompilerParams(
            dimension_semantics=("arbitrary",), vmem_limit_bytes=VMEM_LIMIT),
        name="mix_out_sample",
    )(*([p] * 8), p_lr, wfu, bfg, gn, sinks, cache_k, cache_v, state, p, p, h_in, wa, wg, wo, bg, npost, npre,
      *(carried or ()), *extra)
    return tuple(outs[:n_main]), tuple(outs[n_main:])


def kernel(x_prompt, x_sample, cache_k, cache_v, state_gla, norm_pre, norm_post, w_in, b_gate,
           attn_sinks, w_forget_up, b_forget, gla_norm, w_branch_attn, w_branch_gla, w_out):
    batch, seq, _ = x_prompt.shape
    dec_batch, dec_seq, _ = x_sample.shape
    assert dec_seq == CHUNK and seq % (PROMPT_CHUNKS_PER_STEP * CHUNK) == 0
    assert cache_k.shape[2] == WINDOW
    n_prompt = batch * seq
    n_sample = dec_batch * dec_seq

    w_in_t = jnp.swapaxes(w_in, 1, 2)
    wa =w_branch_attn.astype(BF16)
    wg = w_branch_gla.astype(BF16)
    wo = w_out.astype(BF16)
    wfu = jnp.pad(w_forget_up, ((0, 0), (0, LR_PAD - GLA_RANK), (0, 0))).astype(BF16)
    bfg = b_forget.reshape(DEPTH, 1, GLA_DK)
    gn = gla_norm.reshape(DEPTH, 1, GLA_HV)
    bg = b_gate.reshape(DEPTH, 1, 2 * D_MODEL)
    npre = norm_pre.reshape(DEPTH, 1, D_MODEL)
    npost = norm_post.reshape(DEPTH, 1, D_MODEL)
    ck = cache_k.reshape(DEPTH, dec_batch, WINDOW, KV_WIDTH)
    cv = cache_v.reshape(DEPTH, dec_batch, WINDOW, KV_WIDTH)

    xp = x_prompt.reshape(n_prompt, D_MODEL)
    xs = x_sample.reshape(n_sample, D_MODEL)
    u = _prenorm(xp, xs, npre[0])
    h_prompt, h_sample, sample_row0 = xp, xs, 0

    stacked_p = stacked_s = None
    for l in range(DEPTH):
        p, p_lr = _project(u, w_in_t, l)
        out_p, stacked_p = _mix_out_prompt(p, p_lr, h_prompt, n_prompt + n_sample, wfu, bfg, gn, attn_sinks,
                                           wa, wg, wo, bg, npost, npre, l, batch, seq, stacked_p)
        out_s, stacked_s = _mix_out_sample(p, p_lr, h_sample, sample_row0, out_p, wfu, bfg, gn, attn_sinks,
                                           ck, cv, state_gla, wa, wg, wo, bg, npost, npre, l, dec_batch, stacked_s)
        if l < DEPTH - 1:
            h_prompt = h_sample = out_s[0]
            u = out_s[1]
            sample_row0 = n_prompt

    y_prompt = out_p[0].reshape(batch, seq, D_MODEL)
    y_sample = out_s[0].reshape(dec_batch, dec_seq, D_MODEL)
    heads = lambda a: a.reshape(a.shape[:3] + (N_KV_HEADS, HEAD_DIM))
    return (y_prompt, y_sample, heads(stacked_p[1]), heads(stacked_p[2]), stacked_p[0],
            heads(stacked_s[1]), heads(stacked_s[2]), stacked_s[0])
```

```python
import functools
import math

import jax
import jax.numpy as jnp
from jax import lax
from jax.experimental import pallas as pl
from jax.experimental.pallas import tpu as pltpu

F32 = jnp.float32
BF16 = jnp.bfloat16

D_MODEL = 2048
DEPTH = 4
CHUNK = 64
WINDOW = 128
HEAD_DIM = 64
N_Q_HEADS = 16
N_KV_HEADS = 4
GQA_GROUP = N_Q_HEADS // N_KV_HEADS
ATTN_WIDTH = N_Q_HEADS * HEAD_DIM
KV_WIDTH = N_KV_HEADS * HEAD_DIM
GLA_HEADS = 4
GLA_DK = 512
GLA_DV = 1024
GLA_HK = GLA_DK // GLA_HEADS
GLA_HV = GLA_DV // GLA_HEADS
GLA_RANK = 16
GLA_TAU = 16.0
NORM_EPS = 1e-6
LOG2E = math.log2(math.e)
IN_SIZES = (ATTN_WIDTH, KV_WIDTH, KV_WIDTH, ATTN_WIDTH,
            GLA_DK, GLA_DK, GLA_DV, GLA_DV, GLA_RANK, D_MODEL, D_MODEL)

KEYS = WINDOW + CHUNK
SUB = 8
NSUB = CHUNK // SUB

TM_PROJ = 3072
TM_PROJ_SUB = 512
TN_PROJ = 512
OUT_COLS = 256
TM_NORM = 512
PROMPT_CHUNKS_PER_STEP = 4
SAMPLE_SEQS_PER_STEP = 2

VMEM_LIMIT = 56 * 1024 * 1024

SRC_COLS = {}
_acc = 0
for _name, _size in zip(("qa", "ka", "va", "za", "qg", "kg", "vg", "zg", "lr", "ga", "gg"), IN_SIZES):
    SRC_COLS[_name] = (_acc, _size)
    _acc += _size
IN_COLS = _acc

PACKED_ORDER = ("ga", "gg", "qa", "za", "vg", "zg", "qg", "kg", "ka", "va")
PACKED_COL = {}
PACKED_SRC_ROWS = []
_acc = 0
for _name in PACKED_ORDER:
    _start, _size = SRC_COLS[_name]
    PACKED_COL[_name] = _acc
    if _acc % TN_PROJ == 0:
        PACKED_SRC_ROWS += [_start + k for k in range(0, max(_size, TN_PROJ), TN_PROJ)]
    _acc += _size
PACKED_COLS = _acc
COL_GA, COL_GG, COL_QA, COL_ZA, COL_VG, COL_ZG, COL_QG, COL_KG, COL_KA, COL_VA = (
    PACKED_COL[_n] for _n in PACKED_ORDER)
LR_PAD = 128
assert SRC_COLS["va"][0] == SRC_COLS["ka"][0] + KV_WIDTH and 2 * KV_WIDTH == TN_PROJ
assert len(PACKED_SRC_ROWS) * TN_PROJ == PACKED_COLS
assert all(_r % GLA_RANK == 0 for _r in PACKED_SRC_ROWS)


def _rms(x, g):
    return x * lax.rsqrt(jnp.mean(x * x, axis=-1, keepdims=True) + NORM_EPS) * g


def _two_group_index_maps(n_prompt_tiles):
    p_idx = lambda i: (jnp.minimum(i, n_prompt_tiles - 1), 0)
    s_idx = lambda i: (jnp.maximum(i - n_prompt_tiles, 0), 0)
    return p_idx, s_idx


def _prenorm_kernel(n_prompt_tiles, xp_ref, xs_ref, g_ref, u_ref):
    x = jnp.where(pl.program_id(0) < n_prompt_tiles, xp_ref[...], xs_ref[...])
    u_ref[...] = _rms(x, g_ref[...]).astype(BF16)


def _prenorm(x_prompt, x_sample, g):
    n_prompt_tiles = x_prompt.shape[0] // TM_NORM
    n_tiles = n_prompt_tiles + x_sample.shape[0] // TM_NORM
    p_idx, s_idx = _two_group_index_maps(n_prompt_tiles)
    return pl.pallas_call(
        functools.partial(_prenorm_kernel, n_prompt_tiles),
        grid=(n_tiles,),
        in_specs=[pl.BlockSpec((TM_NORM, D_MODEL), p_idx),
                  pl.BlockSpec((TM_NORM, D_MODEL), s_idx),
                  pl.BlockSpec((1, D_MODEL), lambda i: (0, 0))],
        out_specs=pl.BlockSpec((TM_NORM, D_MODEL), lambda i: (i, 0)),
        out_shape=jax.ShapeDtypeStruct((n_tiles * TM_NORM, D_MODEL), BF16),
        compiler_params=pltpu.CompilerParams(dimension_semantics=("arbitrary",)),
        name="prenorm",
    )(x_prompt, x_sample, g)


def _nt_dot(a, b):
    return lax.dot_general(a, b, (((1,), (1,)), ((), ())), preferred_element_type=F32)


def _proj_kernel(src_rows_ref, u_ref, w_ref, wlr_ref, o_ref, olr_ref):
    del src_rows_ref
    w = w_ref[...].astype(BF16)
    for m in range(TM_PROJ // TM_PROJ_SUB):
        rows = slice(m * TM_PROJ_SUB, (m + 1) * TM_PROJ_SUB)
        o_ref[rows, :] = _nt_dot(u_ref[rows, :], w).astype(BF16)

    @pl.when(pl.program_id(1) == 0)
    def _():
        wlr = wlr_ref[...].astype(BF16)
        for m in range(TM_PROJ // TM_PROJ_SUB):
            rows = slice(m * TM_PROJ_SUB, (m + 1) * TM_PROJ_SUB)
            olr_ref[rows, :] = _nt_dot(u_ref[rows, :], wlr).astype(BF16)


def _project(u, w_in_t, layer):
    t = u.shape[0]
    lr_row = SRC_COLS["lr"][0]
    wspec = lambda rows, index_map: pl.BlockSpec((None, pl.Element(rows), pl.Element(D_MODEL)), index_map)
    grid_spec = pltpu.PrefetchScalarGridSpec(
        num_scalar_prefetch=1,
        grid=(t // TM_PROJ, PACKED_COLS // TN_PROJ),
        in_specs=[pl.BlockSpec((TM_PROJ, D_MODEL), lambda i, j, src: (i, 0)),
                  wspec(TN_PROJ, lambda i, j, src: (layer, src[j] * GLA_RANK, 0)),
                  wspec(LR_PAD, lambda i, j, src: (layer, lr_row, 0))],
        out_specs=[pl.BlockSpec((TM_PROJ, TN_PROJ), lambda i, j, src: (i, j)),
                   pl.BlockSpec((TM_PROJ, LR_PAD), lambda i, j, src: (i, 0))])
    return pl.pallas_call(
        _proj_kernel,
        grid_spec=grid_spec,
        out_shape=[jax.ShapeDtypeStruct((t, PACKED_COLS), BF16),
                   jax.ShapeDtypeStruct((t, LR_PAD), BF16)],
        compiler_params=pltpu.CompilerParams(
            dimension_semantics=("arbitrary", "arbitrary"), vmem_limit_bytes=VMEM_LIMIT),
        name="proj",
    )(jnp.asarray([r // GLA_RANK for r in PACKED_SRC_ROWS], jnp.int32), u, w_in_t, w_in_t)


def _run(stages):
    for _ in stages:
        pass


def _pipelined(tasks, lag):
    started = []
    for task in list(tasks) + [None] * lag:
        if task is not None:
            next(task)
            started.append(task)
        if len(started) > lag or task is None:
            _run(started.pop(0))
            yield


def _interleave(major, n_major, minor, n_minor):
    done = 0
    for i in range(n_major):
        while done < n_minor and done * n_major < (i + 1) * n_minor:
            next(minor)
            done += 1
        next(major)
    _run(minor)
    _run(major)


def _silu(x):
    half = 0.5 * x
    return half + half * jnp.tanh(half)


def _log_sigmoid(x):
    return jnp.minimum(x, 0.0) - jnp.log(1.0 + jnp.exp(-jnp.abs(x)))


def _alibi_slope(h):
    return math.pow(2.0, -8.0 * (h + 1) / N_Q_HEADS)


HEAD_LAG = 1
GROUP_ROWS = GQA_GROUP * CHUNK


def _fill_alibi(bias_ref):
    s = lax.broadcasted_iota(jnp.int32, (KEYS, CHUNK), 0)
    t = lax.broadcasted_iota(jnp.int32, (KEYS, CHUNK), 1)
    dist = jnp.abs(WINDOW + t - s).astype(F32)
    for kh in range(N_KV_HEADS):
        for g in range(GQA_GROUP):
            bias_ref[kh, :, g * CHUNK:(g + 1) * CHUNK] = dist * (-_alibi_slope(kh * GQA_GROUP + g))


def _attention_chunk(r, qa_ref, za_ref, kwin, vwin, sinks_ref, layer, bias_ref, key_lo, xa_ref):
    rows = pl.ds(r, CHUNK)
    lane_head = lax.broadcasted_iota(jnp.int32, (1, GROUP_ROWS), 1) // CHUNK
    if key_lo is not None:
        key_ok = lax.broadcasted_iota(jnp.int32, (KEYS, GROUP_ROWS), 0) >= key_lo

    def head(kh):
        k = kwin(pl.ds(kh * HEAD_DIM, HEAD_DIM))
        v = vwin(pl.ds(kh * HEAD_DIM, HEAD_DIM))
        q = jnp.concatenate(
            [qa_ref[rows, pl.ds((kh * GQA_GROUP + g) * HEAD_DIM, HEAD_DIM)] for g in range(GQA_GROUP)],
            axis=0)
        q = q * jnp.asarray(HEAD_DIM ** -0.5, BF16)
        s = _nt_dot(k, q) + bias_ref[kh]
        if key_lo is not None:
            s = jnp.where(key_ok, s, -jnp.inf)
        sink = jnp.zeros((1, GROUP_ROWS), F32)
        for g in range(GQA_GROUP):
            sink = jnp.where(lane_head == g, sinks_ref[layer, kh * GQA_GROUP + g], sink)
        m = jnp.maximum(jnp.max(s, axis=0, keepdims=True), sink)
        p = jnp.exp(s - m)
        den = jnp.sum(p, axis=0, keepdims=True) + jnp.exp(sink - m)
        p = p.astype(BF16)
        yield
        o_t = lax.dot_general(v, p, (((0,), (0,)), ((), ())), preferred_element_type=F32)
        o = jnp.transpose(o_t / den)
        for g in range(GQA_GROUP):
            h = kh * GQA_GROUP + g
            cols = pl.ds(h * HEAD_DIM, HEAD_DIM)
            z = za_ref[rows, cols].astype(F32)
            xa_ref[rows, cols] = (o[g * CHUNK:(g + 1) * CHUNK] * _silu(z)).astype(BF16)
        yield

    return [head(kh) for kh in range(N_KV_HEADS)]


def _gla_prepare(lr_ref, kg_ref, wfu_ref, bf_ref, k_scr, b_scr):
    n = lr_ref.shape[0]
    la2 = _log2_forget(lr_ref[...], wfu_ref, bf_ref)
    row = lax.broadcasted_iota(jnp.int32, (n, n), 0)
    col = lax.broadcasted_iota(jnp.int32, (n, n), 1)
    tri = jnp.where(row // CHUNK == col // CHUNK, row - col, -1) >= 0
    tri = jnp.where(tri, 1.0, 0.0).astype(BF16)
    la_hi = la2.astype(BF16)
    rem = la2 - la_hi.astype(F32)
    la_mid = rem.astype(BF16)
    la_lo = (rem - la_mid.astype(F32)).astype(BF16)
    b_scr[...] = (jnp.dot(tri, la_hi, preferred_element_type=F32)
                  + jnp.dot(tri, la_mid, preferred_element_type=F32)
                  + jnp.dot(tri, la_lo, preferred_element_type=F32))
    k_scr[...] = kg_ref[...].astype(F32)


def _gla_states(chunk_states, vg_ref, k_scr, b_scr, s0_scr):
    for hh in range(GLA_HEADS):
        kcols = pl.ds(hh * GLA_HK, GLA_HK)
        vcols = pl.ds(hh * GLA_HV, GLA_HV)
        steps = []
        for c in range(len(chunk_states)):
            rows = pl.ds(c * CHUNK, CHUNK)
            b_end = b_scr[c * CHUNK + CHUNK - 1:(c + 1) * CHUNK, kcols]
            k_til = (k_scr[rows, kcols] * jnp.exp2(b_end - b_scr[rows, kcols])).astype(BF16)
            upd = lax.dot_general(k_til, vg_ref[rows, vcols], (((0,), (0,)), ((), ())),
                                  preferred_element_type=F32)
            e_col = jnp.transpose(jnp.broadcast_to(jnp.exp2(b_end), (GLA_HK, GLA_HK)))
            steps.append((jnp.concatenate([e_col, e_col], axis=1), upd))
            yield
        s = None
        for c, ((in_ref, out_ref), (decay, upd)) in enumerate(zip(chunk_states, steps)):
            if c == 0 or in_ref is not chunk_states[c - 1][1]:
                s = in_ref[hh]
            s0_scr[c, hh] = s.astype(BF16)
            s = s * decay + upd
            if c + 1 == len(chunk_states) or chunk_states[c + 1][0] is not out_ref:
                out_ref[hh] = s
        yield


def _gla_chunk(c, qg_ref, vg_ref, zg_ref, s0_scr, gnorm, k_scr, b_scr, xg_ref):
    r = c * CHUNK
    rows = pl.ds(r, CHUNK)
    lane = lax.broadcasted_iota(jnp.int32, (SUB, CHUNK), 1)
    row0 = lax.broadcasted_iota(jnp.int32, (SUB, CHUNK), 0)

    def head(hh):
        kcols = pl.ds(hh * GLA_HK, GLA_HK)
        vcols = pl.ds(hh * GLA_HV, GLA_HV)
        q = qg_ref[rows, kcols].astype(F32) * (GLA_HK ** -0.5)
        k = k_scr[rows, kcols]
        v = vg_ref[rows, vcols]
        b = b_scr[rows, kcols]
        blk = lambda x, j: x[j * SUB:(j + 1) * SUB]
        b_row = lambda t: b_scr[r + t:r + t + 1, kcols]
        b_last = [b_row(j * SUB + SUB - 1) for j in range(NSUB)]

        k_hat = jnp.concatenate([blk(k, j) * jnp.exp2(b_last[j] - blk(b, j)) for j in range(NSUB)], axis=0)
        lhs = jnp.concatenate([q[(j + 1) * SUB:] * jnp.exp2(b[(j + 1) * SUB:] - b_last[j])
                               for j in range(NSUB - 1)], axis=0)
        rr = _nt_dot(lhs.astype(BF16), k_hat.astype(BF16))
        o_inter = jnp.dot((q * jnp.exp2(b)).astype(BF16), s0_scr[c, hh], preferred_element_type=F32)
        yield

        score_rows = []
        for i in range(NSUB):
            acc = jnp.zeros((SUB, CHUNK), F32)
            for j in range(i):
                base = sum((NSUB - 1 - jj) * SUB for jj in range(j)) + (i - j - 1) * SUB
                acc = jnp.where(lane // SUB == j, rr[base:base + SUB], acc)
            qi, bi = blk(q, i), blk(b, i)
            for s in range(i * SUB, (i + 1) * SUB):
                prod = (qi * k_scr[r + s:r + s + 1, kcols]) * jnp.exp2(bi - b_row(s))
                acc = jnp.where(lane == s, jnp.sum(prod, axis=1, keepdims=True), acc)
            score_rows.append(jnp.where(lane <= row0 + i * SUB, acc, 0.0))
        scores = jnp.concatenate(score_rows, axis=0).astype(BF16)

        o = jnp.dot(scores, v, preferred_element_type=F32) + o_inter

        y = _rms(o, gnorm)
        z = zg_ref[rows, vcols].astype(F32)
        xg_ref[rows, vcols] = (y * _silu(z)).astype(BF16)
        yield

    return [head(hh) for hh in range(GLA_HEADS)]


def _log2_forget(lr, wfu_ref, bf_ref):
    x = jnp.dot(lr, wfu_ref[...], preferred_element_type=F32) + bf_ref[...]
    return _log_sigmoid(x) * (LOG2E / GLA_TAU)


def _sigmoid(x):
    return 1.0 / (1.0 + jnp.exp(-x))


def _swap_mixer_outputs(s, xa_buf, xg_buf):
    @pl.when(s == 0)
    def _():
        xa_buf[1] = jnp.zeros(xa_buf.shape[1:], BF16)
        xg_buf[1] = jnp.zeros(xg_buf.shape[1:], BF16)

    return xa_buf.at[s % 2], xg_buf.at[s % 2], xa_buf[1 - s % 2], xg_buf[1 - s % 2]


N_OUT_STAGES = 3 * (D_MODEL // OUT_COLS) + 1


def _out_stages(xa, xg, ga_ref, gg_ref, h_in_ref, wa_ref, wg_ref, wo_ref, bg_ref, npost_ref, npre_ref,
                merged_scr, z_scr, out_refs):
    bg = bg_ref[...]
    for n in range(D_MODEL // OUT_COLS):
        cols = slice(n * OUT_COLS, (n + 1) * OUT_COLS)
        ya = jnp.dot(xa, wa_ref[:, cols], preferred_element_type=F32)
        yield
        yg = jnp.dot(xg, wg_ref[:, cols], preferred_element_type=F32)
        gate_a = _sigmoid(ga_ref[:, cols].astype(F32) + bg[:, n * OUT_COLS:(n + 1) * OUT_COLS])
        gate_g = _sigmoid(gg_ref[:, cols].astype(F32)
                          + bg[:, D_MODEL + n * OUT_COLS:D_MODEL + (n + 1) * OUT_COLS])
        merged_scr[:, cols] = (gate_a * ya + gate_g * yg).astype(BF16)
        yield
    for n in range(D_MODEL // OUT_COLS):
        cols = slice(n * OUT_COLS, (n + 1) * OUT_COLS)
        z_scr[:, cols] = jnp.dot(merged_scr[...], wo_ref[:, cols], preferred_element_type=F32)
        yield
    h = h_in_ref[...] + _rms(z_scr[...], npost_ref[...])
    out_refs[0][...] = h
    if npre_ref is not None:
        out_refs[1][...] = _rms(h, npre_ref[...]).astype(BF16)
    yield


def _mix_out_prompt_kernel(layer, n_steps, chunks, *refs):
    last = layer == DEPTH - 1
    n_carried = 3 if layer > 0 else 0
    (qa_ref, za_ref, vg_ref, zg_ref, qg_ref, kg_ref, ka_ref, va_ref, lr_ref,
     wfu_ref, bf_ref, gn_ref, sinks_ref,
     ga_ref, gg_ref, h_in_ref, wa_ref, wg_ref, wo_ref, bg_ref, npost_ref, npre_ref) = refs[:22]
    refs = refs[22 + n_carried:]
    out_refs, refs = refs[:1 if last else 2], refs[1 if last else 2:]
    (st_ref, knew_ref, vnew_ref,
     halo_k, halo_v, state, bias_scr, k_scr, b_scr, s0_scr, xa_buf, xg_buf, merged_scr, z_scr) = refs
    s = pl.program_id(0)
    b = s // n_steps
    t = s % n_steps
    assert chunks * CHUNK >= WINDOW
    xa_ref, xg_ref, xa_prev, xg_prev = _swap_mixer_outputs(s, xa_buf, xg_buf)
    out_stages = functools.partial(
        _out_stages, xa_prev, xg_prev, ga_ref, gg_ref, h_in_ref, wa_ref, wg_ref, wo_ref, bg_ref, npost_ref,
        None if last else npre_ref, merged_scr, z_scr, out_refs)

    cur = t % 2
    nxt = 1 - cur

    @pl.when((b == 0) & (t == 0))
    def _():
        _fill_alibi(bias_scr)

    @pl.when(t == 0)
    def _():
        halo_k[0] = jnp.zeros((WINDOW, KV_WIDTH), BF16)
        halo_v[0] = jnp.zeros((WINDOW, KV_WIDTH), BF16)
        state[...] = jnp.zeros(state.shape, F32)

    def window(halo, tile_ref, c, cols):
        lo = c * CHUNK - WINDOW
        if lo >= 0:
            return tile_ref[lo:lo + KEYS, cols]
        return jnp.concatenate([halo[cur, WINDOW + lo:, cols], tile_ref[0:(c + 1) * CHUNK, cols]], axis=0)

    def mix_stages():
        _gla_prepare(lr_ref, kg_ref, wfu_ref, bf_ref, k_scr, b_scr)
        yield
        yield from _gla_states([(state, state)] * chunks, vg_ref, k_scr, b_scr, s0_scr)
        gnorm = gn_ref[...]
        heads = []
        for c in range(chunks):
            key_lo = (WINDOW // CHUNK - (t * chunks + c)) * CHUNK if c < WINDOW // CHUNK else None
            heads += _attention_chunk(c * CHUNK, qa_ref, za_ref, functools.partial(window, halo_k, ka_ref, c),
                                      functools.partial(window, halo_v, va_ref, c), sinks_ref, layer, bias_scr,
                                      key_lo, xa_ref)
            heads += _gla_chunk(c, qg_ref, vg_ref, zg_ref, s0_scr, gnorm, k_scr, b_scr, xg_ref)
        yield from _pipelined(heads, HEAD_LAG)
        halo_k[nxt] = ka_ref[chunks * CHUNK - WINDOW:, :]
        halo_v[nxt] = va_ref[chunks * CHUNK - WINDOW:, :]

    n_mix = 1 + GLA_HEADS * (chunks + 1) + chunks * (N_KV_HEADS + GLA_HEADS)
    _interleave(mix_stages(), n_mix, out_stages(), N_OUT_STAGES)

    @pl.when(t == n_steps - 1)
    def _():
        st_ref[0] = state[...]
        knew_ref[0] = ka_ref[chunks * CHUNK - WINDOW:, :].astype(F32)
        vnew_ref[0] = va_ref[chunks * CHUNK - WINDOW:, :].astype(F32)


def _mix_out_sample_kernel(layer, n_seqs, *refs):
    last = layer == DEPTH - 1
    n_untouched = (3 if layer > 0 else 0) + (0 if last else 2)
    (qa_ref, za_ref, vg_ref, zg_ref, qg_ref, kg_ref, ka_ref, va_ref, lr_ref,
     wfu_ref, bf_ref, gn_ref, sinks_ref, ck_ref, cv_ref, s0_ref,
     ga_ref, gg_ref, h_in_ref, wa_ref, wg_ref, wo_ref, bg_ref, npost_ref, npre_ref) = refs[:25]
    refs = refs[25 + n_untouched:]
    out_refs, refs = refs[:1 if last else 2], refs[1 if last else 2:]
    (st_ref, knew_ref, vnew_ref,
     bias_scr, k_scr, b_scr, s0_scr, xa_buf, xg_buf, merged_scr, z_scr) = refs
    s = pl.program_id(0)
    xa_ref, xg_ref, xa_prev, xg_prev = _swap_mixer_outputs(s, xa_buf, xg_buf)
    out_stages = _out_stages(xa_prev, xg_prev, ga_ref, gg_ref, h_in_ref, wa_ref, wg_ref, wo_ref, bg_ref,
                             npost_ref, None if last else npre_ref, merged_scr, z_scr, out_refs)

    @pl.when(s == 0)
    def _():
        _fill_alibi(bias_scr)

    def window(cache_ref, new_ref, g, cols):
        return jnp.concatenate([cache_ref[g, :, cols].astype(BF16), new_ref[g * CHUNK:(g + 1) * CHUNK, cols]], axis=0)

    def mix_stages():
        _gla_prepare(lr_ref, kg_ref, wfu_ref, bf_ref, k_scr, b_scr)
        yield
        yield from _gla_states([(s0_ref.at[g], st_ref.at[g]) for g in range(n_seqs)], vg_ref, k_scr, b_scr, s0_scr)
        gnorm = gn_ref[...]
        heads = []
        for g in range(n_seqs):
            r = g * CHUNK
            knew_ref[g, 0:WINDOW - CHUNK, :] = ck_ref[g, CHUNK:, :]
            knew_ref[g, WINDOW - CHUNK:, :] = ka_ref[r:r + CHUNK, :].astype(F32)
            vnew_ref[g, 0:WINDOW - CHUNK, :] = cv_ref[g, CHUNK:, :]
            vnew_ref[g, WINDOW - CHUNK:, :] = va_ref[r:r + CHUNK, :].astype(F32)
            heads += _attention_chunk(r, qa_ref, za_ref, functools.partial(window, ck_ref, ka_ref, g),
                                      functools.partial(window, cv_ref, va_ref, g), sinks_ref, layer, bias_scr,
                                      None, xa_ref)
            heads += _gla_chunk(g, qg_ref, vg_ref, zg_ref, s0_scr, gnorm, k_scr, b_scr, xg_ref)
        yield from _pipelined(heads, HEAD_LAG)

    n_mix = 1 + GLA_HEADS * (n_seqs + 1) + n_seqs * (N_KV_HEADS + GLA_HEADS)
    _interleave(mix_stages(), n_mix, out_stages, N_OUT_STAGES)


def _section_specs(rows, row_index):
    def spec(width, col):
        return pl.BlockSpec((rows, width), functools.partial(
            lambda cb, *g: (row_index(*g), cb), col // width))
    return [spec(ATTN_WIDTH, COL_QA), spec(ATTN_WIDTH, COL_ZA), spec(GLA_DV, COL_VG), spec(GLA_DV, COL_ZG),
            spec(GLA_DK, COL_QG), spec(GLA_DK, COL_KG), spec(KV_WIDTH, COL_KA), spec(KV_WIDTH, COL_VA),
            spec(LR_PAD, 0)]


def _layer_weight_specs(layer):
    zeros = (0,) * 2
    return [pl.BlockSpec((None, LR_PAD, GLA_DK), lambda *g: (layer,) + zeros),
            pl.BlockSpec((None, 1, GLA_DK), lambda *g: (layer,) + zeros),
            pl.BlockSpec((None, 1, GLA_HV), lambda *g: (layer,) + zeros),
            pl.BlockSpec(memory_space=pltpu.SMEM)]


def _stacked_outputs(layer, batch, seqs_per_step, batch_block, n_inputs, first_out, carried):
    lead = lambda *g: (layer, batch_block(*g))
    nb = seqs_per_step
    specs = [pl.BlockSpec((None, nb, GLA_HEADS, GLA_HK, GLA_HV), lambda *g: lead(*g) + (0, 0, 0)),
             pl.BlockSpec((None, nb, WINDOW, KV_WIDTH), lambda *g: lead(*g) + (0, 0)),
             pl.BlockSpec((None, nb, WINDOW, KV_WIDTH), lambda *g: lead(*g) + (0, 0))]
    shapes = [jax.ShapeDtypeStruct((DEPTH, batch, GLA_HEADS, GLA_HK, GLA_HV), F32),
              jax.ShapeDtypeStruct((DEPTH, batch, WINDOW, KV_WIDTH), F32),
              jax.ShapeDtypeStruct((DEPTH, batch, WINDOW, KV_WIDTH), F32)]
    if carried is None:
        return specs, shapes, [], {}
    alias_specs = [pl.BlockSpec(memory_space=pl.ANY)] * 3
    aliases = {n_inputs + k: first_out + k for k in range(3)}
    return specs, shapes, alias_specs, aliases


def _out_weight_specs(layer):
    wspec = lambda k: pl.BlockSpec((None, k, D_MODEL), lambda *g: (layer, 0, 0), pipeline_mode=pl.Buffered(1))
    vspec = lambda n, l: pl.BlockSpec((None, 1, n), lambda *g: (l, 0, 0))
    return [wspec(ATTN_WIDTH), wspec(GLA_DV), wspec(D_MODEL),
            vspec(2 * D_MODEL, layer), vspec(D_MODEL, layer), vspec(D_MODEL, min(layer + 1, DEPTH - 1))]


def _mix_out_prompt(p, p_lr, h_in, n_total, wfu, bfg, gn, sinks, wa, wg, wo, bg, npost, npre,
                    layer, batch, seq, carried):
    last = layer == DEPTH - 1
    chunks = PROMPT_CHUNKS_PER_STEP
    rows = chunks * CHUNK
    n_steps = seq // rows
    n_tiles = batch * n_steps
    mix_tile = lambda s: jnp.minimum(s, n_tiles - 1)
    out_tile = lambda s: jnp.maximum(s - 1, 0)
    tile = lambda col: pl.BlockSpec((rows, D_MODEL), lambda s: (out_tile(s), col))
    in_specs = (_section_specs(rows, mix_tile) + _layer_weight_specs(layer)
                + [tile(COL_GA // D_MODEL), tile(COL_GG // D_MODEL), tile(0)] + _out_weight_specs(layer))
    n_main = 1 if last else 2
    st_specs, st_shapes, alias_specs, aliases = _stacked_outputs(
        layer, batch, 1, lambda s: jnp.minimum(s // n_steps, batch - 1), len(in_specs), n_main, carried)
    if last:
        out_shape = [jax.ShapeDtypeStruct((n_tiles * rows, D_MODEL), F32)]
    else:
        out_shape = [jax.ShapeDtypeStruct((n_total, D_MODEL), F32), jax.ShapeDtypeStruct((n_total, D_MODEL), BF16)]
    outs = pl.pallas_call(
        functools.partial(_mix_out_prompt_kernel, layer, n_steps, chunks),
        grid=(n_tiles + 1,),
        in_specs=in_specs + alias_specs,
        out_specs=[tile(0)] * n_main + st_specs,
        out_shape=out_shape + st_shapes,
        input_output_aliases=aliases,
        scratch_shapes=[pltpu.VMEM((2, WINDOW, KV_WIDTH), BF16),
                        pltpu.VMEM((2, WINDOW, KV_WIDTH), BF16),
                        pltpu.VMEM((GLA_HEADS, GLA_HK, GLA_HV), F32),
                        pltpu.VMEM((N_KV_HEADS, KEYS, GROUP_ROWS), F32),
                        pltpu.VMEM((rows, GLA_DK), F32),
                        pltpu.VMEM((rows, GLA_DK), F32),
                        pltpu.VMEM((rows // CHUNK, GLA_HEADS, GLA_HK, GLA_HV), BF16),
                        pltpu.VMEM((2, rows, ATTN_WIDTH), BF16),
                        pltpu.VMEM((2, rows, GLA_DV), BF16),
                        pltpu.VMEM((rows, D_MODEL), BF16),
                        pltpu.VMEM((rows, D_MODEL), F32)],
        compiler_params=pltpu.CompilerParams(
            dimension_semantics=("arbitrary",), vmem_limit_bytes=VMEM_LIMIT),
        name="mix_out_prompt",
    )(*([p] * 8), p_lr, wfu, bfg, gn, sinks, p, p, h_in, wa, wg, wo, bg, npost, npre, *(carried or ()))
    return tuple(outs[:n_main]), tuple(outs[n_main:])


def _mix_out_sample(p, p_lr, h_in, row0_h, new_h_u, wfu, bfg, gn, sinks, cache_k, cache_v, state,
                    wa, wg, wo, bg, npost, npre, layer, batch, carried):
    last = layer == DEPTH - 1
    nb = SAMPLE_SEQS_PER_STEP
    rows = nb * CHUNK
    n_tiles = batch // nb
    row0 = p.shape[0] - batch * CHUNK
    assert row0 % rows == 0 and row0_h % rows == 0 and batch % nb == 0
    blk0, blk0_h = row0 // rows, row0_h // rows
    mix_tile = lambda s: jnp.minimum(s, n_tiles - 1)
    out_tile = lambda s: jnp.maximum(s - 1, 0)
    tile = lambda base, col: pl.BlockSpec((rows, D_MODEL), lambda s: (base + out_tile(s), col))
    cache_spec = pl.BlockSpec((None, nb, WINDOW, KV_WIDTH), lambda s: (layer, mix_tile(s), 0, 0))
    state_spec = pl.BlockSpec((None, nb, GLA_HEADS, GLA_HK, GLA_HV), lambda s: (layer, mix_tile(s), 0, 0, 0))
    in_specs = (_section_specs(rows, lambda s: blk0 + mix_tile(s)) + _layer_weight_specs(layer)
                + [cache_spec, cache_spec, state_spec]
                + [tile(blk0, COL_GA // D_MODEL), tile(blk0, COL_GG // D_MODEL), tile(blk0_h, 0)]
                + _out_weight_specs(layer))
    n_main = 1 if last else 2
    st_specs, st_shapes, alias_specs, aliases = _stacked_outputs(
        layer, batch, nb, mix_tile, len(in_specs), n_main, carried)
    if last:
        extra, out_specs = (), [tile(0, 0)]
        out_shape = [jax.ShapeDtypeStruct((batch * CHUNK, D_MODEL), F32)]
    else:
        extra, out_specs = tuple(new_h_u), [tile(blk0, 0), tile(blk0, 0)]
        out_shape = [jax.ShapeDtypeStruct(a.shape, a.dtype) for a in new_h_u]
        first_extra = len(in_specs) + len(alias_specs)
        aliases = {**aliases, first_extra: 0, first_extra + 1: 1}
    outs = pl.pallas_call(
        functools.partial(_mix_out_sample_kernel, layer, nb),
        grid=(n_tiles + 1,),
        in_specs=in_specs + alias_specs + [pl.BlockSpec(memory_space=pl.ANY)] * len(extra),
        out_specs=out_specs + st_specs,
        out_shape=out_shape + st_shapes,
        input_output_aliases=aliases,
        scratch_shapes=[pltpu.VMEM((N_KV_HEADS, KEYS, GROUP_ROWS), F32),
                        pltpu.VMEM((rows, GLA_DK), F32),
                        pltpu.VMEM((rows, GLA_DK), F32),
                        pltpu.VMEM((rows // CHUNK, GLA_HEADS, GLA_HK, GLA_HV), BF16),
                        pltpu.VMEM((2, rows, ATTN_WIDTH), BF16),
                        pltpu.VMEM((2, rows, GLA_DV), BF16),
                        pltpu.VMEM((rows, D_MODEL), BF16),
                        pltpu.VMEM((rows, D_MODEL), F32)],
        compiler_params=pltpu.CompilerParams(
            dimension_semantics=("arbitrary",), vmem_limit_bytes=VMEM_LIMIT),
        name="mix_out_sample",
    )(*([p] * 8), p_lr, wfu, bfg, gn, sinks, cache_k, cache_v, state, p, p, h_in, wa, wg, wo, bg, npost, npre,
      *(carried or ()), *extra)
    return tuple(outs[:n_main]), tuple(outs[n_main:])


def kernel(x_prompt, x_sample, cache_k, cache_v, state_gla, norm_pre, norm_post, w_in, b_gate,
           attn_sinks, w_forget_up, b_forget, gla_norm, w_branch_attn, w_branch_gla, w_out):
    batch, seq, _ = x_prompt.shape
    dec_batch, dec_seq, _ = x_sample.shape
    assert dec_seq == CHUNK and seq % (PROMPT_CHUNKS_PER_STEP * CHUNK) == 0
    assert cache_k.shape[2] == WINDOW
    n_prompt = batch * seq
    n_sample = dec_batch * dec_seq

    w_in_t = jnp.swapaxes(w_in, 1, 2)
    wa =w_branch_attn.astype(BF16)
    wg = w_branch_gla.astype(BF16)
    wo = w_out.astype(BF16)
    wfu = jnp.pad(w_forget_up, ((0, 0), (0, LR_PAD - GLA_RANK), (0, 0))).astype(BF16)
    bfg = b_forget.reshape(DEPTH, 1, GLA_DK)
    gn = gla_norm.reshape(DEPTH, 1, GLA_HV)
    bg = b_gate.reshape(DEPTH, 1, 2 * D_MODEL)
    npre = norm_pre.reshape(DEPTH, 1, D_MODEL)
    npost = norm_post.reshape(DEPTH, 1, D_MODEL)
    ck = cache_k.reshape(DEPTH, dec_batch, WINDOW, KV_WIDTH)
    cv = cache_v.reshape(DEPTH, dec_batch, WINDOW, KV_WIDTH)

    xp = x_prompt.reshape(n_prompt, D_MODEL)
    xs = x_sample.reshape(n_sample, D_MODEL)
    u = _prenorm(xp, xs, npre[0])
    h_prompt, h_sample, sample_row0 = xp, xs, 0

    stacked_p = stacked_s = None
    for l in range(DEPTH):
        p, p_lr = _project(u, w_in_t, l)
        out_p, stacked_p = _mix_out_prompt(p, p_lr, h_prompt, n_prompt + n_sample, wfu, bfg, gn, attn_sinks,
                                           wa, wg, wo, bg, npost, npre, l, batch, seq, stacked_p)
        out_s, stacked_s = _mix_out_sample(p, p_lr, h_sample, sample_row0, out_p, wfu, bfg, gn, attn_sinks,
                                           ck, cv, state_gla, wa, wg, wo, bg, npost, npre, l, dec_batch, stacked_s)
        if l < DEPTH - 1:
            h_prompt = h_sample = out_s[0]
            u = out_s[1]
            sample_row0 = n_prompt

    y_prompt = out_p[0].reshape(batch, seq, D_MODEL)
    y_sample = out_s[0].reshape(dec_batch, dec_seq, D_MODEL)
    heads = lambda a: a.reshape(a.shape[:3] + (N_KV_HEADS, HEAD_DIM))
    return (y_prompt, y_sample, heads(stacked_p[1]), heads(stacked_p[2]), stacked_p[0],
            heads(stacked_s[1]), heads(stacked_s[2]), stacked_s[0])
```

```python
import functools
import math

import jax
import jax.numpy as jnp
from jax import lax
from jax.experimental import pallas as pl
from jax.experimental.pallas import tpu as pltpu

F32 = jnp.float32
BF16 = jnp.bfloat16

D_MODEL = 2048
DEPTH = 4
CHUNK = 64
WINDOW = 128
HEAD_DIM = 64
N_Q_HEADS = 16
N_KV_HEADS = 4
GQA_GROUP = N_Q_HEADS // N_KV_HEADS
ATTN_WIDTH = N_Q_HEADS * HEAD_DIM
KV_WIDTH = N_KV_HEADS * HEAD_DIM
GLA_HEADS = 4
GLA_DK = 512
GLA_DV = 1024
GLA_HK = GLA_DK // GLA_HEADS
GLA_HV = GLA_DV // GLA_HEADS
GLA_RANK = 16
GLA_TAU = 16.0
NORM_EPS = 1e-6
LOG2E = math.log2(math.e)
IN_SIZES = (ATTN_WIDTH, KV_WIDTH, KV_WIDTH, ATTN_WIDTH,
            GLA_DK, GLA_DK, GLA_DV, GLA_DV, GLA_RANK, D_MODEL, D_MODEL)

KEYS = WINDOW + CHUNK
SUB = 8
NSUB = CHUNK // SUB

TM_PROJ = 3072
TM_PROJ_SUB = 512
TN_PROJ = 512
OUT_COLS = 256
TM_NORM = 512
PROMPT_CHUNKS_PER_STEP = 4
SAMPLE_SEQS_PER_STEP = 2

VMEM_LIMIT = 56 * 1024 * 1024

SRC_COLS = {}
_acc = 0
for _name, _size in zip(("qa", "ka", "va", "za", "qg", "kg", "vg", "zg", "lr", "ga", "gg"), IN_SIZES):
    SRC_COLS[_name] = (_acc, _size)
    _acc += _size
IN_COLS = _acc

PACKED_ORDER = ("ga", "gg", "qa", "za", "vg", "zg", "qg", "kg", "ka", "va")
PACKED_COL = {}
PACKED_SRC_ROWS = []
_acc = 0
for _name in PACKED_ORDER:
    _start, _size = SRC_COLS[_name]
    PACKED_COL[_name] = _acc
    if _acc % TN_PROJ == 0:
        PACKED_SRC_ROWS += [_start + k for k in range(0, max(_size, TN_PROJ), TN_PROJ)]
    _acc += _size
PACKED_COLS = _acc
COL_GA, COL_GG, COL_QA, COL_ZA, COL_VG, COL_ZG, COL_QG, COL_KG, COL_KA, COL_VA = (
    PACKED_COL[_n] for _n in PACKED_ORDER)
LR_PAD = 128
assert SRC_COLS["va"][0] == SRC_COLS["ka"][0] + KV_WIDTH and 2 * KV_WIDTH == TN_PROJ
assert len(PACKED_SRC_ROWS) * TN_PROJ == PACKED_COLS
assert all(_r % GLA_RANK == 0 for _r in PACKED_SRC_ROWS)


def _rms(x, g):
    return x * lax.rsqrt(jnp.mean(x * x, axis=-1, keepdims=True) + NORM_EPS) * g


def _two_group_index_maps(n_prompt_tiles):
    p_idx = lambda i: (jnp.minimum(i, n_prompt_tiles - 1), 0)
    s_idx = lambda i: (jnp.maximum(i - n_prompt_tiles, 0), 0)
    return p_idx, s_idx


def _prenorm_kernel(n_prompt_tiles, xp_ref, xs_ref, g_ref, u_ref):
    x = jnp.where(pl.program_id(0) < n_prompt_tiles, xp_ref[...], xs_ref[...])
    u_ref[...] = _rms(x, g_ref[...]).astype(BF16)


def _prenorm(x_prompt, x_sample, g):
    n_prompt_tiles = x_prompt.shape[0] // TM_NORM
    n_tiles = n_prompt_tiles + x_sample.shape[0] // TM_NORM
    p_idx, s_idx = _two_group_index_maps(n_prompt_tiles)
    return pl.pallas_call(
        functools.partial(_prenorm_kernel, n_prompt_tiles),
        grid=(n_tiles,),
        in_specs=[pl.BlockSpec((TM_NORM, D_MODEL), p_idx),
                  pl.BlockSpec((TM_NORM, D_MODEL), s_idx),
                  pl.BlockSpec((1, D_MODEL), lambda i: (0, 0))],
        out_specs=pl.BlockSpec((TM_NORM, D_MODEL), lambda i: (i, 0)),
        out_shape=jax.ShapeDtypeStruct((n_tiles * TM_NORM, D_MODEL), BF16),
        compiler_params=pltpu.CompilerParams(dimension_semantics=("arbitrary",)),
        name="prenorm",
    )(x_prompt, x_sample, g)


def _nt_dot(a, b):
    return lax.dot_general(a, b, (((1,), (1,)), ((), ())), preferred_element_type=F32)


def _proj_kernel(src_rows_ref, u_ref, w_ref, wlr_ref, o_ref, olr_ref):
    del src_rows_ref
    w = w_ref[...].astype(BF16)
    for m in range(TM_PROJ // TM_PROJ_SUB):
        rows = slice(m * TM_PROJ_SUB, (m + 1) * TM_PROJ_SUB)
        o_ref[rows, :] = _nt_dot(u_ref[rows, :], w).astype(BF16)

    @pl.when(pl.program_id(1) == 0)
    def _():
        wlr = wlr_ref[...].astype(BF16)
        for m in range(TM_PROJ // TM_PROJ_SUB):
            rows = slice(m * TM_PROJ_SUB, (m + 1) * TM_PROJ_SUB)
            olr_ref[rows, :] = _nt_dot(u_ref[rows, :], wlr).astype(BF16)


def _project(u, w_in_t, layer):
    t = u.shape[0]
    lr_row = SRC_COLS["lr"][0]
    wspec = lambda rows, index_map: pl.BlockSpec((None, pl.Element(rows), pl.Element(D_MODEL)), index_map)
    grid_spec = pltpu.PrefetchScalarGridSpec(
        num_scalar_prefetch=1,
        grid=(t // TM_PROJ, PACKED_COLS // TN_PROJ),
        in_specs=[pl.BlockSpec((TM_PROJ, D_MODEL), lambda i, j, src: (i, 0)),
                  wspec(TN_PROJ, lambda i, j, src: (layer, src[j] * GLA_RANK, 0)),
                  wspec(LR_PAD, lambda i, j, src: (layer, lr_row, 0))],
        out_specs=[pl.BlockSpec((TM_PROJ, TN_PROJ), lambda i, j, src: (i, j)),
                   pl.BlockSpec((TM_PROJ, LR_PAD), lambda i, j, src: (i, 0))])
    return pl.pallas_call(
        _proj_kernel,
        grid_spec=grid_spec,
        out_shape=[jax.ShapeDtypeStruct((t, PACKED_COLS), BF16),
                   jax.ShapeDtypeStruct((t, LR_PAD), BF16)],
        compiler_params=pltpu.CompilerParams(
            dimension_semantics=("arbitrary", "arbitrary"), vmem_limit_bytes=VMEM_LIMIT),
        name="proj",
    )(jnp.asarray([r // GLA_RANK for r in PACKED_SRC_ROWS], jnp.int32), u, w_in_t, w_in_t)


def _run(stages):
    for _ in stages:
        pass


def _pipelined(tasks, lag):
    started = []
    for task in list(tasks) + [None] * lag:
        if task is not None:
            next(task)
            started.append(task)
        if len(started) > lag or task is None:
            _run(started.pop(0))
            yield


def _interleave(major, n_major, minor, n_minor):
    done = 0
    for i in range(n_major):
        while done < n_minor and done * n_major < (i + 1) * n_minor:
            next(minor)
            done += 1
        next(major)
    _run(minor)
    _run(major)


def _silu(x):
    half = 0.5 * x
    return half + half * jnp.tanh(half)


def _log_sigmoid(x):
    return jnp.minimum(x, 0.0) - jnp.log(1.0 + jnp.exp(-jnp.abs(x)))


def _alibi_slope(h):
    return math.pow(2.0, -8.0 * (h + 1) / N_Q_HEADS)


HEAD_LAG = 1
GROUP_ROWS = GQA_GROUP * CHUNK


def _fill_alibi(bias_ref):
    s = lax.broadcasted_iota(jnp.int32, (KEYS, CHUNK), 0)
    t = lax.broadcasted_iota(jnp.int32, (KEYS, CHUNK), 1)
    dist = jnp.abs(WINDOW + t - s).astype(F32)
    for kh in range(N_KV_HEADS):
        for g in range(GQA_GROUP):
            bias_ref[kh, :, g * CHUNK:(g + 1) * CHUNK] = dist * (-_alibi_slope(kh * GQA_GROUP + g))


def _attention_chunk(r, qa_ref, za_ref, kwin, vwin, sinks_ref, layer, bias_ref, key_lo, xa_ref):
    rows = pl.ds(r, CHUNK)
    lane_head = lax.broadcasted_iota(jnp.int32, (1, GROUP_ROWS), 1) // CHUNK
    if key_lo is not None:
        key_ok = lax.broadcasted_iota(jnp.int32, (KEYS, GROUP_ROWS), 0) >= key_lo

    def head(kh):
        k = kwin(pl.ds(kh * HEAD_DIM, HEAD_DIM))
        v = vwin(pl.ds(kh * HEAD_DIM, HEAD_DIM))
        q = jnp.concatenate(
            [qa_ref[rows, pl.ds((kh * GQA_GROUP + g) * HEAD_DIM, HEAD_DIM)] for g in range(GQA_GROUP)],
            axis=0)
        q = q * jnp.asarray(HEAD_DIM ** -0.5, BF16)
        s = _nt_dot(k, q) + bias_ref[kh]
        if key_lo is not None:
            s = jnp.where(key_ok, s, -jnp.inf)
        sink = jnp.zeros((1, GROUP_ROWS), F32)
        for g in range(GQA_GROUP):
            sink = jnp.where(lane_head == g, sinks_ref[layer, kh * GQA_GROUP + g], sink)
        m = jnp.maximum(jnp.max(s, axis=0, keepdims=True), sink)
        p = jnp.exp(s - m)
        den = jnp.sum(p, axis=0, keepdims=True) + jnp.exp(sink - m)
        p = p.astype(BF16)
        yield
        o_t = lax.dot_general(v, p, (((0,), (0,)), ((), ())), preferred_element_type=F32)
        o = jnp.transpose(o_t / den)
        for g in range(GQA_GROUP):
            h = kh * GQA_GROUP + g
            cols = pl.ds(h * HEAD_DIM, HEAD_DIM)
            z = za_ref[rows, cols].astype(F32)
            xa_ref[rows, cols] = (o[g * CHUNK:(g + 1) * CHUNK] * _silu(z)).astype(BF16)
        yield

    return [head(kh) for kh in range(N_KV_HEADS)]


def _gla_prepare(lr_ref, kg_ref, wfu_ref, bf_ref, k_scr, b_scr):
    n = lr_ref.shape[0]
    la2 = _log2_forget(lr_ref[...], wfu_ref, bf_ref)
    row = lax.broadcasted_iota(jnp.int32, (n, n), 0)
    col = lax.broadcasted_iota(jnp.int32, (n, n), 1)
    tri = jnp.where(row // CHUNK == col // CHUNK, row - col, -1) >= 0
    tri = jnp.where(tri, 1.0, 0.0).astype(BF16)
    la_hi = la2.astype(BF16)
    rem = la2 - la_hi.astype(F32)
    la_mid = rem.astype(BF16)
    la_lo = (rem - la_mid.astype(F32)).astype(BF16)
    b_scr[...] = (jnp.dot(tri, la_hi, preferred_element_type=F32)
                  + jnp.dot(tri, la_mid, preferred_element_type=F32)
                  + jnp.dot(tri, la_lo, preferred_element_type=F32))
    k_scr[...] = kg_ref[...].astype(F32)


def _gla_states(chunk_states, vg_ref, k_scr, b_scr, s0_scr):
    for hh in range(GLA_HEADS):
        kcols = pl.ds(hh * GLA_HK, GLA_HK)
        vcols = pl.ds(hh * GLA_HV, GLA_HV)
        steps = []
        for c in range(len(chunk_states)):
            rows = pl.ds(c * CHUNK, CHUNK)
            b_end = b_scr[c * CHUNK + CHUNK - 1:(c + 1) * CHUNK, kcols]
            k_til = (k_scr[rows, kcols] * jnp.exp2(b_end - b_scr[rows, kcols])).astype(BF16)
            upd = lax.dot_general(k_til, vg_ref[rows, vcols], (((0,), (0,)), ((), ())),
                                  preferred_element_type=F32)
            e_col = jnp.transpose(jnp.broadcast_to(jnp.exp2(b_end), (GLA_HK, GLA_HK)))
            steps.append((jnp.concatenate([e_col, e_col], axis=1), upd))
            yield
        s = None
        for c, ((in_ref, out_ref), (decay, upd)) in enumerate(zip(chunk_states, steps)):
            if c == 0 or in_ref is not chunk_states[c - 1][1]:
                s = in_ref[hh]
            s0_scr[c, hh] = s.astype(BF16)
            s = s * decay + upd
            if c + 1 == len(chunk_states) or chunk_states[c + 1][0] is not out_ref:
                out_ref[hh] = s
        yield


def _gla_chunk(c, qg_ref, vg_ref, zg_ref, s0_scr, gnorm, k_scr, b_scr, xg_ref):
    r = c * CHUNK
    rows = pl.ds(r, CHUNK)
    lane = lax.broadcasted_iota(jnp.int32, (SUB, CHUNK), 1)
    row0 = lax.broadcasted_iota(jnp.int32, (SUB, CHUNK), 0)

    def head(hh):
        kcols = pl.ds(hh * GLA_HK, GLA_HK)
        vcols = pl.ds(hh * GLA_HV, GLA_HV)
        q = qg_ref[rows, kcols].astype(F32) * (GLA_HK ** -0.5)
        k = k_scr[rows, kcols]
        v = vg_ref[rows, vcols]
        b = b_scr[rows, kcols]
        blk = lambda x, j: x[j * SUB:(j + 1) * SUB]
        b_row = lambda t: b_scr[r + t:r + t + 1, kcols]
        b_last = [b_row(j * SUB + SUB - 1) for j in range(NSUB)]

        k_hat = jnp.concatenate([blk(k, j) * jnp.exp2(b_last[j] - blk(b, j)) for j in range(NSUB)], axis=0)
        lhs = jnp.concatenate([q[(j + 1) * SUB:] * jnp.exp2(b[(j + 1) * SUB:] - b_last[j])
                               for j in range(NSUB - 1)], axis=0)
        rr = _nt_dot(lhs.astype(BF16), k_hat.astype(BF16))
        o_inter = jnp.dot((q * jnp.exp2(b)).astype(BF16), s0_scr[c, hh], preferred_element_type=F32)
        yield

        score_rows = []
        for i in range(NSUB):
            acc = jnp.zeros((SUB, CHUNK), F32)
            for j in range(i):
                base = sum((NSUB - 1 - jj) * SUB for jj in range(j)) + (i - j - 1) * SUB
                acc = jnp.where(lane // SUB == j, rr[base:base + SUB], acc)
            qi, bi = blk(q, i), blk(b, i)
            for s in range(i * SUB, (i + 1) * SUB):
                prod = (qi * k_scr[r + s:r + s + 1, kcols]) * jnp.exp2(bi - b_row(s))
                acc = jnp.where(lane == s, jnp.sum(prod, axis=1, keepdims=True), acc)
            score_rows.append(jnp.where(lane <= row0 + i * SUB, acc, 0.0))
        scores = jnp.concatenate(score_rows, axis=0).astype(BF16)

        o = jnp.dot(scores, v, preferred_element_type=F32) + o_inter

        y = _rms(o, gnorm)
        z = zg_ref[rows, vcols].astype(F32)
        xg_ref[rows, vcols] = (y * _silu(z)).astype(BF16)
        yield

    return [head(hh) for hh in range(GLA_HEADS)]


def _log2_forget(lr, wfu_ref, bf_ref):
    x = jnp.dot(lr, wfu_ref[...], preferred_element_type=F32) + bf_ref[...]
    return _log_sigmoid(x) * (LOG2E / GLA_TAU)


def _sigmoid(x):
    return 1.0 / (1.0 + jnp.exp(-x))


def _swap_mixer_outputs(s, xa_buf, xg_buf):
    @pl.when(s == 0)
    def _():
        xa_buf[1] = jnp.zeros(xa_buf.shape[1:], BF16)
        xg_buf[1] = jnp.zeros(xg_buf.shape[1:], BF16)

    return xa_buf.at[s % 2], xg_buf.at[s % 2], xa_buf.at[1 - s % 2], xg_buf.at[1 - s % 2]


N_OUT_STAGES = 3 * (D_MODEL // OUT_COLS) + 1


def _out_stages(xa, xg, ga_ref, gg_ref, h_in_ref, wa_ref, wg_ref, wo_ref, bg_ref, npost_ref, npre_ref,
                merged_scr, z_scr, out_refs):
    bg = bg_ref[...]
    for n in range(D_MODEL // OUT_COLS):
        cols = slice(n * OUT_COLS, (n + 1) * OUT_COLS)
        ya = jnp.dot(xa[...], wa_ref[:, cols], preferred_element_type=F32)
        yield
        yg = jnp.dot(xg[...], wg_ref[:, cols], preferred_element_type=F32)
        gate_a = _sigmoid(ga_ref[:, cols].astype(F32) + bg[:, n * OUT_COLS:(n + 1) * OUT_COLS])
        gate_g = _sigmoid(gg_ref[:, cols].astype(F32)
                          + bg[:, D_MODEL + n * OUT_COLS:D_MODEL + (n + 1) * OUT_COLS])
        merged_scr[:, cols] = (gate_a * ya + gate_g * yg).astype(BF16)
        yield
    for n in range(D_MODEL // OUT_COLS):
        cols = slice(n * OUT_COLS, (n + 1) * OUT_COLS)
        z_scr[:, cols] = jnp.dot(merged_scr[...], wo_ref[:, cols], preferred_element_type=F32)
        yield
    h = h_in_ref[...] + _rms(z_scr[...], npost_ref[...])
    out_refs[0][...] = h
    if npre_ref is not None:
        out_refs[1][...] = _rms(h, npre_ref[...]).astype(BF16)
    yield


def _mix_out_prompt_kernel(layer, n_steps, chunks, *refs):
    last = layer == DEPTH - 1
    n_carried = 3 if layer > 0 else 0
    (qa_ref, za_ref, vg_ref, zg_ref, qg_ref, kg_ref, ka_ref, va_ref, lr_ref,
     wfu_ref, bf_ref, gn_ref, sinks_ref,
     ga_ref, gg_ref, h_in_ref, wa_ref, wg_ref, wo_ref, bg_ref, npost_ref, npre_ref) = refs[:22]
    refs = refs[22 + n_carried:]
    out_refs, refs = refs[:1 if last else 2], refs[1 if last else 2:]
    (st_ref, knew_ref, vnew_ref,
     halo_k, halo_v, state, bias_scr, k_scr, b_scr, s0_scr, xa_buf, xg_buf, merged_scr, z_scr) = refs
    s = pl.program_id(0)
    b = s // n_steps
    t = s % n_steps
    assert chunks * CHUNK >= WINDOW
    xa_ref, xg_ref, xa_prev, xg_prev = _swap_mixer_outputs(s, xa_buf, xg_buf)
    out_stages = functools.partial(
        _out_stages, xa_prev, xg_prev, ga_ref, gg_ref, h_in_ref, wa_ref, wg_ref, wo_ref, bg_ref, npost_ref,
        None if last else npre_ref, merged_scr, z_scr, out_refs)

    cur = t % 2
    nxt = 1 - cur

    @pl.when((b == 0) & (t == 0))
    def _():
        _fill_alibi(bias_scr)

    @pl.when(t == 0)
    def _():
        halo_k[0] = jnp.zeros((WINDOW, KV_WIDTH), BF16)
        halo_v[0] = jnp.zeros((WINDOW, KV_WIDTH), BF16)
        state[...] = jnp.zeros(state.shape, F32)

    def window(halo, tile_ref, c, cols):
        lo = c * CHUNK - WINDOW
        if lo >= 0:
            return tile_ref[lo:lo + KEYS, cols]
        return jnp.concatenate([halo[cur, WINDOW + lo:, cols], tile_ref[0:(c + 1) * CHUNK, cols]], axis=0)

    def mix_stages():
        _gla_prepare(lr_ref, kg_ref, wfu_ref, bf_ref, k_scr, b_scr)
        yield
        yield from _gla_states([(state, state)] * chunks, vg_ref, k_scr, b_scr, s0_scr)
        gnorm = gn_ref[...]
        heads = []
        for c in range(chunks):
            key_lo = (WINDOW // CHUNK - (t * chunks + c)) * CHUNK if c < WINDOW // CHUNK else None
            heads += _attention_chunk(c * CHUNK, qa_ref, za_ref, functools.partial(window, halo_k, ka_ref, c),
                                      functools.partial(window, halo_v, va_ref, c), sinks_ref, layer, bias_scr,
                                      key_lo, xa_ref)
            heads += _gla_chunk(c, qg_ref, vg_ref, zg_ref, s0_scr, gnorm, k_scr, b_scr, xg_ref)
        yield from _pipelined(heads, HEAD_LAG)
        halo_k[nxt] = ka_ref[chunks * CHUNK - WINDOW:, :]
        halo_v[nxt] = va_ref[chunks * CHUNK - WINDOW:, :]

    n_mix = 1 + GLA_HEADS * (chunks + 1) + chunks * (N_KV_HEADS + GLA_HEADS)
    _interleave(mix_stages(), n_mix, out_stages(), N_OUT_STAGES)

    @pl.when(t == n_steps - 1)
    def _():
        st_ref[0] = state[...]
        knew_ref[0] = ka_ref[chunks * CHUNK - WINDOW:, :].astype(F32)
        vnew_ref[0] = va_ref[chunks * CHUNK - WINDOW:, :].astype(F32)


def _mix_out_sample_kernel(layer, n_seqs, *refs):
    last = layer == DEPTH - 1
    n_untouched = (3 if layer > 0 else 0) + (0 if last else 2)
    (qa_ref, za_ref, vg_ref, zg_ref, qg_ref, kg_ref, ka_ref, va_ref, lr_ref,
     wfu_ref, bf_ref, gn_ref, sinks_ref, ck_ref, cv_ref, s0_ref,
     ga_ref, gg_ref, h_in_ref, wa_ref, wg_ref, wo_ref, bg_ref, npost_ref, npre_ref) = refs[:25]
    refs = refs[25 + n_untouched:]
    out_refs, refs = refs[:1 if last else 2], refs[1 if last else 2:]
    (st_ref, knew_ref, vnew_ref,
     bias_scr, k_scr, b_scr, s0_scr, xa_buf, xg_buf, merged_scr, z_scr) = refs
    s = pl.program_id(0)
    xa_ref, xg_ref, xa_prev, xg_prev = _swap_mixer_outputs(s, xa_buf, xg_buf)
    out_stages = _out_stages(xa_prev, xg_prev, ga_ref, gg_ref, h_in_ref, wa_ref, wg_ref, wo_ref, bg_ref,
                             npost_ref, None if last else npre_ref, merged_scr, z_scr, out_refs)

    @pl.when(s == 0)
    def _():
        _fill_alibi(bias_scr)

    def window(cache_ref, new_ref, g, cols):
        return jnp.concatenate([cache_ref[g, :, cols].astype(BF16), new_ref[g * CHUNK:(g + 1) * CHUNK, cols]], axis=0)

    def mix_stages():
        _gla_prepare(lr_ref, kg_ref, wfu_ref, bf_ref, k_scr, b_scr)
        yield
        yield from _gla_states([(s0_ref.at[g], st_ref.at[g]) for g in range(n_seqs)], vg_ref, k_scr, b_scr, s0_scr)
        gnorm = gn_ref[...]
        heads = []
        for g in range(n_seqs):
            r = g * CHUNK
            knew_ref[g, 0:WINDOW - CHUNK, :] = ck_ref[g, CHUNK:, :]
            knew_ref[g, WINDOW - CHUNK:, :] = ka_ref[r:r + CHUNK, :].astype(F32)
            vnew_ref[g, 0:WINDOW - CHUNK, :] = cv_ref[g, CHUNK:, :]
            vnew_ref[g, WINDOW - CHUNK:, :] = va_ref[r:r + CHUNK, :].astype(F32)
            heads += _attention_chunk(r, qa_ref, za_ref, functools.partial(window, ck_ref, ka_ref, g),
                                      functools.partial(window, cv_ref, va_ref, g), sinks_ref, layer, bias_scr,
                                      None, xa_ref)
            heads += _gla_chunk(g, qg_ref, vg_ref, zg_ref, s0_scr, gnorm, k_scr, b_scr, xg_ref)
        yield from _pipelined(heads, HEAD_LAG)

    n_mix = 1 + GLA_HEADS * (n_seqs + 1) + n_seqs * (N_KV_HEADS + GLA_HEADS)
    _interleave(mix_stages(), n_mix, out_stages, N_OUT_STAGES)


def _section_specs(rows, row_index):
    def spec(width, col):
        return pl.BlockSpec((rows, width), functools.partial(
            lambda cb, *g: (row_index(*g), cb), col // width))
    return [spec(ATTN_WIDTH, COL_QA), spec(ATTN_WIDTH, COL_ZA), spec(GLA_DV, COL_VG), spec(GLA_DV, COL_ZG),
            spec(GLA_DK, COL_QG), spec(GLA_DK, COL_KG), spec(KV_WIDTH, COL_KA), spec(KV_WIDTH, COL_VA),
            spec(LR_PAD, 0)]


def _layer_weight_specs(layer):
    zeros = (0,) * 2
    return [pl.BlockSpec((None, LR_PAD, GLA_DK), lambda *g: (layer,) + zeros),
            pl.BlockSpec((None, 1, GLA_DK), lambda *g: (layer,) + zeros),
            pl.BlockSpec((None, 1, GLA_HV), lambda *g: (layer,) + zeros),
            pl.BlockSpec(memory_space=pltpu.SMEM)]


def _stacked_outputs(layer, batch, seqs_per_step, batch_block, n_inputs, first_out, carried):
    lead = lambda *g: (layer, batch_block(*g))
    nb = seqs_per_step
    specs = [pl.BlockSpec((None, nb, GLA_HEADS, GLA_HK, GLA_HV), lambda *g: lead(*g) + (0, 0, 0)),
             pl.BlockSpec((None, nb, WINDOW, KV_WIDTH), lambda *g: lead(*g) + (0, 0)),
             pl.BlockSpec((None, nb, WINDOW, KV_WIDTH), lambda *g: lead(*g) + (0, 0))]
    shapes = [jax.ShapeDtypeStruct((DEPTH, batch, GLA_HEADS, GLA_HK, GLA_HV), F32),
              jax.ShapeDtypeStruct((DEPTH, batch, WINDOW, KV_WIDTH), F32),
              jax.ShapeDtypeStruct((DEPTH, batch, WINDOW, KV_WIDTH), F32)]
    if carried is None:
        return specs, shapes, [], {}
    alias_specs = [pl.BlockSpec(memory_space=pl.ANY)] * 3
    aliases = {n_inputs + k: first_out + k for k in range(3)}
    return specs, shapes, alias_specs, aliases


def _out_weight_specs(layer):
    wspec = lambda k: pl.BlockSpec((None, k, D_MODEL), lambda *g: (layer, 0, 0), pipeline_mode=pl.Buffered(1))
    vspec = lambda n, l: pl.BlockSpec((None, 1, n), lambda *g: (l, 0, 0))
    return [wspec(ATTN_WIDTH), wspec(GLA_DV), wspec(D_MODEL),
            vspec(2 * D_MODEL, layer), vspec(D_MODEL, layer), vspec(D_MODEL, min(layer + 1, DEPTH - 1))]


def _mix_out_prompt(p, p_lr, h_in, n_total, wfu, bfg, gn, sinks, wa, wg, wo, bg, npost, npre,
                    layer, batch, seq, carried):
    last = layer == DEPTH - 1
    chunks = PROMPT_CHUNKS_PER_STEP
    rows = chunks * CHUNK
    n_steps = seq // rows
    n_tiles = batch * n_steps
    mix_tile = lambda s: jnp.minimum(s, n_tiles - 1)
    out_tile = lambda s: jnp.maximum(s - 1, 0)
    tile = lambda col: pl.BlockSpec((rows, D_MODEL), lambda s: (out_tile(s), col))
    in_specs = (_section_specs(rows, mix_tile) + _layer_weight_specs(layer)
                + [tile(COL_GA // D_MODEL), tile(COL_GG // D_MODEL), tile(0)] + _out_weight_specs(layer))
    n_main = 1 if last else 2
    st_specs, st_shapes, alias_specs, aliases = _stacked_outputs(
        layer, batch, 1, lambda s: jnp.minimum(s // n_steps, batch - 1), len(in_specs), n_main, carried)
    if last:
        out_shape = [jax.ShapeDtypeStruct((n_tiles * rows, D_MODEL), F32)]
    else:
        out_shape = [jax.ShapeDtypeStruct((n_total, D_MODEL), F32), jax.ShapeDtypeStruct((n_total, D_MODEL), BF16)]
    outs = pl.pallas_call(
        functools.partial(_mix_out_prompt_kernel, layer, n_steps, chunks),
        grid=(n_tiles + 1,),
        in_specs=in_specs + alias_specs,
        out_specs=[tile(0)] * n_main + st_specs,
        out_shape=out_shape + st_shapes,
        input_output_aliases=aliases,
        scratch_shapes=[pltpu.VMEM((2, WINDOW, KV_WIDTH), BF16),
                        pltpu.VMEM((2, WINDOW, KV_WIDTH), BF16),
                        pltpu.VMEM((GLA_HEADS, GLA_HK, GLA_HV), F32),
                        pltpu.VMEM((N_KV_HEADS, KEYS, GROUP_ROWS), F32),
                        pltpu.VMEM((rows, GLA_DK), F32),
                        pltpu.VMEM((rows, GLA_DK), F32),
                        pltpu.VMEM((rows // CHUNK, GLA_HEADS, GLA_HK, GLA_HV), BF16),
                        pltpu.VMEM((2, rows, ATTN_WIDTH), BF16),
                        pltpu.VMEM((2, rows, GLA_DV), BF16),
                        pltpu.VMEM((rows, D_MODEL), BF16),
                        pltpu.VMEM((rows, D_MODEL), F32)],
        compiler_params=pltpu.CompilerParams(
            dimension_semantics=("arbitrary",), vmem_limit_bytes=VMEM_LIMIT),
        name="mix_out_prompt",
    )(*([p] * 8), p_lr, wfu, bfg, gn, sinks, p, p, h_in, wa, wg, wo, bg, npost, npre, *(carried or ()))
    return tuple(outs[:n_main]), tuple(outs[n_main:])


def _mix_out_sample(p, p_lr, h_in, row0_h, new_h_u, wfu, bfg, gn, sinks, cache_k, cache_v, state,
                    wa, wg, wo, bg, npost, npre, layer, batch, carried):
    last = layer == DEPTH - 1
    nb = SAMPLE_SEQS_PER_STEP
    rows = nb * CHUNK
    n_tiles = batch // nb
    row0 = p.shape[0] - batch * CHUNK
    assert row0 % rows == 0 and row0_h % rows == 0 and batch % nb == 0
    blk0, blk0_h = row0 // rows, row0_h // rows
    mix_tile = lambda s: jnp.minimum(s, n_tiles - 1)
    out_tile = lambda s: jnp.maximum(s - 1, 0)
    tile = lambda base, col: pl.BlockSpec((rows, D_MODEL), lambda s: (base + out_tile(s), col))
    cache_spec = pl.BlockSpec((None, nb, WINDOW, KV_WIDTH), lambda s: (layer, mix_tile(s), 0, 0))
    state_spec = pl.BlockSpec((None, nb, GLA_HEADS, GLA_HK, GLA_HV), lambda s: (layer, mix_tile(s), 0, 0, 0))
    in_specs = (_section_specs(rows, lambda s: blk0 + mix_tile(s)) + _layer_weight_specs(layer)
                + [cache_spec, cache_spec, state_spec]
                + [tile(blk0, COL_GA // D_MODEL), tile(blk0, COL_GG // D_MODEL), tile(blk0_h, 0)]
                + _out_weight_specs(layer))
    n_main = 1 if last else 2
    st_specs, st_shapes, alias_specs, aliases = _stacked_outputs(
        layer, batch, nb, mix_tile, len(in_specs), n_main, carried)
    if last:
        extra, out_specs = (), [tile(0, 0)]
        out_shape = [jax.ShapeDtypeStruct((batch * CHUNK, D_MODEL), F32)]
    else:
        extra, out_specs = tuple(new_h_u), [tile(blk0, 0), tile(blk0, 0)]
        out_shape = [jax.ShapeDtypeStruct(a.shape, a.dtype) for a in new_h_u]
        first_extra = len(in_specs) + len(alias_specs)
        aliases = {**aliases, first_extra: 0, first_extra + 1: 1}
    outs = pl.pallas_call(
        functools.partial(_mix_out_sample_kernel, layer, nb),
        grid=(n_tiles + 1,),
        in_specs=in_specs + alias_specs + [pl.BlockSpec(memory_space=pl.ANY)] * len(extra),
        out_specs=out_specs + st_specs,
        out_shape=out_shape + st_shapes,
        input_output_aliases=aliases,
        scratch_shapes=[pltpu.VMEM((N_KV_HEADS, KEYS, GROUP_ROWS), F32),
                        pltpu.VMEM((rows, GLA_DK), F32),
                        pltpu.VMEM((rows, GLA_DK), F32),
                        pltpu.VMEM((rows // CHUNK, GLA_HEADS, GLA_HK, GLA_HV), BF16),
                        pltpu.VMEM((2, rows, ATTN_WIDTH), BF16),
                        pltpu.VMEM((2, rows, GLA_DV), BF16),
                        pltpu.VMEM((rows, D_MODEL), BF16),
                        pltpu.VMEM((rows, D_MODEL), F32)],
        compiler_params=pltpu.CompilerParams(
            dimension_semantics=("arbitrary",), vmem_limit_bytes=VMEM_LIMIT),
        name="mix_out_sample",
    )(*([p] * 8), p_lr, wfu, bfg, gn, sinks, cache_k, cache_v, state, p, p, h_in, wa, wg, wo, bg, npost, npre,
      *(carried or ()), *extra)
    return tuple(outs[:n_main]), tuple(outs[n_main:])


def kernel(x_prompt, x_sample, cache_k, cache_v, state_gla, norm_pre, norm_post, w_in, b_gate,
           attn_sinks, w_forget_up, b_forget, gla_norm, w_branch_attn, w_branch_gla, w_out):
    batch, seq, _ = x_prompt.shape
    dec_batch, dec_seq, _ = x_sample.shape
    assert dec_seq == CHUNK and seq % (PROMPT_CHUNKS_PER_STEP * CHUNK) == 0
    assert cache_k.shape[2] == WINDOW
    n_prompt = batch * seq
    n_sample = dec_batch * dec_seq

    w_in_t = jnp.swapaxes(w_in, 1, 2)
    wa =w_branch_attn.astype(BF16)
    wg = w_branch_gla.astype(BF16)
    wo = w_out.astype(BF16)
    wfu = jnp.pad(w_forget_up, ((0, 0), (0, LR_PAD - GLA_RANK), (0, 0))).astype(BF16)
    bfg = b_forget.reshape(DEPTH, 1, GLA_DK)
    gn = gla_norm.reshape(DEPTH, 1, GLA_HV)
    bg = b_gate.reshape(DEPTH, 1, 2 * D_MODEL)
    npre = norm_pre.reshape(DEPTH, 1, D_MODEL)
    npost = norm_post.reshape(DEPTH, 1, D_MODEL)
    ck = cache_k.reshape(DEPTH, dec_batch, WINDOW, KV_WIDTH)
    cv = cache_v.reshape(DEPTH, dec_batch, WINDOW, KV_WIDTH)

    xp = x_prompt.reshape(n_prompt, D_MODEL)
    xs = x_sample.reshape(n_sample, D_MODEL)
    u = _prenorm(xp, xs, npre[0])
    h_prompt, h_sample, sample_row0 = xp, xs, 0

    stacked_p = stacked_s = None
    for l in range(DEPTH):
        p, p_lr = _project(u, w_in_t, l)
        out_p, stacked_p = _mix_out_prompt(p, p_lr, h_prompt, n_prompt + n_sample, wfu, bfg, gn, attn_sinks,
                                           wa, wg, wo, bg, npost, npre, l, batch, seq, stacked_p)
        out_s, stacked_s = _mix_out_sample(p, p_lr, h_sample, sample_row0, out_p, wfu, bfg, gn, attn_sinks,
                                           ck, cv, state_gla, wa, wg, wo, bg, npost, npre, l, dec_batch, stacked_s)
        if l < DEPTH - 1:
            h_prompt = h_sample = out_s[0]
            u = out_s[1]
            sample_row0 = n_prompt

    y_prompt = out_p[0].reshape(batch, seq, D_MODEL)
    y_sample = out_s[0].reshape(dec_batch, dec_seq, D_MODEL)
    heads = lambda a: a.reshape(a.shape[:3] + (N_KV_HEADS, HEAD_DIM))
    return (y_prompt, y_sample, heads(stacked_p[1]), heads(stacked_p[2]), stacked_p[0],
            heads(stacked_s[1]), heads(stacked_s[2]), stacked_s[0])
```
